```python
import jax
import jax.numpy as jnp
from jax import lax
import numpy as np

D_MODEL = 1024
BATCH = 2
SEQ = 8192
DEPTH = 2
DEC_BATCH = 128
DEC_SEQ = 4
PAST_LEN = 8192
PAGE_SIZE = 128

ATT_HEADS = 8
ATT_KV_HEADS = 2
ATT_GROUP = ATT_HEADS // ATT_KV_HEADS
HEAD_DIM = 64
ROT_DIM = HEAD_DIM // 4
ROPE_THETA = 500000.0
WINDOW = 128
CONV_CH = 512
CONV_WIDTH = 31
LN_EPS = 1e-5
RWKV_HEADS = 8
RWKV_HEAD = 64
RWKV_W = RWKV_HEADS * RWKV_HEAD
DECAY_LORA = 64
ICLR_LORA = 64
GATE_LORA = 128
GN_EPS = 64e-5
N_BRANCH = 3
CA_HEADS = 4
CA_HEAD_DIM = D_MODEL // CA_HEADS
MEM_LEN = 256
PEER_HEADS = 8
N_KEYS = 128
N_EXPERTS = N_KEYS * N_KEYS
PEER_TOPK = 16
PEER_QDIM = 256
PEER_HALF = PEER_QDIM // 2
PEER_BLOCK = 128
RMS_EPS = 1e-6
NEG = -1e30

Q_W = ATT_HEADS * HEAD_DIM
KV_W = ATT_KV_HEADS * HEAD_DIM
A_COLS = Q_W + 2 * KV_W
B_COLS = 2 * CONV_CH
C_COLS = 3 * RWKV_W + DECAY_LORA + ICLR_LORA + GATE_LORA
G_COLS = N_BRANCH * D_MODEL
IN_COLS = A_COLS + B_COLS + C_COLS + G_COLS

kernel_name = 'hybrid_swa_conformer_rwkv7_peer_step'


def rmsnorm(x, g):
    xf = x.astype(jnp.float32)
    y = xf * lax.rsqrt(jnp.mean(xf * xf, axis=-1, keepdims=True) + RMS_EPS)
    return (y * g.astype(jnp.float32)).astype(x.dtype)


def rotary(x, pos):
    half = ROT_DIM // 2
    inv = ROPE_THETA ** (-2.0 * jnp.arange(half, dtype=jnp.float32) / ROT_DIM)
    ang = pos[:, None] * inv[None, :]
    cos = jnp.cos(ang)[None, :, None, :]
    sin = jnp.sin(ang)[None, :, None, :]
    xf = x.astype(jnp.float32)
    x1 = xf[..., :half]
    x2 = xf[..., half:ROT_DIM]
    out = jnp.concatenate([x1 * cos - x2 * sin, x2 * cos + x1 * sin, xf[..., ROT_DIM:]], axis=-1)
    return out.astype(x.dtype)


def sink_attention(q, k, v, mask, sinks):
    s = jnp.einsum('bnqkgd,bnskd->bnkgqs', q, k).astype(jnp.float32) * (HEAD_DIM ** -0.5)
    s = jnp.where(mask[None, :, None, None], s, NEG)
    sk = sinks.astype(jnp.float32).reshape(1, 1, ATT_KV_HEADS, ATT_GROUP, 1, 1)
    m = jnp.maximum(jnp.max(s, axis=-1, keepdims=True), sk)
    p = jnp.exp(s - m)
    den = jnp.sum(p, axis=-1, keepdims=True) + jnp.exp(sk - m)
    return jnp.einsum('bnkgqs,bnskd->bnqkgd', (p / den).astype(v.dtype), v)


def window_attn_prompt(q, k, v, sinks):
    B, T = q.shape[0], q.shape[1]
    nb = T // WINDOW
    qb = q.reshape(B, nb, WINDOW, ATT_KV_HEADS, ATT_GROUP, HEAD_DIM)
    kb = k.reshape(B, nb, WINDOW, ATT_KV_HEADS, HEAD_DIM)
    vb = v.reshape(B, nb, WINDOW, ATT_KV_HEADS, HEAD_DIM)
    pad = jnp.zeros_like(kb[:, :1])
    kk = jnp.concatenate([jnp.concatenate([pad, kb[:, :-1]], axis=1), kb], axis=2)
    vv = jnp.concatenate([jnp.concatenate([pad, vb[:, :-1]], axis=1), vb], axis=2)
    qi = jnp.arange(WINDOW)[:, None]
    kj = jnp.arange(2 * WINDOW)[None, :]
    diff = WINDOW + qi - kj
    band = (diff >= 0) & (diff < WINDOW)
    valid = (jnp.arange(nb)[:, None, None] * WINDOW + kj[None] - WINDOW) >= 0
    mask = band[None] & valid
    o = sink_attention(qb, kk, vv, mask, sinks)
    return o.reshape(B, T, Q_W)


def window_attn_sample(q, k, v, buf_k, buf_v, sinks):
    DB, T = q.shape[0], q.shape[1]
    wb = buf_k.shape[1]
    kk = jnp.concatenate([buf_k.astype(k.dtype), k], axis=1)
    vv = jnp.concatenate([buf_v.astype(v.dtype), v], axis=1)
    qpos = PAST_LEN + jnp.arange(T)
    kpos = PAST_LEN - wb + jnp.arange(wb + T)
    diff = qpos[:, None] - kpos[None, :]
    mask = ((diff >= 0) & (diff < WINDOW))[None]
    o = sink_attention(q.reshape(DB, 1, T, ATT_KV_HEADS, ATT_GROUP, HEAD_DIM), kk[:, None], vv[:, None], mask, sinks)
    return o.reshape(DB, T, Q_W), kk[:, T:], vv[:, T:]


def conformer_conv(u_pair, buf, dw, db, ln_g, ln_b):
    u = u_pair[..., :CONV_CH] * jax.nn.sigmoid(u_pair[..., CONV_CH:])
    ext = jnp.concatenate([buf.astype(u.dtype), u], axis=1)
    y = lax.conv_general_dilated(ext, dw[:, None, :].astype(ext.dtype), window_strides=(1,), padding='VALID',
                                 dimension_numbers=('NWC', 'WIO', 'NWC'), feature_group_count=CONV_CH) + db
    yf = y.astype(jnp.float32)
    mu = jnp.mean(yf, axis=-1, keepdims=True)
    var = jnp.mean(jnp.square(yf - mu), axis=-1, keepdims=True)
    yn = (yf - mu) * lax.rsqrt(var + LN_EPS) * ln_g.astype(jnp.float32) + ln_b.astype(jnp.float32)
    return jax.nn.silu(yn).astype(u.dtype), ext[:, ext.shape[1] - (CONV_WIDTH - 1):]


def rwkv_time_mix(pc, shift, wkv0, lp):
    f32 = jnp.float32
    B, T = pc.shape[0], pc.shape[1]
    prev = jnp.concatenate([shift[:, None, :].astype(pc.dtype), pc[:, :-1]], axis=1)
    xs = pc + (prev - pc) * lp['rk_mu']
    o1, o2, o3 = RWKV_W, 2 * RWKV_W, 3 * RWKV_W
    o4 = o3 + DECAY_LORA
    o5 = o4 + ICLR_LORA
    r, k, v = xs[..., :o1], xs[..., o1:o2], xs[..., o2:o3]
    wd, ad, gd = xs[..., o3:o4], xs[..., o4:o5], xs[..., o5:]
    w_log = -jax.nn.softplus(-(lp['rk_w0'] + jnp.tanh(wd) @ lp['rk_w2']).astype(f32)) - 0.5
    decay = jnp.exp(-jnp.exp(w_log))
    a = jax.nn.sigmoid((lp['rk_a0'] + ad @ lp['rk_a2']).astype(f32))
    g = (jax.nn.sigmoid(gd) @ lp['rk_g2']).astype(f32)

    def heads(t):
        return t.reshape(B, T, RWKV_HEADS, RWKV_HEAD)

    kk = heads((k * lp['rk_kk']).astype(f32))
    kk = kk / jnp.maximum(jnp.linalg.norm(kk, axis=-1, keepdims=True), 1e-12)
    k_mod = k.astype(f32) * (1.0 + (a - 1.0) * lp['rk_ka'].astype(f32))
    rh, kh, vh, wh, ah = heads(r.astype(f32)), heads(k_mod), heads(v.astype(f32)), heads(decay), heads(a)

    def step(S, inp):
        r_t, w_t, k_t, v_t, a_t, b_t = inp
        Sa = jnp.einsum('bhvk,bhk->bhv', S, a_t)
        S = S * w_t[:, :, None, :] + Sa[..., None] * b_t[:, :, None, :] + v_t[..., None] * k_t[:, :, None, :]
        return S, jnp.einsum('bhvk,bhk->bhv', S, r_t)

    seq_in = (rh, wh, kh, vh, -kk, kk * ah)
    S_fin, o = lax.scan(step, wkv0.astype(f32), tuple(jnp.swapaxes(t, 0, 1) for t in seq_in))
    o = jnp.swapaxes(o, 0, 1)
    mu = jnp.mean(o, axis=-1, keepdims=True)
    var = jnp.mean(jnp.square(o - mu), axis=-1, keepdims=True)
    on = ((o - mu) * lax.rsqrt(var + GN_EPS)).reshape(B, T, RWKV_W) * lp['rk_gn_g'].astype(f32) + lp['rk_gn_b'].astype(f32)
    bonus = jnp.sum(rh * kh * lp['rk_rk'].astype(f32), axis=-1, keepdims=True) * vh
    y = (on + bonus.reshape(B, T, RWKV_W)) * g
    return y.astype(pc.dtype), pc[:, -1], S_fin


def mixer_block(h, pos, win_buf, conv_buf, shift, wkv0, lp):
    B, T = h.shape[0], h.shape[1]
    proj = h @ lp['w_in']
    pa = proj[..., :A_COLS]
    pb = proj[..., A_COLS:A_COLS + B_COLS]
    pc = proj[..., A_COLS + B_COLS:A_COLS + B_COLS + C_COLS]
    pg = proj[..., A_COLS + B_COLS + C_COLS:]
    q = rotary(pa[..., :Q_W].reshape(B, T, ATT_HEADS, HEAD_DIM), pos)
    k = rotary(pa[..., Q_W:Q_W + KV_W].reshape(B, T, ATT_KV_HEADS, HEAD_DIM), pos)
    v = pa[..., Q_W + KV_W:].reshape(B, T, ATT_KV_HEADS, HEAD_DIM)
    if win_buf is None:
        o_a = window_attn_prompt(q, k, v, lp['att_sinks'])
        wb = min(WINDOW, T)
        new_k, new_v = k[:, T - wb:], v[:, T - wb:]
    else:
        o_a, new_k, new_v = window_attn_sample(q, k, v, win_buf[0], win_buf[1], lp['att_sinks'])
    y_b, new_conv = conformer_conv(pb, conv_buf, lp['conv_dw'], lp['conv_db'], lp['conv_ln_g'], lp['conv_ln_b'])
    y_c, new_shift, new_wkv = rwkv_time_mix(pc, shift, wkv0, lp)
    gates = jax.nn.sigmoid(pg.astype(jnp.float32)).astype(h.dtype).reshape(B, T, N_BRANCH, D_MODEL)
    merged = (gates[:, :, 0] * (o_a @ lp['w_a_out'])
              + gates[:, :, 1] * (y_b @ lp['w_b_out'] + lp['b_b_out'])
              + gates[:, :, 2] * (y_c @ lp['w_c_out']))
    return merged @ lp['w_o'], (new_k, new_v, new_conv, new_shift, new_wkv)


def mem_kv(mem, g_mem, w_mk, w_mv):
    B = mem.shape[0]
    mn = rmsnorm(mem, g_mem)
    mk = (mn @ w_mk).reshape(B, -1, CA_HEADS, CA_HEAD_DIM)
    mv = (mn @ w_mv).reshape(B, -1, CA_HEADS, CA_HEAD_DIM)
    return mk, mv


def cross_attn(h, mk, mv, w_cq, w_co):
    B, T = h.shape[0], h.shape[1]
    q = (h @ w_cq).reshape(B, T, CA_HEADS, CA_HEAD_DIM)
    s = jnp.einsum('bqhd,bkhd->bhqk', q, mk.astype(q.dtype)).astype(jnp.float32) * (CA_HEAD_DIM ** -0.5)
    p = jax.nn.softmax(s, axis=-1).astype(h.dtype)
    o = jnp.einsum('bhqk,bkhd->bqhd', p, mv.astype(h.dtype)).reshape(B, T, D_MODEL)
    return o @ w_co


def peer_ffn(h, w_pq, sub_keys, u_tab, v_tab):
    lead = h.shape[:-1]
    xf = h.reshape(-1, D_MODEL)
    M = xf.shape[0]
    nblk = -(-M // PEER_BLOCK)
    xp = jnp.pad(xf, ((0, nblk * PEER_BLOCK - M), (0, 0))).reshape(nblk, PEER_BLOCK, D_MODEL)

    def one_block(xb):
        q = (xb @ w_pq).reshape(PEER_BLOCK, PEER_HEADS, 2, PEER_HALF)
        s = jnp.einsum('phcd,hcnd->phcn', q, sub_keys).astype(jnp.float32)
        s1, i1 = lax.top_k(s[:, :, 0], PEER_TOPK)
        s2, i2 = lax.top_k(s[:, :, 1], PEER_TOPK)
        cand_s = (s1[..., :, None] + s2[..., None, :]).reshape(PEER_BLOCK, PEER_HEADS, PEER_TOPK * PEER_TOPK)
        cand_i = (i1[..., :, None] * N_KEYS + i2[..., None, :]).reshape(PEER_BLOCK, PEER_HEADS, PEER_TOPK * PEER_TOPK)
        best_s, sel = lax.top_k(cand_s, PEER_TOPK)
        idx = jnp.take_along_axis(cand_i, sel, axis=-1)
        gate = jax.nn.softmax(best_s, axis=-1)
        hid = jax.nn.gelu(jnp.einsum('phkd,pd->phk', u_tab[idx], xb).astype(jnp.float32))
        return jnp.einsum('phk,phkd->pd', (gate * hid).astype(xb.dtype), v_tab[idx])

    out = lax.map(one_block, xp).reshape(-1, D_MODEL)[:M]
    return out.reshape(lead + (D_MODEL,))


def trunk_layer(x, pos, win_buf, conv_buf, shift, wkv0, mk, mv, lp):
    o, st = mixer_block(rmsnorm(x, lp['g_mix']), pos, win_buf, conv_buf, shift, wkv0, lp)
    x = x + o
    x = x + cross_attn(rmsnorm(x, lp['g_ca']), mk, mv, lp['w_cq'], lp['w_co'])
    x = x + peer_ffn(rmsnorm(x, lp['g_ffn']), lp['w_pq'], lp['peer_keys'], lp['peer_u'], lp['peer_v'])
    return x, st


def setup_inputs(seed: int = 0) -> dict:
    key = jax.random.key(seed)
    ks = iter(jax.random.split(key, 64))
    f32 = jnp.float32

    def nrm(shape, scale):
        return jax.random.normal(next(ks), shape, f32) * scale

    def gain(shape):
        return 1.0 + nrm(shape, 0.05)

    L = DEPTH
    wb = min(WINDOW, PAST_LEN)
    dsc = D_MODEL ** -0.5
    return {
        'x_prompt': nrm((BATCH, SEQ, D_MODEL), 1.0),
        'x_sample': nrm((DEC_BATCH, DEC_SEQ, D_MODEL), 1.0),
        'cache_win_k': nrm((L, DEC_BATCH, wb, ATT_KV_HEADS, HEAD_DIM), 1.0),
        'cache_win_v': nrm((L, DEC_BATCH, wb, ATT_KV_HEADS, HEAD_DIM), 1.0),
        'cache_conv': nrm((L, DEC_BATCH, CONV_WIDTH - 1, CONV_CH), 0.5),
        'state_shift': nrm((L, DEC_BATCH, C_COLS), 1.0),
        'state_wkv': nrm((L, DEC_BATCH, RWKV_HEADS, RWKV_HEAD, RWKV_HEAD), 0.3),
        'cache_mem_k': nrm((L, DEC_BATCH, MEM_LEN, CA_HEADS, CA_HEAD_DIM), 1.0),
        'cache_mem_v': nrm((L, DEC_BATCH, MEM_LEN, CA_HEADS, CA_HEAD_DIM), 1.0),
        'mem_prompt': nrm((BATCH, MEM_LEN, D_MODEL), 1.0),
        'g_mix': gain((L, D_MODEL)),
        'w_in': nrm((L, D_MODEL, IN_COLS), dsc),
        'att_sinks': nrm((L, ATT_HEADS), 1.0),
        'w_a_out': nrm((L, Q_W, D_MODEL), Q_W ** -0.5),
        'conv_dw': nrm((L, CONV_WIDTH, CONV_CH), CONV_WIDTH ** -0.5),
        'conv_db': nrm((L, CONV_CH), 0.02),
        'conv_ln_g': gain((L, CONV_CH)),
        'conv_ln_b': nrm((L, CONV_CH), 0.02),
        'w_b_out': nrm((L, CONV_CH, D_MODEL), CONV_CH ** -0.5),
        'b_b_out': nrm((L, D_MODEL), 0.02),
        'rk_mu': jax.random.uniform(next(ks), (L, C_COLS), f32),
        'rk_w0': nrm((L, RWKV_W), 1.0),
        'rk_w2': nrm((L, DECAY_LORA, RWKV_W), 0.1),
        'rk_a0': nrm((L, RWKV_W), 0.5),
        'rk_a2': nrm((L, ICLR_LORA, RWKV_W), 0.1),
        'rk_g2': nrm((L, GATE_LORA, RWKV_W), GATE_LORA ** -0.5),
        'rk_kk': 0.85 + nrm((L, RWKV_W), 0.05),
        'rk_ka': 1.0 + nrm((L, RWKV_W), 0.05),
        'rk_rk': nrm((L, RWKV_HEADS, RWKV_HEAD), 0.1),
        'rk_gn_g': gain((L, RWKV_W)),
        'rk_gn_b': nrm((L, RWKV_W), 0.02),
        'w_c_out': nrm((L, RWKV_W, D_MODEL), RWKV_W ** -0.5),
        'w_o': nrm((L, D_MODEL, D_MODEL), dsc),
        'g_ca': gain((L, D_MODEL)),
        'g_mem': gain((L, D_MODEL)),
        'w_cq': nrm((L, D_MODEL, D_MODEL), dsc),
        'w_mk': nrm((L, D_MODEL, D_MODEL), dsc),
        'w_mv': nrm((L, D_MODEL, D_MODEL), dsc),
        'w_co': nrm((L, D_MODEL, D_MODEL), dsc),
        'g_ffn': gain((L, D_MODEL)),
        'w_pq': nrm((L, D_MODEL, PEER_HEADS * PEER_QDIM), dsc),
        'peer_keys': nrm((L, PEER_HEADS, 2, N_KEYS, PEER_HALF), PEER_HALF ** -0.5),
        'peer_u': nrm((L, N_EXPERTS, D_MODEL), dsc),
        'peer_v': nrm((L, N_EXPERTS, D_MODEL), PEER_HEADS ** -0.5),
        'g_final': gain((D_MODEL,)),
    }


def reference(x_prompt, x_sample, cache_win_k, cache_win_v, cache_conv, state_shift, state_wkv, cache_mem_k, cache_mem_v,
              mem_prompt, g_mix, w_in, att_sinks, w_a_out, conv_dw, conv_db, conv_ln_g, conv_ln_b, w_b_out, b_b_out,
              rk_mu, rk_w0, rk_w2, rk_a0, rk_a2, rk_g2, rk_kk, rk_ka, rk_rk, rk_gn_g, rk_gn_b, w_c_out, w_o,
              g_ca, g_mem, w_cq, w_mk, w_mv, w_co, g_ffn, w_pq, peer_keys, peer_u, peer_v, g_final):
    B, T = x_prompt.shape[0], x_prompt.shape[1]
    pos_p = jnp.arange(T, dtype=jnp.float32)
    pos_s = PAST_LEN + jnp.arange(x_sample.shape[1], dtype=jnp.float32)
    zero_conv = jnp.zeros((B, CONV_WIDTH - 1, CONV_CH), x_prompt.dtype)
    zero_shift = jnp.zeros((B, C_COLS), x_prompt.dtype)
    zero_wkv = jnp.zeros((B, RWKV_HEADS, RWKV_HEAD, RWKV_HEAD), jnp.float32)
    xp, xs = x_prompt, x_sample
    p_states = [[] for _ in range(7)]
    s_states = [[] for _ in range(5)]
    for l in range(DEPTH):
        lp = {'g_mix': g_mix[l], 'w_in': w_in[l], 'att_sinks': att_sinks[l], 'w_a_out': w_a_out[l],
              'conv_dw': conv_dw[l], 'conv_db': conv_db[l], 'conv_ln_g': conv_ln_g[l], 'conv_ln_b': conv_ln_b[l],
              'w_b_out': w_b_out[l], 'b_b_out': b_b_out[l], 'rk_mu': rk_mu[l], 'rk_w0': rk_w0[l], 'rk_w2': rk_w2[l],
              'rk_a0': rk_a0[l], 'rk_a2': rk_a2[l], 'rk_g2': rk_g2[l], 'rk_kk': rk_kk[l], 'rk_ka': rk_ka[l],
              'rk_rk': rk_rk[l], 'rk_gn_g': rk_gn_g[l], 'rk_gn_b': rk_gn_b[l], 'w_c_out': w_c_out[l], 'w_o': w_o[l],
              'g_ca': g_ca[l], 'w_cq': w_cq[l], 'w_co': w_co[l], 'g_ffn': g_ffn[l], 'w_pq': w_pq[l],
              'peer_keys': peer_keys[l], 'peer_u': peer_u[l], 'peer_v': peer_v[l]}
        mk, mv = mem_kv(mem_prompt, g_mem[l], w_mk[l], w_mv[l])
        xp, st_p = trunk_layer(xp, pos_p, None, zero_conv, zero_shift, zero_wkv, mk, mv, lp)
        xs, st_s = trunk_layer(xs, pos_s, (cache_win_k[l], cache_win_v[l]), cache_conv[l], state_shift[l],
                               state_wkv[l], cache_mem_k[l], cache_mem_v[l], lp)
        for lst, arr in zip(p_states, st_p + (mk, mv)):
            lst.append(arr)
        for lst, arr in zip(s_states, st_s):
            lst.append(arr)
    y_prompt = rmsnorm(xp, g_final)
    y_sample = rmsnorm(xs, g_final)
    win_k_p = jnp.stack(p_states[0])
    win_v_p = jnp.stack(p_states[1])
    conv_p = jnp.stack(p_states[2])
    shift_p = jnp.stack(p_states[3])
    wkv_p = jnp.stack(p_states[4])
    mem_k_p = jnp.stack(p_states[5])
    mem_v_p = jnp.stack(p_states[6])
    win_k_s = jnp.stack(s_states[0])
    win_v_s = jnp.stack(s_states[1])
    conv_s = jnp.stack(s_states[2])
    shift_s = jnp.stack(s_states[3])
    wkv_s = jnp.stack(s_states[4])
    return (y_prompt, y_sample, win_k_p, win_v_p, conv_p, shift_p, wkv_p, mem_k_p, mem_v_p,
            win_k_s, win_v_s, conv_s, shift_s, wkv_s)
```

```python
import functools

import numpy as np
import jax
import jax.numpy as jnp
from jax import lax
from jax.experimental import pallas as pl
from jax.experimental.pallas import tpu as pltpu

F32 = jnp.float32
BF16 = jnp.bfloat16
HIGHEST = lax.Precision.HIGHEST

D_MODEL = 1024
PAST_LEN = 8192
ATT_HEADS = 8
ATT_KV_HEADS = 2
ATT_GROUP = ATT_HEADS // ATT_KV_HEADS
HEAD_DIM = 64
ROT_DIM = HEAD_DIM // 4
ROPE_THETA = 500000.0
WINDOW = 128
CONV_CH = 512
CONV_WIDTH = 31
LN_EPS = 1e-5
RWKV_HEADS = 8
RWKV_HEAD = 64
RWKV_W = RWKV_HEADS * RWKV_HEAD
DECAY_LORA = 64
ICLR_LORA = 64
GATE_LORA = 128
GN_EPS = 64e-5
CA_HEADS = 4
CA_HEAD_DIM = D_MODEL // CA_HEADS
PEER_HEADS = 8
N_KEYS = 128
PEER_TOPK = 16
PEER_HALF = 128
RMS_EPS = 1e-6
NEG = -1e30

Q_W = ATT_HEADS * HEAD_DIM
KV_W = ATT_KV_HEADS * HEAD_DIM
A_COLS = Q_W + 2 * KV_W
B_COLS = 2 * CONV_CH
C_COLS = 3 * RWKV_W + DECAY_LORA + ICLR_LORA + GATE_LORA
G_COLS = 3 * D_MODEL

VMEM_LIMIT_V7X = 56 * 1024 * 1024
LANES = 128
SUBLANES = 8
SCAN_PAIRS = 16
SCAN_TSUB = 8
NT_DIMS = (((1,), (1,)), ((), ()))


def _tile(n, prefs=(512, 256, 128, 64, 32, 16, 8)):
    for t in prefs:
        if n % t == 0:
            return t
    raise ValueError(f"no tile for {n}")


def _params(sem):
    return pltpu.CompilerParams(dimension_semantics=sem, vmem_limit_bytes=VMEM_LIMIT_V7X)


def _sigmoid(x):
    return 1.0 / (1.0 + jnp.exp(-x))


def _full(shape):
    nd = len(shape)
    return pl.BlockSpec(shape, lambda *_: (0,) * nd)


def _norm_matmul_kernel(x_ref, g_ref, w_ref, o_ref):
    x = x_ref[...]
    h = x * lax.rsqrt(jnp.mean(x * x, axis=-1, keepdims=True) + RMS_EPS) * g_ref[...]
    o_ref[...] = jnp.dot(h.astype(BF16), w_ref[...], preferred_element_type=F32)


def norm_matmul(x, g, w):
    n, d = x.shape
    c = w.shape[1]
    tm = _tile(n)
    return pl.pallas_call(
        _norm_matmul_kernel,
        grid=(n // tm,),
        in_specs=[pl.BlockSpec((tm, d), lambda i: (i, 0)), _full((1, d)), _full((d, c))],
        out_specs=pl.BlockSpec((tm, c), lambda i: (i, 0)),
        out_shape=jax.ShapeDtypeStruct((n, c), F32),
        compiler_params=_params(("parallel",)),
    )(x, g.reshape(1, d), w)


def _matmul_res_kernel(a_ref, w_ref, x_ref, o_ref):
    o_ref[...] = x_ref[...] + jnp.dot(a_ref[...].astype(BF16), w_ref[...], preferred_element_type=F32)


def matmul_residual(a, w, x):
    n, k = a.shape
    c = w.shape[1]
    tm = _tile(n)
    return pl.pallas_call(
        _matmul_res_kernel,
        grid=(n // tm,),
        in_specs=[pl.BlockSpec((tm, k), lambda i: (i, 0)), _full((k, c)), pl.BlockSpec((tm, c), lambda i: (i, 0))],
        out_specs=pl.BlockSpec((tm, c), lambda i: (i, 0)),
        out_shape=jax.ShapeDtypeStruct((n, c), F32),
        compiler_params=_params(("parallel",)),
    )(a, w, x)


def _final_norm_kernel(x_ref, g_ref, o_ref):
    x = x_ref[...]
    o_ref[...] = x * lax.rsqrt(jnp.mean(x * x, axis=-1, keepdims=True) + RMS_EPS) * g_ref[...]


def final_norm(x, g):
    n, d = x.shape
    tm = _tile(n)
    return pl.pallas_call(
        _final_norm_kernel,
        grid=(n // tm,),
        in_specs=[pl.BlockSpec((tm, d), lambda i: (i, 0)), _full((1, d))],
        out_specs=pl.BlockSpec((tm, d), lambda i: (i, 0)),
        out_shape=jax.ShapeDtypeStruct((n, d), F32),
        compiler_params=_params(("parallel",)),
    )(x, g.reshape(1, d))


def _rotary_tables(pos):
    half = ROT_DIM // 2
    inv = ROPE_THETA ** (-2.0 * jnp.arange(half, dtype=F32) / ROT_DIM)
    ang = pos[:, None] * inv[None, :]
    cos, sin = jnp.cos(ang), jnp.sin(ang)
    n = pos.shape[0]
    one = jnp.ones((n, HEAD_DIM - ROT_DIM), F32)
    zero = jnp.zeros((n, HEAD_DIM - ROT_DIM), F32)
    zh = jnp.zeros((n, half), F32)
    c = jnp.concatenate([cos, cos, one], axis=1)
    s_up = jnp.concatenate([-sin, zh, zero], axis=1)
    s_dn = jnp.concatenate([zh, sin, zero], axis=1)
    return tuple(jnp.concatenate([t, t], axis=1) for t in (c, s_up, s_dn))


def _rotary_kernel(pa_ref, c_ref, su_ref, sd_ref, q_ref, k_ref):
    c, su, sd = c_ref[...], su_ref[...], sd_ref[...]
    half = ROT_DIM // 2
    for j in range((Q_W + KV_W) // LANES):
        x = pa_ref[:, j * LANES:(j + 1) * LANES]
        y = x * c + pltpu.roll(x, LANES - half, 1) * su + pltpu.roll(x, half, 1) * sd
        if j < Q_W // LANES:
            q_ref[:, j * LANES:(j + 1) * LANES] = y
        else:
            k_ref[...] = y


def rotary_qk(pa, tabs):
    n = pa.shape[0]
    tm = _tile(n)
    row = lambda w: pl.BlockSpec((tm, w), lambda i: (i, 0))
    return pl.pallas_call(
        _rotary_kernel,
        grid=(n // tm,),
        in_specs=[row(A_COLS), row(LANES), row(LANES), row(LANES)],
        out_specs=[row(Q_W), row(KV_W)],
        out_shape=[jax.ShapeDtypeStruct((n, Q_W), F32), jax.ShapeDtypeStruct((n, KV_W), F32)],
        compiler_params=_params(("parallel",)),
    )(pa, *tabs)


def _sink_softmax_pv(parts, sink):
    m = sink
    for s, _ in parts:
        m = jnp.maximum(m, jnp.max(s, axis=-1, keepdims=True))
    ps = [jnp.exp(s - m) for s, _ in parts]
    den = jnp.exp(sink - m)
    for p in ps:
        den = den + jnp.sum(p, axis=-1, keepdims=True)
    inv = 1.0 / den
    out = None
    for p, (_, v) in zip(ps, parts):
        o = jnp.dot((p * inv).astype(BF16), v, preferred_element_type=F32)
        out = o if out is None else out + o
    return out


def _swa_prompt_kernel(sink_ref, q_ref, kc_ref, kp_ref, vc_ref, vp_ref, o_ref):
    n = pl.program_id(1)
    qi = lax.broadcasted_iota(jnp.int32, (WINDOW, WINDOW), 0)
    kj = lax.broadcasted_iota(jnp.int32, (WINDOW, WINDOW), 1)
    mask_c = kj <= qi
    mask_p = kj > qi + jnp.where(n > 0, 0, WINDOW)
    scale = HEAD_DIM ** -0.5
    outs = []
    for h in range(ATT_HEADS):
        g = h // ATT_GROUP
        hs = slice(h * HEAD_DIM, (h + 1) * HEAD_DIM)
        gs = slice(g * HEAD_DIM, (g + 1) * HEAD_DIM)
        q = q_ref[:, hs].astype(BF16)
        sc = lax.dot_general(q, kc_ref[:, gs].astype(BF16), NT_DIMS, preferred_element_type=F32) * scale
        sp = lax.dot_general(q, kp_ref[:, gs].astype(BF16), NT_DIMS, preferred_element_type=F32) * scale
        sc = jnp.where(mask_c, sc, NEG)
        sp = jnp.where(mask_p, sp, NEG)
        outs.append(_sink_softmax_pv([(sp, vp_ref[:, gs].astype(BF16)), (sc, vc_ref[:, gs].astype(BF16))],
                                     sink_ref[h]))
    o_ref[...] = jnp.concatenate(outs, axis=1)


def swa_prompt(q, k, pa, sinks, b, t):
    nb = t // WINDOW
    cur = lambda w, c: pl.BlockSpec((WINDOW, w), lambda bi, ni: (bi * nb + ni, c))
    prev = lambda w, c: pl.BlockSpec((WINDOW, w), lambda bi, ni: (bi * nb + jnp.maximum(ni - 1, 0), c))
    vcol = (Q_W + KV_W) // KV_W
    return pl.pallas_call(
        _swa_prompt_kernel,
        grid=(b, nb),
        in_specs=[pl.BlockSpec(memory_space=pltpu.SMEM), cur(Q_W, 0), cur(KV_W, 0), prev(KV_W, 0),
                  cur(KV_W, vcol), prev(KV_W, vcol)],
        out_specs=cur(Q_W, 0),
        out_shape=jax.ShapeDtypeStruct((b * t, Q_W), F32),
        compiler_params=_params(("parallel", "parallel")),
    )(sinks, q, k, k, pa, pa)


def _swa_sample_kernel(sink_ref, q_ref, kb_ref, vb_ref, kn_ref, vn_ref, o_ref, *, bt, dt, wb):
    rows = ATT_GROUP * dt
    tb = lax.rem(lax.broadcasted_iota(jnp.int32, (rows, wb), 0), dt)
    jb = lax.broadcasted_iota(jnp.int32, (rows, wb), 1)
    mask_b = jb > tb + (wb - WINDOW)
    npad = kn_ref.shape[1]
    tn = lax.rem(lax.broadcasted_iota(jnp.int32, (rows, npad), 0), dt)
    jn = lax.broadcasted_iota(jnp.int32, (rows, npad), 1)
    mask_n = jn <= tn
    scale = HEAD_DIM ** -0.5
    for b in range(bt):
        for g in range(ATT_KV_HEADS):
            gs = slice(g * HEAD_DIM, (g + 1) * HEAD_DIM)
            q = q_ref[b, g].astype(BF16)
            sb = lax.dot_general(q, kb_ref[b, :, gs].astype(BF16), NT_DIMS, preferred_element_type=F32) * scale
            sn = lax.dot_general(q, kn_ref[b, :, gs].astype(BF16), NT_DIMS, preferred_element_type=F32) * scale
            sb = jnp.where(mask_b, sb, NEG)
            sn = jnp.where(mask_n, sn, NEG)
            o_ref[b, g] = _sink_softmax_pv(
                [(sb, vb_ref[b, :, gs].astype(BF16)), (sn, vn_ref[b, :, gs].astype(BF16))], sink_ref[g][:, :1])


def swa_sample(qs, kbuf, vbuf, knew, vnew, sink_rows):
    db, _, rows, _ = qs.shape
    dt = rows // ATT_GROUP
    wb = kbuf.shape[1]
    bt = _tile(db, (8, 4, 2, 1))
    blk = lambda a: pl.BlockSpec((bt,) + a.shape[1:], lambda i: (i,) + (0,) * (a.ndim - 1))
    return pl.pallas_call(
        functools.partial(_swa_sample_kernel, bt=bt, dt=dt, wb=wb),
        grid=(db // bt,),
        in_specs=[_full(sink_rows.shape), blk(qs), blk(kbuf), blk(vbuf), blk(knew), blk(vnew)],
        out_specs=blk(qs),
        out_shape=jax.ShapeDtypeStruct(qs.shape, F32),
        compiler_params=_params(("parallel",)),
    )(sink_rows, qs, kbuf, vbuf, knew, vnew)


CONV_HALO = 32


def _ln_swish(y, lg, lb):
    mu = jnp.mean(y, axis=-1, keepdims=True)
    var = jnp.mean(jnp.square(y - mu), axis=-1, keepdims=True)
    yn = (y - mu) * lax.rsqrt(var + LN_EPS) * lg + lb
    return yn * _sigmoid(yn)


def _conv_prompt_kernel(pb_ref, dw_ref, db_ref, lg_ref, lb_ref, y_ref, st_ref, ext_ref, *, tt):
    i = pl.program_id(1)

    @pl.when(i == 0)
    def _():
        ext_ref[0:CONV_HALO, :] = jnp.zeros((CONV_HALO, CONV_CH), F32)

    @pl.when(i > 0)
    def _():
        ext_ref[0:CONV_HALO, :] = ext_ref[tt:tt + CONV_HALO, :]

    pb = pb_ref[...]
    ext_ref[CONV_HALO:CONV_HALO + tt, :] = pb[:, :CONV_CH] * _sigmoid(pb[:, CONV_CH:])
    acc = jnp.zeros((tt, CONV_CH), F32) + db_ref[...]
    first = CONV_HALO - (CONV_WIDTH - 1)
    for j in range(CONV_WIDTH):
        acc = acc + ext_ref[pl.ds(first + j, tt), :] * dw_ref[j:j + 1, :]
    y_ref[...] = _ln_swish(acc, lg_ref[...], lb_ref[...])

    @pl.when(i == pl.num_programs(1) - 1)
    def _():
        st_ref[0] = ext_ref[tt:tt + CONV_HALO, :]


def conv_prompt(pb, dw, db, lg, lb, b, t):
    tt = _tile(t)
    nt = t // tt
    vec = lambda a: a.reshape(1, CONV_CH)
    return pl.pallas_call(
        functools.partial(_conv_prompt_kernel, tt=tt),
        grid=(b, nt),
        in_specs=[pl.BlockSpec((tt, B_COLS), lambda bi, i: (bi * nt + i, 0)), _full((CONV_WIDTH, CONV_CH)),
                  _full((1, CONV_CH)), _full((1, CONV_CH)), _full((1, CONV_CH))],
        out_specs=[pl.BlockSpec((tt, CONV_CH), lambda bi, i: (bi * nt + i, 0)),
                   pl.BlockSpec((1, CONV_HALO, CONV_CH), lambda bi, i: (bi, 0, 0))],
        out_shape=[jax.ShapeDtypeStruct((b * t, CONV_CH), F32), jax.ShapeDtypeStruct((b, CONV_HALO, CONV_CH), F32)],
        scratch_shapes=[pltpu.VMEM((tt + CONV_HALO, CONV_CH), F32)],
        compiler_params=_params(("arbitrary", "arbitrary")),
    )(pb, dw, vec(db), vec(lg), vec(lb))


def _conv_sample_kernel(c_ref, pb_ref, dw_ref, db_ref, lg_ref, lb_ref, y_ref, nc_ref, *, dt):
    nprev = CONV_WIDTH - 1
    us = []
    for t in range(dt):
        pb = pb_ref[t]
        us.append(pb[:, :CONV_CH] * _sigmoid(pb[:, CONV_CH:]))

    def ext(j):
        return c_ref[j] if j < nprev else us[j - nprev]

    for t in range(dt):
        acc = db_ref[...] + ext(t) * dw_ref[0:1, :]
        for j in range(1, CONV_WIDTH):
            acc = acc + ext(t + j) * dw_ref[j:j + 1, :]
        y_ref[t] = _ln_swish(acc, lg_ref[...], lb_ref[...])
    for j in range(nprev):
        nc_ref[j] = ext(j + dt)


def conv_sample(cache_t, pb_t, dw, db, lg, lb):
    nprev, dbt, _ = cache_t.shape
    dt = pb_t.shape[0]
    bt = _tile(dbt, (32, 16, 8))
    vec = lambda a: a.reshape(1, CONV_CH)
    blk = lambda lead, w: pl.BlockSpec((lead, bt, w), lambda i: (0, i, 0))
    return pl.pallas_call(
        functools.partial(_conv_sample_kernel, dt=dt),
        grid=(dbt // bt,),
        in_specs=[blk(nprev, CONV_CH), blk(dt, B_COLS), _full((CONV_WIDTH, CONV_CH)), _full((1, CONV_CH)),
                  _full((1, CONV_CH)), _full((1, CONV_CH))],
        out_specs=[blk(dt, CONV_CH), blk(nprev, CONV_CH)],
        out_shape=[jax.ShapeDtypeStruct((dt, dbt, CONV_CH), F32), jax.ShapeDtypeStruct((nprev, dbt, CONV_CH), F32)],
        compiler_params=_params(("parallel",)),
    )(cache_t, pb_t, dw, vec(db), vec(lg), vec(lb))


def _head_sum_matrix():
    h = np.arange(RWKV_W) // RWKV_HEAD
    return jnp.asarray((h[:, None] == h[None, :]).astype(np.float32))


def _rwkv_pre_kernel(pc_ref, pv_ref, mu_ref, w0_ref, w2_ref, a0_ref, a2_ref, g2_ref, kkp_ref, ka_ref, rk_ref, hs_ref,
                     x5_ref, v_ref, g_ref, bonus_ref):
    pc = pc_ref[...]
    xs = pc + (pv_ref[...] - pc) * mu_ref[...]
    o1, o2, o3 = RWKV_W, 2 * RWKV_W, 3 * RWKV_W
    o4 = o3 + DECAY_LORA
    o5 = o4 + ICLR_LORA
    r, k, v = xs[:, :o1], xs[:, o1:o2], xs[:, o2:o3]
    wd, ad, gd = xs[:, o3:o4], xs[:, o4:o5], xs[:, o5:]
    y = -(w0_ref[...] + jnp.dot(jnp.tanh(wd).astype(BF16), w2_ref[...], preferred_element_type=F32))
    softplus = jnp.maximum(y, 0.0) + jnp.log(1.0 + jnp.exp(-jnp.abs(y)))
    decay = jnp.exp(-jnp.exp(-softplus - 0.5))
    a = _sigmoid(a0_ref[...] + jnp.dot(ad.astype(BF16), a2_ref[...], preferred_element_type=F32))
    g_ref[...] = jnp.dot(_sigmoid(gd).astype(BF16), g2_ref[...], preferred_element_type=F32)
    hs = hs_ref[...]
    kk = k * kkp_ref[...]
    norm = jnp.sqrt(jnp.dot(kk * kk, hs, precision=HIGHEST, preferred_element_type=F32))
    kk = kk / jnp.maximum(norm, 1e-12)
    k_mod = k * (1.0 + (a - 1.0) * ka_ref[...])
    x5_ref[0] = r
    x5_ref[1] = decay
    x5_ref[2] = k_mod
    x5_ref[3] = -kk
    x5_ref[4] = kk * a
    v_ref[...] = v
    bonus_ref[...] = jnp.dot(r * k_mod * rk_ref[...], hs, precision=HIGHEST, preferred_element_type=F32) * v


def rwkv_pre(pc, prev, lp):
    n = pc.shape[0]
    tm = _tile(n, (256, 128, 64, 32, 16, 8))
    row = lambda w: pl.BlockSpec((tm, w), lambda i: (i, 0))
    vec = lambda a: a.reshape(1, -1)
    consts = [vec(lp['rk_mu']), vec(lp['rk_w0']), lp['rk_w2'].astype(BF16), vec(lp['rk_a0']), lp['rk_a2'].astype(BF16),
              lp['rk_g2'].astype(BF16), vec(lp['rk_kk']), vec(lp['rk_ka']), vec(lp['rk_rk']), _head_sum_matrix()]
    out = jax.ShapeDtypeStruct((n, RWKV_W), F32)
    return pl.pallas_call(
        _rwkv_pre_kernel,
        grid=(n // tm,),
        in_specs=[row(C_COLS), row(C_COLS)] + [_full(c.shape) for c in consts],
        out_specs=[pl.BlockSpec((5, tm, RWKV_W), lambda i: (0, i, 0)), row(RWKV_W), row(RWKV_W), row(RWKV_W)],
        out_shape=[jax.ShapeDtypeStruct((5, n, RWKV_W), F32), out, out, out],
        compiler_params=_params(("parallel",)),
    )(pc, prev, *consts)


def _scan_select_matrices():
    e = np.zeros((SCAN_TSUB, LANES, LANES), np.float32)
    vl = LANES // SCAN_PAIRS
    for tl in range(SCAN_TSUB):
        for g in range(SCAN_PAIRS):
            e[tl, tl * SCAN_PAIRS + g, g * vl:(g + 1) * vl] = 1.0
    return jnp.asarray(e)


def _scan_kernel(x_ref, v_ref, s0_ref, e_ref, o_ref, sf_ref, z_ref, *, groups, steps):
    c = pl.program_id(1)
    nvh = z_ref.shape[0]

    @pl.when(c == 0)
    def _():
        z_ref[...] = s0_ref[0]

    def body(th, carry):
        xg = x_ref[0, th]
        for tl in range(steps):
            xt = jnp.dot(xg, e_ref[tl], precision=HIGHEST, preferred_element_type=F32)
            r_t, w_t, k_t, a_t, b_t = (xt[j * RWKV_HEAD:(j + 1) * RWKV_HEAD] for j in range(5))
            t = th * steps + tl
            vrow = v_ref[0, t]
            orow = []
            for vh in range(nvh):
                z = z_ref[vh]
                sa = jnp.sum(z * a_t, axis=0, keepdims=True)
                zn = z * w_t + b_t * sa + k_t * vrow[vh:vh + 1]
                z_ref[vh] = zn
                orow.append(jnp.sum(zn * r_t, axis=0, keepdims=True))
            o_ref[0, t] = jnp.concatenate(orow, axis=0)
        return carry

    lax.fori_loop(0, groups, body, 0)

    @pl.when(c == pl.num_programs(1) - 1)
    def _():
        sf_ref[0] = z_ref[...]


def rwkv_scan(x5, v, s0, nseq, t):
    spb = SCAN_PAIRS // RWKV_HEADS
    assert nseq % spb == 0
    nblk = nseq // spb
    vl = LANES // SCAN_PAIRS
    vh = RWKV_HEAD // vl
    steps = min(SCAN_TSUB, t)
    assert t % steps == 0
    t8 = t // steps
    xk = x5.reshape(5, nblk, spb, t8, steps, RWKV_HEADS, RWKV_HEAD)
    if steps < SCAN_TSUB:
        xk = jnp.pad(xk, ((0, 0),) * 4 + ((0, SCAN_TSUB - steps),) + ((0, 0),) * 2)
    xk = xk.transpose(1, 3, 0, 6, 4, 2, 5).reshape(nblk, t8, 5 * RWKV_HEAD, LANES)
    vk = v.reshape(nblk, spb, t, RWKV_HEADS, vh, vl).transpose(0, 2, 4, 1, 3, 5).reshape(nblk, t, vh, LANES)
    sk = s0.reshape(nblk, spb, RWKV_HEADS, vh, vl, RWKV_HEAD).transpose(0, 3, 5, 1, 2, 4).reshape(
        nblk, vh, RWKV_HEAD, LANES)
    groups = _tile(t8, (32, 16, 8, 4, 2, 1))
    nchunk = t8 // groups
    ch = groups * steps
    o, sf = pl.pallas_call(
        functools.partial(_scan_kernel, groups=groups, steps=steps),
        grid=(nblk, nchunk),
        in_specs=[pl.BlockSpec((1, groups, 5 * RWKV_HEAD, LANES), lambda i, c: (i, c, 0, 0)),
                  pl.BlockSpec((1, ch, vh, LANES), lambda i, c: (i, c, 0, 0)),
                  pl.BlockSpec((1, vh, RWKV_HEAD, LANES), lambda i, c: (i, 0, 0, 0)),
                  _full((SCAN_TSUB, LANES, LANES))],
        out_specs=[pl.BlockSpec((1, ch, vh, LANES), lambda i, c: (i, c, 0, 0)),
                   pl.BlockSpec((1, vh, RWKV_HEAD, LANES), lambda i, c: (i, 0, 0, 0))],
        out_shape=[jax.ShapeDtypeStruct((nblk, t, vh, LANES), F32),
                   jax.ShapeDtypeStruct((nblk, vh, RWKV_HEAD, LANES), F32)],
        scratch_shapes=[pltpu.VMEM((vh, RWKV_HEAD, LANES), F32)],
        compiler_params=_params(("arbitrary", "arbitrary")),
    )(xk, vk, sk, _scan_select_matrices())
    o = o.reshape(nblk, t, vh, spb, RWKV_HEADS, vl).transpose(0, 3, 1, 4, 2, 5).reshape(nseq * t, RWKV_W)
    sf = sf.reshape(nblk, vh, RWKV_HEAD, spb, RWKV_HEADS, vl).transpose(0, 3, 4, 1, 5, 2).reshape(
        nseq, RWKV_HEADS, RWKV_HEAD, RWKV_HEAD)
    return o, sf


def _merge_kernel(x_ref, oa_ref, yb_ref, oc_ref, bonus_ref, g_ref, pg_ref, wa_ref, wb_ref, bb_ref, wc_ref, wo_ref,
                  gng_ref, gnb_ref, hs_ref, o_ref):
    hs = hs_ref[...] * (1.0 / RWKV_HEAD)
    oc = oc_ref[...]
    mu = jnp.dot(oc, hs, precision=HIGHEST, preferred_element_type=F32)
    dev = oc - mu
    var = jnp.dot(dev * dev, hs, precision=HIGHEST, preferred_element_type=F32)
    on = dev * lax.rsqrt(var + GN_EPS) * gng_ref[...] + gnb_ref[...]
    yc = (on + bonus_ref[...]) * g_ref[...]
    d = D_MODEL
    dot = lambda a, w: jnp.dot(a.astype(BF16), w[...], preferred_element_type=F32)
    merged = (_sigmoid(pg_ref[:, 0:d]) * dot(oa_ref[...], wa_ref)
              + _sigmoid(pg_ref[:, d:2 * d]) * (dot(yb_ref[...], wb_ref) + bb_ref[...])
              + _sigmoid(pg_ref[:, 2 * d:3 * d]) * dot(yc, wc_ref))
    o_ref[...] = x_ref[...] + dot(merged, wo_ref)


def merge_out(x, oa, yb, oc, bonus, g, pg, lp):
    n = x.shape[0]
    tm = _tile(n, (256, 128, 64, 32, 16, 8))
    row = lambda w: pl.BlockSpec((tm, w), lambda i: (i, 0))
    vec = lambda a: a.reshape(1, -1)
    consts = [lp['w_a_out'].astype(BF16), lp['w_b_out'].astype(BF16), vec(lp['b_b_out']), lp['w_c_out'].astype(BF16),
              lp['w_o'].astype(BF16), vec(lp['rk_gn_g']), vec(lp['rk_gn_b']), _head_sum_matrix()]
    return pl.pallas_call(
        _merge_kernel,
        grid=(n // tm,),
        in_specs=[row(D_MODEL), row(Q_W), row(CONV_CH), row(RWKV_W), row(RWKV_W), row(RWKV_W), row(G_COLS)]
        + [_full(c.shape) for c in consts],
        out_specs=row(D_MODEL),
        out_shape=jax.ShapeDtypeStruct((n, D_MODEL), F32),
        compiler_params=_params(("parallel",)),
    )(x, oa, yb, oc, bonus, g, pg, *consts)


def _cross_attn_kernel(q_ref, k_ref, v_ref, o_ref):
    scale = CA_HEAD_DIM ** -0.5
    outs = []
    for h in range(CA_HEADS):
        hs = slice(h * CA_HEAD_DIM, (h + 1) * CA_HEAD_DIM)
        q = q_ref[0, :, hs].astype(BF16)
        s = lax.dot_general(q, k_ref[0, :, hs].astype(BF16), NT_DIMS, preferred_element_type=F32) * scale
        m = jnp.max(s, axis=-1, keepdims=True)
        p = jnp.exp(s - m)
        p = p / jnp.sum(p, axis=-1, keepdims=True)
        outs.append(jnp.dot(p.astype(BF16), v_ref[0, :, hs].astype(BF16), preferred_element_type=F32))
    o_ref[0] = jnp.concatenate(outs, axis=1)


def cross_attn(q, mk, mv, tiles_per_seq):
    nt, tq, d = q.shape
    m = mk.shape[1]
    kv = pl.BlockSpec((1, m, d), lambda i: (i // tiles_per_seq, 0, 0))
    qs = pl.BlockSpec((1, tq, d), lambda i: (i, 0, 0))
    return pl.pallas_call(
        _cross_attn_kernel,
        grid=(nt,),
        in_specs=[qs, kv, kv],
        out_specs=qs,
        out_shape=jax.ShapeDtypeStruct(q.shape, F32),
        compiler_params=_params(("parallel",)),
    )(q, mk, mv)


def _staircase():
    return [(i, PEER_TOPK // (i + 1)) for i in range(PEER_TOPK)]


PEER_CAND = sum(nj for _, nj in _staircase())
PEER_CAND_PAD = -(-PEER_CAND // SUBLANES) * SUBLANES
PEER_STATS = 4


def _extract_topk(cur, out_ref, base):
    for k in range(PEER_TOPK):
        m = jnp.max(cur, axis=0, keepdims=True)
        out_ref[base + k:base + k + 1, :] = m
        if k + 1 < PEER_TOPK:
            cur = jnp.where(cur == m, -jnp.inf, cur)


def _peer_select_kernel(qp_ref, keys_ref, s_ref, stat_ref, tv_ref, cand_ref, best_ref):
    nsub = 2 * PEER_HEADS
    for hc in range(nsub):
        q = qp_ref[:, hc * PEER_HALF:(hc + 1) * PEER_HALF].astype(BF16)
        st = lax.dot_general(keys_ref[hc], q, NT_DIMS, preferred_element_type=F32)
        s_ref[hc] = st
        _extract_topk(st, tv_ref, hc * PEER_TOPK)
    tm = qp_ref.shape[0]
    for h in range(PEER_HEADS):
        b1 = 2 * h * PEER_TOPK
        b2 = b1 + PEER_TOPK
        cand_ref[PEER_CAND_PAD - SUBLANES:PEER_CAND_PAD, :] = jnp.full((SUBLANES, tm), -jnp.inf, F32)
        off = 0
        for i, nj in _staircase():
            cand_ref[off:off + nj, :] = tv_ref[b1 + i:b1 + i + 1, :] + tv_ref[b2:b2 + nj, :]
            off += nj
        _extract_topk(cand_ref[...], best_ref, 0)
        best = best_ref[...]
        z = jnp.sum(jnp.exp(best - best[0:1, :]), axis=0, keepdims=True)
        r = h * PEER_STATS
        stat_ref[r:r + 1, :] = best[PEER_TOPK - 1:PEER_TOPK, :]
        stat_ref[r + 1:r + 2, :] = tv_ref[b1:b1 + 1, :]
        stat_ref[r + 2:r + 3, :] = tv_ref[b2:b2 + 1, :]
        stat_ref[r + 3:r + 4, :] = 1.0 / z


def peer_select(qp, keys):
    n = qp.shape[0]
    tm = _tile(n, (256, 128))
    nsub = 2 * PEER_HEADS
    return pl.pallas_call(
        _peer_select_kernel,
        grid=(n // tm,),
        in_specs=[pl.BlockSpec((tm, nsub * PEER_HALF), lambda i: (i, 0)), _full(keys.shape)],
        out_specs=[pl.BlockSpec((nsub, N_KEYS, tm), lambda i: (0, 0, i)),
                   pl.BlockSpec((PEER_HEADS * PEER_STATS, tm), lambda i: (0, i))],
        out_shape=[jax.ShapeDtypeStruct((nsub, N_KEYS, n), F32),
                   jax.ShapeDtypeStruct((PEER_HEADS * PEER_STATS, n), F32)],
        scratch_shapes=[pltpu.VMEM((nsub * PEER_TOPK, tm), F32), pltpu.VMEM((PEER_CAND_PAD, tm), F32),
                        pltpu.VMEM((PEER_TOPK, tm), F32)],
        compiler_params=_params(("parallel",)),
    )(qp, keys)


def _gelu_tanh(x):
    return 0.5 * x * (1.0 + jnp.tanh(0.7978845608028654 * (x + 0.044715 * (x * x * x))))


def _peer_dense_kernel(x_ref, g_ref, s_ref, stat_ref, u_ref, vt_ref, o_ref, xnt_ref, e2_ref, acc_ref, *, te):
    e = pl.program_id(1)

    @pl.when(e == 0)
    def _():
        x = x_ref[...]
        xn = x * lax.rsqrt(jnp.mean(x * x, axis=-1, keepdims=True) + RMS_EPS) * g_ref[...]
        xnt_ref[...] = xn.T.astype(BF16)
        for h in range(PEER_HEADS):
            r = h * PEER_STATS
            e2_ref[h] = jnp.exp(s_ref[2 * h + 1] - stat_ref[r + 2:r + 3, :])
        acc_ref[...] = jnp.zeros_like(acc_ref)

    ht = _gelu_tanh(jnp.dot(u_ref[...], xnt_ref[...], preferred_element_type=F32))
    parts = []
    for al in range(te // N_KEYS):
        a = e * (te // N_KEYS) + al
        w = None
        for h in range(PEER_HEADS):
            r = h * PEER_STATS
            s1 = s_ref[2 * h, pl.ds(a, 1), :]
            rho = jnp.exp(s1 - stat_ref[r + 1:r + 2, :]) * stat_ref[r + 3:r + 4, :]
            sel = (s1 + s_ref[2 * h + 1]) >= stat_ref[r:r + 1, :]
            c = jnp.where(sel, e2_ref[h] * rho, 0.0)
            w = c if w is None else w + c
        parts.append((w * ht[al * N_KEYS:(al + 1) * N_KEYS]).astype(BF16))
    wh = jnp.concatenate(parts, axis=0) if len(parts) > 1 else parts[0]
    acc_ref[...] += jnp.dot(vt_ref[...], wh, preferred_element_type=F32)

    @pl.when(e == pl.num_programs(1) - 1)
    def _():
        o_ref[...] = x_ref[...] + acc_ref[...].T


def peer_dense(x, g, s_t, stats, u, v_t):
    n, d = x.shape
    ne = u.shape[0]
    tm = _tile(n, (512, 256, 128))
    te = 2 * N_KEYS
    nsub = 2 * PEER_HEADS
    return pl.pallas_call(
        functools.partial(_peer_dense_kernel, te=te),
        grid=(n // tm, ne // te),
        in_specs=[pl.BlockSpec((tm, d), lambda i, e: (i, 0)), _full((1, d)),
                  pl.BlockSpec((nsub, N_KEYS, tm), lambda i, e: (0, 0, i)),
                  pl.BlockSpec((PEER_HEADS * PEER_STATS, tm), lambda i, e: (0, i)),
                  pl.BlockSpec((te, d), lambda i, e: (e, 0)),
                  pl.BlockSpec((d, te), lambda i, e: (0, e))],
        out_specs=pl.BlockSpec((tm, d), lambda i, e: (i, 0)),
        out_shape=jax.ShapeDtypeStruct((n, d), F32),
        scratch_shapes=[pltpu.VMEM((d, tm), BF16), pltpu.VMEM((PEER_HEADS, N_KEYS, tm), F32),
                        pltpu.VMEM((d, tm), F32)],
        compiler_params=_params(("parallel", "arbitrary")),
    )(x, g.reshape(1, d), s_t, stats, u, v_t)


def kernel(x_prompt, x_sample, cache_win_k, cache_win_v, cache_conv, state_shift, state_wkv, cache_mem_k, cache_mem_v,
           mem_prompt, g_mix, w_in, att_sinks, w_a_out, conv_dw, conv_db, conv_ln_g, conv_ln_b, w_b_out, b_b_out,
           rk_mu, rk_w0, rk_w2, rk_a0, rk_a2, rk_g2, rk_kk, rk_ka, rk_rk, rk_gn_g, rk_gn_b, w_c_out, w_o,
           g_ca, g_mem, w_cq, w_mk, w_mv, w_co, g_ffn, w_pq, peer_keys, peer_u, peer_v, g_final):
    b, t, d = x_prompt.shape
    db, dt, _ = x_sample.shape
    depth = w_in.shape[0]
    wb = cache_win_k.shape[2]
    mem_len = mem_prompt.shape[1]
    n_p, n_s = b * t, db * dt
    assert t % WINDOW == 0 and wb == WINDOW and dt <= SUBLANES

    x = jnp.concatenate([x_prompt.reshape(n_p, d), x_sample.reshape(n_s, d)], axis=0)
    pos = jnp.concatenate([jnp.tile(jnp.arange(t, dtype=F32), b),
                           jnp.tile(PAST_LEN + jnp.arange(dt, dtype=F32), db)])
    rot_tabs = _rotary_tables(pos)
    tq = _tile(t)
    new_pad = SUBLANES - dt

    outs = [[] for _ in range(12)]
    for l in range(depth):
        lp = {'rk_mu': rk_mu[l], 'rk_w0': rk_w0[l], 'rk_w2': rk_w2[l], 'rk_a0': rk_a0[l], 'rk_a2': rk_a2[l],
              'rk_g2': rk_g2[l], 'rk_kk': rk_kk[l], 'rk_ka': rk_ka[l], 'rk_rk': rk_rk[l].reshape(-1),
              'rk_gn_g': rk_gn_g[l], 'rk_gn_b': rk_gn_b[l], 'w_a_out': w_a_out[l], 'w_b_out': w_b_out[l],
              'b_b_out': b_b_out[l], 'w_c_out': w_c_out[l], 'w_o': w_o[l]}
        win = w_in[l].astype(BF16)
        c0, c1, c2 = A_COLS, A_COLS + B_COLS, A_COLS + B_COLS + C_COLS
        pa = norm_matmul(x, g_mix[l], win[:, :c0])
        pb = norm_matmul(x, g_mix[l], win[:, c0:c1])
        pc = norm_matmul(x, g_mix[l], win[:, c1:c2])
        pg = norm_matmul(x, g_mix[l], win[:, c2:])

        q_rot, k_rot = rotary_qk(pa, rot_tabs)
        v_att = pa[:, Q_W + KV_W:]
        oa_p = swa_prompt(q_rot, k_rot, pa, att_sinks[l], b, t)
        qs = q_rot[n_p:].reshape(db, dt, ATT_KV_HEADS, ATT_GROUP, HEAD_DIM).transpose(0, 2, 3, 1, 4).reshape(
            db, ATT_KV_HEADS, ATT_GROUP * dt, HEAD_DIM)
        k_new = k_rot[n_p:].reshape(db, dt, KV_W)
        v_new = v_att[n_p:].reshape(db, dt, KV_W)
        padn = lambda a: jnp.pad(a, ((0, 0), (0, new_pad), (0, 0)))
        sink_rows = jnp.broadcast_to(
            jnp.repeat(att_sinks[l].reshape(ATT_KV_HEADS, ATT_GROUP), dt, axis=1)[:, :, None],
            (ATT_KV_HEADS, ATT_GROUP * dt, LANES))
        kbuf = cache_win_k[l].reshape(db, wb, KV_W)
        vbuf = cache_win_v[l].reshape(db, wb, KV_W)
        oa_s = swa_sample(qs, kbuf, vbuf, padn(k_new), padn(v_new), sink_rows)
        oa_s = oa_s.reshape(db, ATT_KV_HEADS, ATT_GROUP, dt, HEAD_DIM).transpose(0, 3, 1, 2, 4).reshape(n_s, Q_W)
        oa = jnp.concatenate([oa_p, oa_s], axis=0)

        yb_p, conv_st = conv_prompt(pb, conv_dw[l], conv_db[l], conv_ln_g[l], conv_ln_b[l], b, t)
        yb_s, conv_new = conv_sample(cache_conv[l].transpose(1, 0, 2), pb[n_p:].reshape(db, dt, B_COLS).transpose(1, 0, 2),
                                     conv_dw[l], conv_db[l], conv_ln_g[l], conv_ln_b[l])
        yb = jnp.concatenate([yb_p, yb_s.transpose(1, 0, 2).reshape(n_s, CONV_CH)], axis=0)

        pc_p = pc[:n_p].reshape(b, t, C_COLS)
        pc_s = pc[n_p:].reshape(db, dt, C_COLS)
        prev = jnp.concatenate([
            jnp.concatenate([jnp.zeros((b, 1, C_COLS), F32), pc_p[:, :-1]], axis=1).reshape(n_p, C_COLS),
            jnp.concatenate([state_shift[l][:, None], pc_s[:, :-1]], axis=1).reshape(n_s, C_COLS)], axis=0)
        x5, v_rk, g_rk, bonus = rwkv_pre(pc, prev, lp)
        oc_p, wkv_p = rwkv_scan(x5[:, :n_p], v_rk[:n_p], jnp.zeros((b, RWKV_HEADS, RWKV_HEAD, RWKV_HEAD), F32), b, t)
        oc_s, wkv_s = rwkv_scan(x5[:, n_p:], v_rk[n_p:], state_wkv[l], db, dt)
        oc = jnp.concatenate([oc_p, oc_s], axis=0)

        x = merge_out(x, oa, yb, oc, bonus, g_rk, pg, lp)

        mk = norm_matmul(mem_prompt.reshape(b * mem_len, d), g_mem[l], w_mk[l].astype(BF16))
        mv = norm_matmul(mem_prompt.reshape(b * mem_len, d), g_mem[l], w_mv[l].astype(BF16))
        qc = norm_matmul(x, g_ca[l], w_cq[l].astype(BF16))
        oc_ca_p = cross_attn(qc[:n_p].reshape(n_p // tq, tq, d), mk.reshape(b, mem_len, d), mv.reshape(b, mem_len, d),
                             t // tq)
        oc_ca_s = cross_attn(qc[n_p:].reshape(db, dt, d), cache_mem_k[l].reshape(db, mem_len, d),
                             cache_mem_v[l].reshape(db, mem_len, d), 1)
        x = matmul_residual(jnp.concatenate([oc_ca_p.reshape(n_p, d), oc_ca_s.reshape(n_s, d)], axis=0),
                            w_co[l].astype(BF16), x)

        qp = norm_matmul(x, g_ffn[l], w_pq[l].astype(BF16))
        s_t, stats = peer_select(qp, peer_keys[l].reshape(2 * PEER_HEADS, N_KEYS, PEER_HALF).astype(BF16))
        x = peer_dense(x, g_ffn[l], s_t, stats, peer_u[l].astype(BF16), peer_v[l].astype(BF16).T)

        k4 = lambda a, nb_, tt_: a.reshape(nb_, tt_, ATT_KV_HEADS, HEAD_DIM)
        outs[0].append(k4(k_rot[:n_p], b, t)[:, t - wb:])
        outs[1].append(k4(v_att[:n_p], b, t)[:, t - wb:])
        outs[2].append(conv_st[:, CONV_HALO - (CONV_WIDTH - 1):])
        outs[3].append(pc_p[:, -1])
        outs[4].append(wkv_p)
        outs[5].append(mk.reshape(b, mem_len, CA_HEADS, CA_HEAD_DIM))
        outs[6].append(mv.reshape(b, mem_len, CA_HEADS, CA_HEAD_DIM))
        outs[7].append(jnp.concatenate([cache_win_k[l][:, dt:], k4(k_new, db, dt)], axis=1))
        outs[8].append(jnp.concatenate([cache_win_v[l][:, dt:], k4(v_new, db, dt)], axis=1))
        outs[9].append(conv_new.transpose(1, 0, 2))
        outs[10].append(pc_s[:, -1])
        outs[11].append(wkv_s)

    y = final_norm(x, g_final)
    return (y[:n_p].reshape(b, t, d), y[n_p:].reshape(db, dt, d)) + tuple(jnp.stack(o) for o in outs)
```

```python
import functools

import numpy as np
import jax
import jax.numpy as jnp
from jax import lax
from jax.experimental import pallas as pl
from jax.experimental.pallas import tpu as pltpu

F32 = jnp.float32
BF16 = jnp.bfloat16
HIGHEST = lax.Precision.HIGHEST

D_MODEL = 1024
PAST_LEN = 8192
ATT_HEADS = 8
ATT_KV_HEADS = 2
ATT_GROUP = ATT_HEADS // ATT_KV_HEADS
HEAD_DIM = 64
ROT_DIM = HEAD_DIM // 4
ROPE_THETA = 500000.0
WINDOW = 128
CONV_CH = 512
CONV_WIDTH = 31
LN_EPS = 1e-5
RWKV_HEADS = 8
RWKV_HEAD = 64
RWKV_W = RWKV_HEADS * RWKV_HEAD
DECAY_LORA = 64
ICLR_LORA = 64
GATE_LORA = 128
GN_EPS = 64e-5
CA_HEADS = 4
CA_HEAD_DIM = D_MODEL // CA_HEADS
PEER_HEADS = 8
N_KEYS = 128
PEER_TOPK = 16
PEER_HALF = 128
RMS_EPS = 1e-6
NEG = -1e30

Q_W = ATT_HEADS * HEAD_DIM
KV_W = ATT_KV_HEADS * HEAD_DIM
A_COLS = Q_W + 2 * KV_W
B_COLS = 2 * CONV_CH
C_COLS = 3 * RWKV_W + DECAY_LORA + ICLR_LORA + GATE_LORA
G_COLS = 3 * D_MODEL

VMEM_LIMIT_V7X = 56 * 1024 * 1024
LANES = 128
SUBLANES = 8
SCAN_PAIRS = 16
SCAN_TSUB = 8
NT_DIMS = (((1,), (1,)), ((), ()))


def _tile(n, prefs=(512, 256, 128, 64, 32, 16, 8)):
    for t in prefs:
        if n % t == 0:
            return t
    raise ValueError(f"no tile for {n}")


def _params(sem):
    return pltpu.CompilerParams(dimension_semantics=sem, vmem_limit_bytes=VMEM_LIMIT_V7X)


def _sigmoid(x):
    return 1.0 / (1.0 + jnp.exp(-x))


def _full(shape):
    nd = len(shape)
    return pl.BlockSpec(shape, lambda *_: (0,) * nd)


def _norm_matmul_kernel(x_ref, g_ref, w_ref, o_ref):
    x = x_ref[...]
    h = x * lax.rsqrt(jnp.mean(x * x, axis=-1, keepdims=True) + RMS_EPS) * g_ref[...]
    o_ref[...] = jnp.dot(h.astype(BF16), w_ref[...], preferred_element_type=F32)


def norm_matmul(x, g, w):
    n, d = x.shape
    c = w.shape[1]
    tm = _tile(n)
    return pl.pallas_call(
        _norm_matmul_kernel,
        grid=(n // tm,),
        in_specs=[pl.BlockSpec((tm, d), lambda i: (i, 0)), _full((1, d)), _full((d, c))],
        out_specs=pl.BlockSpec((tm, c), lambda i: (i, 0)),
        out_shape=jax.ShapeDtypeStruct((n, c), F32),
        compiler_params=_params(("parallel",)),
    )(x, g.reshape(1, d), w)


def _matmul_res_kernel(a_ref, w_ref, x_ref, o_ref):
    o_ref[...] = x_ref[...] + jnp.dot(a_ref[...].astype(BF16), w_ref[...], preferred_element_type=F32)


def matmul_residual(a, w, x):
    n, k = a.shape
    c = w.shape[1]
    tm = _tile(n)
    return pl.pallas_call(
        _matmul_res_kernel,
        grid=(n // tm,),
        in_specs=[pl.BlockSpec((tm, k), lambda i: (i, 0)), _full((k, c)), pl.BlockSpec((tm, c), lambda i: (i, 0))],
        out_specs=pl.BlockSpec((tm, c), lambda i: (i, 0)),
        out_shape=jax.ShapeDtypeStruct((n, c), F32),
        compiler_params=_params(("parallel",)),
    )(a, w, x)


def _final_norm_kernel(x_ref, g_ref, o_ref):
    x = x_ref[...]
    o_ref[...] = x * lax.rsqrt(jnp.mean(x * x, axis=-1, keepdims=True) + RMS_EPS) * g_ref[...]


def final_norm(x, g):
    n, d = x.shape
    tm = _tile(n)
    return pl.pallas_call(
        _final_norm_kernel,
        grid=(n // tm,),
        in_specs=[pl.BlockSpec((tm, d), lambda i: (i, 0)), _full((1, d))],
        out_specs=pl.BlockSpec((tm, d), lambda i: (i, 0)),
        out_shape=jax.ShapeDtypeStruct((n, d), F32),
        compiler_params=_params(("parallel",)),
    )(x, g.reshape(1, d))


def _rotary_tables(pos):
    half = ROT_DIM // 2
    inv = ROPE_THETA ** (-2.0 * jnp.arange(half, dtype=F32) / ROT_DIM)
    ang = pos[:, None] * inv[None, :]
    cos, sin = jnp.cos(ang), jnp.sin(ang)
    n = pos.shape[0]
    one = jnp.ones((n, HEAD_DIM - ROT_DIM), F32)
    zero = jnp.zeros((n, HEAD_DIM - ROT_DIM), F32)
    zh = jnp.zeros((n, half), F32)
    c = jnp.concatenate([cos, cos, one], axis=1)
    s_up = jnp.concatenate([-sin, zh, zero], axis=1)
    s_dn = jnp.concatenate([zh, sin, zero], axis=1)
    return tuple(jnp.concatenate([t, t], axis=1) for t in (c, s_up, s_dn))


def _rotary_kernel(pa_ref, c_ref, su_ref, sd_ref, q_ref, k_ref):
    c, su, sd = c_ref[...], su_ref[...], sd_ref[...]
    half = ROT_DIM // 2
    for j in range((Q_W + KV_W) // LANES):
        x = pa_ref[:, j * LANES:(j + 1) * LANES]
        y = x * c + pltpu.roll(x, LANES - half, 1) * su + pltpu.roll(x, half, 1) * sd
        if j < Q_W // LANES:
            q_ref[:, j * LANES:(j + 1) * LANES] = y
        else:
            k_ref[...] = y


def rotary_qk(pa, tabs):
    n = pa.shape[0]
    tm = _tile(n)
    row = lambda w: pl.BlockSpec((tm, w), lambda i: (i, 0))
    return pl.pallas_call(
        _rotary_kernel,
        grid=(n // tm,),
        in_specs=[row(A_COLS), row(LANES), row(LANES), row(LANES)],
        out_specs=[row(Q_W), row(KV_W)],
        out_shape=[jax.ShapeDtypeStruct((n, Q_W), F32), jax.ShapeDtypeStruct((n, KV_W), F32)],
        compiler_params=_params(("parallel",)),
    )(pa, *tabs)


def _sink_softmax_pv(parts, sink):
    m = sink
    for s, _ in parts:
        m = jnp.maximum(m, jnp.max(s, axis=-1, keepdims=True))
    ps = [jnp.exp(s - m) for s, _ in parts]
    den = jnp.exp(sink - m)
    for p in ps:
        den = den + jnp.sum(p, axis=-1, keepdims=True)
    inv = 1.0 / den
    out = None
    for p, (_, v) in zip(ps, parts):
        o = jnp.dot((p * inv).astype(BF16), v, preferred_element_type=F32)
        out = o if out is None else out + o
    return out


def _swa_prompt_kernel(sink_ref, q_ref, kc_ref, kp_ref, vc_ref, vp_ref, o_ref):
    n = pl.program_id(1)
    qi = lax.broadcasted_iota(jnp.int32, (WINDOW, WINDOW), 0)
    kj = lax.broadcasted_iota(jnp.int32, (WINDOW, WINDOW), 1)
    mask_c = kj <= qi
    mask_p = kj > qi + jnp.where(n > 0, 0, WINDOW)
    scale = HEAD_DIM ** -0.5
    outs = []
    for h in range(ATT_HEADS):
        g = h // ATT_GROUP
        hs = slice(h * HEAD_DIM, (h + 1) * HEAD_DIM)
        gs = slice(g * HEAD_DIM, (g + 1) * HEAD_DIM)
        q = q_ref[:, hs].astype(BF16)
        sc = lax.dot_general(q, kc_ref[:, gs].astype(BF16), NT_DIMS, preferred_element_type=F32) * scale
        sp = lax.dot_general(q, kp_ref[:, gs].astype(BF16), NT_DIMS, preferred_element_type=F32) * scale
        sc = jnp.where(mask_c, sc, NEG)
        sp = jnp.where(mask_p, sp, NEG)
        outs.append(_sink_softmax_pv([(sp, vp_ref[:, gs].astype(BF16)), (sc, vc_ref[:, gs].astype(BF16))],
                                     sink_ref[h]))
    o_ref[...] = jnp.concatenate(outs, axis=1)


def swa_prompt(q, k, pa, sinks, b, t):
    nb = t // WINDOW
    cur = lambda w, c: pl.BlockSpec((WINDOW, w), lambda bi, ni: (bi * nb + ni, c))
    prev = lambda w, c: pl.BlockSpec((WINDOW, w), lambda bi, ni: (bi * nb + jnp.maximum(ni - 1, 0), c))
    vcol = (Q_W + KV_W) // KV_W
    return pl.pallas_call(
        _swa_prompt_kernel,
        grid=(b, nb),
        in_specs=[pl.BlockSpec(memory_space=pltpu.SMEM), cur(Q_W, 0), cur(KV_W, 0), prev(KV_W, 0),
                  cur(KV_W, vcol), prev(KV_W, vcol)],
        out_specs=cur(Q_W, 0),
        out_shape=jax.ShapeDtypeStruct((b * t, Q_W), F32),
        compiler_params=_params(("parallel", "parallel")),
    )(sinks, q, k, k, pa, pa)


def _swa_sample_kernel(sink_ref, q_ref, kb_ref, vb_ref, kn_ref, vn_ref, o_ref, *, bt, dt, wb):
    rows = ATT_GROUP * dt
    tb = lax.rem(lax.broadcasted_iota(jnp.int32, (rows, wb), 0), dt)
    jb = lax.broadcasted_iota(jnp.int32, (rows, wb), 1)
    mask_b = jb > tb + (wb - WINDOW)
    npad = kn_ref.shape[1]
    tn = lax.rem(lax.broadcasted_iota(jnp.int32, (rows, npad), 0), dt)
    jn = lax.broadcasted_iota(jnp.int32, (rows, npad), 1)
    mask_n = jn <= tn
    scale = HEAD_DIM ** -0.5
    for b in range(bt):
        for g in range(ATT_KV_HEADS):
            gs = slice(g * HEAD_DIM, (g + 1) * HEAD_DIM)
            q = q_ref[b, g].astype(BF16)
            sb = lax.dot_general(q, kb_ref[b, :, gs].astype(BF16), NT_DIMS, preferred_element_type=F32) * scale
            sn = lax.dot_general(q, kn_ref[b, :, gs].astype(BF16), NT_DIMS, preferred_element_type=F32) * scale
            sb = jnp.where(mask_b, sb, NEG)
            sn = jnp.where(mask_n, sn, NEG)
            o_ref[b, g] = _sink_softmax_pv(
                [(sb, vb_ref[b, :, gs].astype(BF16)), (sn, vn_ref[b, :, gs].astype(BF16))], sink_ref[g][:, :1])


def swa_sample(qs, kbuf, vbuf, knew, vnew, sink_rows):
    db, _, rows, _ = qs.shape
    dt = rows // ATT_GROUP
    wb = kbuf.shape[1]
    bt = _tile(db, (8, 4, 2, 1))
    blk = lambda a: pl.BlockSpec((bt,) + a.shape[1:], lambda i: (i,) + (0,) * (a.ndim - 1))
    return pl.pallas_call(
        functools.partial(_swa_sample_kernel, bt=bt, dt=dt, wb=wb),
        grid=(db // bt,),
        in_specs=[_full(sink_rows.shape), blk(qs), blk(kbuf), blk(vbuf), blk(knew), blk(vnew)],
        out_specs=blk(qs),
        out_shape=jax.ShapeDtypeStruct(qs.shape, F32),
        compiler_params=_params(("parallel",)),
    )(sink_rows, qs, kbuf, vbuf, knew, vnew)


CONV_HALO = 32


def _ln_swish(y, lg, lb):
    mu = jnp.mean(y, axis=-1, keepdims=True)
    var = jnp.mean(jnp.square(y - mu), axis=-1, keepdims=True)
    yn = (y - mu) * lax.rsqrt(var + LN_EPS) * lg + lb
    return yn * _sigmoid(yn)


def _conv_prompt_kernel(pb_ref, dw_ref, db_ref, lg_ref, lb_ref, y_ref, st_ref, ext_ref, *, tt):
    i = pl.program_id(1)

    @pl.when(i == 0)
    def _():
        ext_ref[0:CONV_HALO, :] = jnp.zeros((CONV_HALO, CONV_CH), F32)

    @pl.when(i > 0)
    def _():
        ext_ref[0:CONV_HALO, :] = ext_ref[tt:tt + CONV_HALO, :]

    pb = pb_ref[...]
    ext_ref[CONV_HALO:CONV_HALO + tt, :] = pb[:, :CONV_CH] * _sigmoid(pb[:, CONV_CH:])
    acc = jnp.zeros((tt, CONV_CH), F32) + db_ref[...]
    first = CONV_HALO - (CONV_WIDTH - 1)
    for j in range(CONV_WIDTH):
        acc = acc + ext_ref[pl.ds(first + j, tt), :] * dw_ref[j:j + 1, :]
    y_ref[...] = _ln_swish(acc, lg_ref[...], lb_ref[...])

    @pl.when(i == pl.num_programs(1) - 1)
    def _():
        st_ref[0] = ext_ref[tt:tt + CONV_HALO, :]


def conv_prompt(pb, dw, db, lg, lb, b, t):
    tt = _tile(t)
    nt = t // tt
    vec = lambda a: a.reshape(1, CONV_CH)
    return pl.pallas_call(
        functools.partial(_conv_prompt_kernel, tt=tt),
        grid=(b, nt),
        in_specs=[pl.BlockSpec((tt, B_COLS), lambda bi, i: (bi * nt + i, 0)), _full((CONV_WIDTH, CONV_CH)),
                  _full((1, CONV_CH)), _full((1, CONV_CH)), _full((1, CONV_CH))],
        out_specs=[pl.BlockSpec((tt, CONV_CH), lambda bi, i: (bi * nt + i, 0)),
                   pl.BlockSpec((1, CONV_HALO, CONV_CH), lambda bi, i: (bi, 0, 0))],
        out_shape=[jax.ShapeDtypeStruct((b * t, CONV_CH), F32), jax.ShapeDtypeStruct((b, CONV_HALO, CONV_CH), F32)],
        scratch_shapes=[pltpu.VMEM((tt + CONV_HALO, CONV_CH), F32)],
        compiler_params=_params(("arbitrary", "arbitrary")),
    )(pb, dw, vec(db), vec(lg), vec(lb))


def _conv_sample_kernel(c_ref, pb_ref, dw_ref, db_ref, lg_ref, lb_ref, y_ref, nc_ref, *, dt):
    nprev = CONV_WIDTH - 1
    us = []
    for t in range(dt):
        pb = pb_ref[t]
        us.append(pb[:, :CONV_CH] * _sigmoid(pb[:, CONV_CH:]))

    def ext(j):
        return c_ref[j] if j < nprev else us[j - nprev]

    for t in range(dt):
        acc = db_ref[...] + ext(t) * dw_ref[0:1, :]
        for j in range(1, CONV_WIDTH):
            acc = acc + ext(t + j) * dw_ref[j:j + 1, :]
        y_ref[t] = _ln_swish(acc, lg_ref[...], lb_ref[...])
    for j in range(nprev):
        nc_ref[j] = ext(j + dt)


def conv_sample(cache_t, pb_t, dw, db, lg, lb):
    nprev, dbt, _ = cache_t.shape
    dt = pb_t.shape[0]
    bt = _tile(dbt, (32, 16, 8))
    vec = lambda a: a.reshape(1, CONV_CH)
    blk = lambda lead, w: pl.BlockSpec((lead, bt, w), lambda i: (0, i, 0))
    return pl.pallas_call(
        functools.partial(_conv_sample_kernel, dt=dt),
        grid=(dbt // bt,),
        in_specs=[blk(nprev, CONV_CH), blk(dt, B_COLS), _full((CONV_WIDTH, CONV_CH)), _full((1, CONV_CH)),
                  _full((1, CONV_CH)), _full((1, CONV_CH))],
        out_specs=[blk(dt, CONV_CH), blk(nprev, CONV_CH)],
        out_shape=[jax.ShapeDtypeStruct((dt, dbt, CONV_CH), F32), jax.ShapeDtypeStruct((nprev, dbt, CONV_CH), F32)],
        compiler_params=_params(("parallel",)),
    )(cache_t, pb_t, dw, vec(db), vec(lg), vec(lb))


def _head_sum_matrix():
    h = np.arange(RWKV_W) // RWKV_HEAD
    return jnp.asarray((h[:, None] == h[None, :]).astype(np.float32))


def _rwkv_pre_math(pc, prev, c):
    mu_ref, w0_ref, w2_ref, a0_ref, a2_ref, g2_ref, kkp_ref, ka_ref, rk_ref, hs_ref = c
    xs = pc + (prev - pc) * mu_ref[...]
    o1, o2, o3 = RWKV_W, 2 * RWKV_W, 3 * RWKV_W
    o4 = o3 + DECAY_LORA
    o5 = o4 + ICLR_LORA
    r, k, v = xs[:, :o1], xs[:, o1:o2], xs[:, o2:o3]
    wd, ad, gd = xs[:, o3:o4], xs[:, o4:o5], xs[:, o5:]
    y = -(w0_ref[...] + jnp.dot(jnp.tanh(wd).astype(BF16), w2_ref[...], preferred_element_type=F32))
    softplus = jnp.maximum(y, 0.0) + jnp.log(1.0 + jnp.exp(-jnp.abs(y)))
    decay = jnp.exp(-jnp.exp(-softplus - 0.5))
    a = _sigmoid(a0_ref[...] + jnp.dot(ad.astype(BF16), a2_ref[...], preferred_element_type=F32))
    g = jnp.dot(_sigmoid(gd).astype(BF16), g2_ref[...], preferred_element_type=F32)
    hs = hs_ref[...]
    kk = k * kkp_ref[...]
    norm = jnp.sqrt(jnp.dot(kk * kk, hs, precision=HIGHEST, preferred_element_type=F32))
    kk = kk / jnp.maximum(norm, 1e-12)
    k_mod = k * (1.0 + (a - 1.0) * ka_ref[...])
    bonus = jnp.dot(r * k_mod * rk_ref[...], hs, precision=HIGHEST, preferred_element_type=F32) * v
    return r, decay, k_mod, -kk, kk * a, v, g, bonus


RWKV_PRE_OUTS = 8
SHIFT_HALO = SUBLANES


def _rwkv_pre_prompt_kernel(pc_ref, *refs, tt):
    consts, outs, ext_ref = refs[:10], refs[10:10 + RWKV_PRE_OUTS], refs[-1]
    i = pl.program_id(1)

    @pl.when(i == 0)
    def _():
        ext_ref[0:SHIFT_HALO, :] = jnp.zeros((SHIFT_HALO, C_COLS), F32)

    @pl.when(i > 0)
    def _():
        ext_ref[0:SHIFT_HALO, :] = ext_ref[tt:tt + SHIFT_HALO, :]

    pc = pc_ref[...]
    ext_ref[SHIFT_HALO:SHIFT_HALO + tt, :] = pc
    res = _rwkv_pre_math(pc, ext_ref[pl.ds(SHIFT_HALO - 1, tt), :], consts)
    for o_ref, val in zip(outs, res):
        o_ref[...] = val


def _rwkv_pre_sample_kernel(pc_ref, shift_ref, *refs, dt):
    consts, outs = refs[:10], refs[10:10 + RWKV_PRE_OUTS]
    for t in range(dt):
        res = _rwkv_pre_math(pc_ref[t], shift_ref[...] if t == 0 else pc_ref[t - 1], consts)
        for o_ref, val in zip(outs, res):
            o_ref[t] = val


def _rwkv_pre_consts(lp):
    vec = lambda a: a.reshape(1, -1)
    return [vec(lp['rk_mu']), vec(lp['rk_w0']), lp['rk_w2'].astype(BF16), vec(lp['rk_a0']), lp['rk_a2'].astype(BF16),
            lp['rk_g2'].astype(BF16), vec(lp['rk_kk']), vec(lp['rk_ka']), vec(lp['rk_rk']), _head_sum_matrix()]


def rwkv_pre_prompt(pc, lp, b, t):
    tt = _tile(t, (256, 128))
    nt = t // tt
    consts = _rwkv_pre_consts(lp)
    row = lambda w: pl.BlockSpec((tt, w), lambda bi, i: (bi * nt + i, 0))
    return pl.pallas_call(
        functools.partial(_rwkv_pre_prompt_kernel, tt=tt),
        grid=(b, nt),
        in_specs=[row(C_COLS)] + [_full(c.shape) for c in consts],
        out_specs=[row(RWKV_W)] * RWKV_PRE_OUTS,
        out_shape=[jax.ShapeDtypeStruct((b * t, RWKV_W), F32)] * RWKV_PRE_OUTS,
        scratch_shapes=[pltpu.VMEM((tt + SHIFT_HALO, C_COLS), F32)],
        compiler_params=_params(("arbitrary", "arbitrary")),
    )(pc, *consts)


def rwkv_pre_sample(pc_t, shift, lp):
    dt, dbt, _ = pc_t.shape
    bt = _tile(dbt, (64, 32, 16, 8))
    consts = _rwkv_pre_consts(lp)
    blk = lambda w: pl.BlockSpec((dt, bt, w), lambda i: (0, i, 0))
    return pl.pallas_call(
        functools.partial(_rwkv_pre_sample_kernel, dt=dt),
        grid=(dbt // bt,),
        in_specs=[blk(C_COLS), pl.BlockSpec((bt, C_COLS), lambda i: (i, 0))] + [_full(c.shape) for c in consts],
        out_specs=[blk(RWKV_W)] * RWKV_PRE_OUTS,
        out_shape=[jax.ShapeDtypeStruct((dt, dbt, RWKV_W), F32)] * RWKV_PRE_OUTS,
        compiler_params=_params(("parallel",)),
    )(pc_t, shift, *consts)


def _scan_select_matrices():
    e = np.zeros((SCAN_TSUB, LANES, LANES), np.float32)
    vl = LANES // SCAN_PAIRS
    for tl in range(SCAN_TSUB):
        for g in range(SCAN_PAIRS):
            e[tl, tl * SCAN_PAIRS + g, g * vl:(g + 1) * vl] = 1.0
    return jnp.asarray(np.concatenate([e, e, e], axis=1), dtype=BF16)


def _sublane_allsum(x):
    x = x + pltpu.roll(x, 4, 0)
    x = x + pltpu.roll(x, 2, 0)
    return x + pltpu.roll(x, 1, 0)


def _scan_kernel(x_ref, v_ref, s0_ref, e_ref, o_ref, sf_ref, z_ref, tile_ref, xs_ref, *, groups, steps):
    c = pl.program_id(1)
    nvh = z_ref.shape[0]
    nkb = RWKV_HEAD // SUBLANES
    sub = lax.broadcasted_iota(jnp.int32, (SUBLANES, LANES), 0)
    R_ROW, W_ROW, K_ROW, A_ROW, B_ROW = (j * RWKV_HEAD for j in range(5))

    @pl.when(c == 0)
    def _():
        z_ref[...] = s0_ref[0]

    def split_group(th):
        xg = x_ref[0, th]
        hi = xg.astype(BF16)
        r1 = xg - hi.astype(F32)
        mid = r1.astype(BF16)
        lo = (r1 - mid.astype(F32)).astype(BF16)
        xs_ref[:, 0:LANES] = hi
        xs_ref[:, LANES:2 * LANES] = mid
        xs_ref[:, 2 * LANES:3 * LANES] = lo

    def spread(tl, buf):
        tile_ref[buf] = jnp.dot(xs_ref[...], e_ref[tl], preferred_element_type=F32)

    split_group(0)
    spread(0, 0)

    def body(th, carry):
        for tl in range(steps):
            cur = tl % 2
            if tl + 1 < steps:
                spread(tl + 1, 1 - cur)
            else:
                split_group(jnp.minimum(th + 1, groups - 1))
                spread(0, 1 - cur)
            t = th * steps + tl
            vrow = v_ref[0, t]
            blk = lambda row0, kb: tile_ref[cur, row0 + kb * SUBLANES:row0 + (kb + 1) * SUBLANES, :]
            sa = []
            for vh in range(nvh):
                acc = None
                for kb in range(nkb):
                    p = z_ref[vh, kb * SUBLANES:(kb + 1) * SUBLANES, :] * blk(A_ROW, kb)
                    acc = p if kb == 0 else acc + p
                sa.append(_sublane_allsum(acc))
            orow = jnp.zeros((SUBLANES, LANES), F32)
            for vh in range(nvh):
                vb = jnp.broadcast_to(vrow[vh:vh + 1], (SUBLANES, LANES))
                acc = None
                for kb in range(nkb):
                    ks = slice(kb * SUBLANES, (kb + 1) * SUBLANES)
                    zn = z_ref[vh, ks, :] * blk(W_ROW, kb) + blk(B_ROW, kb) * sa[vh] + blk(K_ROW, kb) * vb
                    z_ref[vh, ks, :] = zn
                    p = zn * blk(R_ROW, kb)
                    acc = p if kb == 0 else acc + p
                orow = jnp.where(sub == vh, _sublane_allsum(acc), orow)
            o_ref[0, t] = orow
        return carry

    lax.fori_loop(0, groups, body, 0)

    @pl.when(c == pl.num_programs(1) - 1)
    def _():
        sf_ref[0] = z_ref[...]


def rwkv_scan(x5, v, s0, nseq, t):
    spb = SCAN_PAIRS // RWKV_HEADS
    assert nseq % spb == 0
    nblk = nseq // spb
    vl = LANES // SCAN_PAIRS
    vh = RWKV_HEAD // vl
    steps = min(SCAN_TSUB, t)
    assert t % steps == 0 and steps % 2 == 0
    t8 = t // steps
    def key_layout(a):
        a = a.reshape(nblk, spb, t8, steps, RWKV_HEADS, RWKV_HEAD)
        if steps < SCAN_TSUB:
            a = jnp.pad(a, ((0, 0),) * 3 + ((0, SCAN_TSUB - steps),) + ((0, 0),) * 2)
        return a.transpose(0, 2, 5, 3, 1, 4).reshape(nblk, t8, RWKV_HEAD, LANES)

    xk = jnp.concatenate([key_layout(a) for a in x5], axis=2)
    vk = v.reshape(nblk, spb, t, RWKV_HEADS, vh, vl).transpose(0, 2, 4, 1, 3, 5).reshape(nblk, t, vh, LANES)
    sk = s0.reshape(nblk, spb, RWKV_HEADS, vh, vl, RWKV_HEAD).transpose(0, 3, 5, 1, 2, 4).reshape(
        nblk, vh, RWKV_HEAD, LANES)
    groups = _tile(t8, (32, 16, 8, 4, 2, 1))
    nchunk = t8 // groups
    ch = groups * steps
    o, sf = pl.pallas_call(
        functools.partial(_scan_kernel, groups=groups, steps=steps),
        grid=(nblk, nchunk),
        in_specs=[pl.BlockSpec((1, groups, 5 * RWKV_HEAD, LANES), lambda i, c: (i, c, 0, 0)),
                  pl.BlockSpec((1, ch, vh, LANES), lambda i, c: (i, c, 0, 0)),
                  pl.BlockSpec((1, vh, RWKV_HEAD, LANES), lambda i, c: (i, 0, 0, 0)),
                  _full((SCAN_TSUB, 3 * LANES, LANES))],
        out_specs=[pl.BlockSpec((1, ch, vh, LANES), lambda i, c: (i, c, 0, 0)),
                   pl.BlockSpec((1, vh, RWKV_HEAD, LANES), lambda i, c: (i, 0, 0, 0))],
        out_shape=[jax.ShapeDtypeStruct((nblk, t, vh, LANES), F32),
                   jax.ShapeDtypeStruct((nblk, vh, RWKV_HEAD, LANES), F32)],
        scratch_shapes=[pltpu.VMEM((vh, RWKV_HEAD, LANES), F32), pltpu.VMEM((2, 5 * RWKV_HEAD, LANES), F32),
                        pltpu.VMEM((5 * RWKV_HEAD, 3 * LANES), BF16)],
        compiler_params=_params(("arbitrary", "arbitrary")),
    )(xk, vk, sk, _scan_select_matrices())
    o = o.reshape(nblk, t, vh, spb, RWKV_HEADS, vl).transpose(0, 3, 1, 4, 2, 5).reshape(nseq * t, RWKV_W)
    sf = sf.reshape(nblk, vh, RWKV_HEAD, spb, RWKV_HEADS, vl).transpose(0, 3, 4, 1, 5, 2).reshape(
        nseq, RWKV_HEADS, RWKV_HEAD, RWKV_HEAD)
    return o, sf


def _merge_kernel(x_ref, oa_ref, yb_ref, oc_ref, bonus_ref, g_ref, pg_ref, wa_ref, wb_ref, bb_ref, wc_ref, wo_ref,
                  gng_ref, gnb_ref, hs_ref, o_ref):
    hs = hs_ref[...] * (1.0 / RWKV_HEAD)
    oc = oc_ref[...]
    mu = jnp.dot(oc, hs, precision=HIGHEST, preferred_element_type=F32)
    dev = oc - mu
    var = jnp.dot(dev * dev, hs, precision=HIGHEST, preferred_element_type=F32)
    on = dev * lax.rsqrt(var + GN_EPS) * gng_ref[...] + gnb_ref[...]
    yc = (on + bonus_ref[...]) * g_ref[...]
    d = D_MODEL
    dot = lambda a, w: jnp.dot(a.astype(BF16), w[...], preferred_element_type=F32)
    merged = (_sigmoid(pg_ref[:, 0:d]) * dot(oa_ref[...], wa_ref)
              + _sigmoid(pg_ref[:, d:2 * d]) * (dot(yb_ref[...], wb_ref) + bb_ref[...])
              + _sigmoid(pg_ref[:, 2 * d:3 * d]) * dot(yc, wc_ref))
    o_ref[...] = x_ref[...] + dot(merged, wo_ref)


def merge_out(x, oa, yb, oc, bonus, g, pg, lp):
    n = x.shape[0]
    tm = _tile(n, (256, 128, 64, 32, 16, 8))
    row = lambda w: pl.BlockSpec((tm, w), lambda i: (i, 0))
    vec = lambda a: a.reshape(1, -1)
    consts = [lp['w_a_out'].astype(BF16), lp['w_b_out'].astype(BF16), vec(lp['b_b_out']), lp['w_c_out'].astype(BF16),
              lp['w_o'].astype(BF16), vec(lp['rk_gn_g']), vec(lp['rk_gn_b']), _head_sum_matrix()]
    return pl.pallas_call(
        _merge_kernel,
        grid=(n // tm,),
        in_specs=[row(D_MODEL), row(Q_W), row(CONV_CH), row(RWKV_W), row(RWKV_W), row(RWKV_W), row(G_COLS)]
        + [_full(c.shape) for c in consts],
        out_specs=row(D_MODEL),
        out_shape=jax.ShapeDtypeStruct((n, D_MODEL), F32),
        compiler_params=_params(("parallel",)),
    )(x, oa, yb, oc, bonus, g, pg, *consts)


def _cross_attn_kernel(q_ref, k_ref, v_ref, o_ref):
    scale = CA_HEAD_DIM ** -0.5
    outs = []
    for h in range(CA_HEADS):
        hs = slice(h * CA_HEAD_DIM, (h + 1) * CA_HEAD_DIM)
        q = q_ref[0, :, hs].astype(BF16)
        s = lax.dot_general(q, k_ref[0, :, hs].astype(BF16), NT_DIMS, preferred_element_type=F32) * scale
        m = jnp.max(s, axis=-1, keepdims=True)
        p = jnp.exp(s - m)
        p = p / jnp.sum(p, axis=-1, keepdims=True)
        outs.append(jnp.dot(p.astype(BF16), v_ref[0, :, hs].astype(BF16), preferred_element_type=F32))
    o_ref[0] = jnp.concatenate(outs, axis=1)


def cross_attn(q, mk, mv, tiles_per_seq):
    nt, tq, d = q.shape
    m = mk.shape[1]
    kv = pl.BlockSpec((1, m, d), lambda i: (i // tiles_per_seq, 0, 0))
    qs = pl.BlockSpec((1, tq, d), lambda i: (i, 0, 0))
    return pl.pallas_call(
        _cross_attn_kernel,
        grid=(nt,),
        in_specs=[qs, kv, kv],
        out_specs=qs,
        out_shape=jax.ShapeDtypeStruct(q.shape, F32),
        compiler_params=_params(("parallel",)),
    )(q, mk, mv)


def _staircase():
    return [(i, PEER_TOPK // (i + 1)) for i in range(PEER_TOPK)]


PEER_CAND = sum(nj for _, nj in _staircase())
PEER_CAND_PAD = -(-PEER_CAND // SUBLANES) * SUBLANES


def _extract_topk(cur, out_ref, base, with_rank=False):
    rank = jnp.full(cur.shape, float(PEER_TOPK), F32) if with_rank else None
    for k in range(PEER_TOPK):
        m = jnp.max(cur, axis=0, keepdims=True)
        out_ref[base + k:base + k + 1, :] = m
        hit = cur == m
        if with_rank:
            rank = jnp.where(hit, float(k), rank)
        if k + 1 < PEER_TOPK:
            cur = jnp.where(hit, -jnp.inf, cur)
    return rank


def _peer_select_kernel(qp_ref, keys_ref, r1_ref, rho_ref, m2_ref, e2_ref, s_ref, tv_ref, cand_ref, best_ref):
    nsub = 2 * PEER_HEADS
    for hc in range(nsub):
        q = qp_ref[:, hc * PEER_HALF:(hc + 1) * PEER_HALF].astype(BF16)
        st = lax.dot_general(keys_ref[hc], q, NT_DIMS, preferred_element_type=F32)
        s_ref[hc] = st
        rank = _extract_topk(st, tv_ref, hc * PEER_TOPK, with_rank=hc % 2 == 0)
        if hc % 2 == 0:
            r1_ref[hc // 2] = rank
    tm = qp_ref.shape[0]
    for h in range(PEER_HEADS):
        b1 = 2 * h * PEER_TOPK
        b2 = b1 + PEER_TOPK
        cand_ref[PEER_CAND_PAD - SUBLANES:PEER_CAND_PAD, :] = jnp.full((SUBLANES, tm), -jnp.inf, F32)
        off = 0
        for i, nj in _staircase():
            cand_ref[off:off + nj, :] = tv_ref[b1 + i:b1 + i + 1, :] + tv_ref[b2:b2 + nj, :]
            off += nj
        _extract_topk(cand_ref[...], best_ref, 0)
        best = best_ref[...]
        z = jnp.sum(jnp.exp(best - best[0:1, :]), axis=0, keepdims=True)
        thr = best[PEER_TOPK - 1:PEER_TOPK, :]
        s2 = s_ref[2 * h + 1]
        m2 = jnp.zeros(s2.shape, F32)
        for i in range(PEER_TOPK):
            m2 = jnp.where((tv_ref[b1 + i:b1 + i + 1, :] + s2) >= thr, float(i + 1), m2)
        m2_ref[h] = m2
        e2_ref[h] = jnp.exp(s2 - tv_ref[b2:b2 + 1, :])
        rho_ref[h] = jnp.exp(s_ref[2 * h] - tv_ref[b1:b1 + 1, :]) * (1.0 / z)


def peer_select(qp, keys):
    n = qp.shape[0]
    tm = _tile(n, (256, 128))
    nsub = 2 * PEER_HEADS
    blk = pl.BlockSpec((PEER_HEADS, N_KEYS, tm), lambda i: (0, 0, i))
    shp = lambda dt: jax.ShapeDtypeStruct((PEER_HEADS, N_KEYS, n), dt)
    return pl.pallas_call(
        _peer_select_kernel,
        grid=(n // tm,),
        in_specs=[pl.BlockSpec((tm, nsub * PEER_HALF), lambda i: (i, 0)), _full(keys.shape)],
        out_specs=[blk, blk, blk, blk],
        out_shape=[shp(F32)] * 4,
        scratch_shapes=[pltpu.VMEM((nsub, N_KEYS, tm), F32), pltpu.VMEM((nsub * PEER_TOPK, tm), F32),
                        pltpu.VMEM((PEER_CAND_PAD, tm), F32), pltpu.VMEM((PEER_TOPK, tm), F32)],
        compiler_params=_params(("parallel",)),
    )(qp, keys)


PEER_PACK_ROWS = 16
PEER_EXPERT_TILE = 4 * N_KEYS


def _gelu_tanh(x):
    return 0.5 * x * (1.0 + jnp.tanh(0.7978845608028654 * (x + 0.044715 * (x * x * x))))


def _peer_dense_kernel(x_ref, g_ref, r1_ref, rho_ref, m2_ref, e2_ref, u_ref, vt_ref, o_ref, xnt_ref, acc_ref, ht_ref,
                       wh_ref, m2b_ref, e2b_ref, *, te):
    e = pl.program_id(1)
    tm = x_ref.shape[0]
    pk = PEER_PACK_ROWS

    @pl.when(e == 0)
    def _():
        x = x_ref[...]
        xn = x * lax.rsqrt(jnp.mean(x * x, axis=-1, keepdims=True) + RMS_EPS) * g_ref[...]
        xnt_ref[...] = xn.T.astype(BF16)
        acc_ref[...] = jnp.zeros_like(acc_ref)
        m2b_ref[...] = m2_ref[...].astype(BF16)
        e2b_ref[...] = e2_ref[...].astype(BF16)

    n_a = te // N_KEYS
    r1_rows, rho_rows = [], []
    for al in range(n_a):
        a = e * n_a + al
        for h in range(PEER_HEADS):
            r1_rows.append(jnp.broadcast_to(r1_ref[h, pl.ds(a, 1), :], (pk, tm)).astype(BF16))
            rho_rows.append(jnp.broadcast_to(rho_ref[h, pl.ds(a, 1), :], (pk, tm)).astype(BF16))

    ht_ref[...] = jnp.dot(u_ref[...], xnt_ref[...], preferred_element_type=F32)
    zero = jnp.zeros((), BF16)
    for al in range(n_a):
        for cl in range(tm // LANES):
            cs = slice(cl * LANES, (cl + 1) * LANES)
            for rb in range(N_KEYS // pk):
                rs = slice(rb * pk, (rb + 1) * pk)
                w = None
                for h in range(PEER_HEADS):
                    sel = r1_rows[al * PEER_HEADS + h][:, cs] < m2b_ref[h, rs, cs]
                    contrib = jnp.where(sel, e2b_ref[h, rs, cs] * rho_rows[al * PEER_HEADS + h][:, cs], zero)
                    w = contrib if w is None else w + contrib
                hr = slice(al * N_KEYS + rb * pk, al * N_KEYS + (rb + 1) * pk)
                wh_ref[hr, cs] = w * _gelu_tanh(ht_ref[hr, cs]).astype(BF16)
    acc_ref[...] += jnp.dot(vt_ref[...], wh_ref[...], preferred_element_type=F32)

    @pl.when(e == pl.num_programs(1) - 1)
    def _():
        o_ref[...] = x_ref[...] + acc_ref[...].T


def peer_dense(x, g, sel, u, v_t, layer):
    n, d = x.shape
    ne = u.shape[1]
    tm = _tile(n, (512, 256, 128))
    te = PEER_EXPERT_TILE
    blk = pl.BlockSpec((PEER_HEADS, N_KEYS, tm), lambda i, e: (0, 0, i))
    return pl.pallas_call(
        functools.partial(_peer_dense_kernel, te=te),
        grid=(n // tm, ne // te),
        in_specs=[pl.BlockSpec((tm, d), lambda i, e: (i, 0)), _full((1, d)), blk, blk, blk, blk,
                  pl.BlockSpec((None, te, d), lambda i, e: (layer, e, 0)),
                  pl.BlockSpec((None, d, te), lambda i, e: (layer, 0, e))],
        out_specs=pl.BlockSpec((tm, d), lambda i, e: (i, 0)),
        out_shape=jax.ShapeDtypeStruct((n, d), F32),
        scratch_shapes=[pltpu.VMEM((d, tm), BF16), pltpu.VMEM((d, tm), F32), pltpu.VMEM((te, tm), F32),
                        pltpu.VMEM((te, tm), BF16), pltpu.VMEM((PEER_HEADS, N_KEYS, tm), BF16),
                        pltpu.VMEM((PEER_HEADS, N_KEYS, tm), BF16)],
        compiler_params=_params(("parallel", "arbitrary")),
    )(x, g.reshape(1, d), *sel, u, v_t)


def kernel(x_prompt, x_sample, cache_win_k, cache_win_v, cache_conv, state_shift, state_wkv, cache_mem_k, cache_mem_v,
           mem_prompt, g_mix, w_in, att_sinks, w_a_out, conv_dw, conv_db, conv_ln_g, conv_ln_b, w_b_out, b_b_out,
           rk_mu, rk_w0, rk_w2, rk_a0, rk_a2, rk_g2, rk_kk, rk_ka, rk_rk, rk_gn_g, rk_gn_b, w_c_out, w_o,
           g_ca, g_mem, w_cq, w_mk, w_mv, w_co, g_ffn, w_pq, peer_keys, peer_u, peer_v, g_final):
    b, t, d = x_prompt.shape
    db, dt, _ = x_sample.shape
    depth = w_in.shape[0]
    wb = cache_win_k.shape[2]
    mem_len = mem_prompt.shape[1]
    n_p, n_s = b * t, db * dt
    assert t % WINDOW == 0 and wb == WINDOW and dt <= SUBLANES

    xp = x_prompt.reshape(n_p, d)
    xs = x_sample.reshape(n_s, d)
    tabs_p = _rotary_tables(jnp.tile(jnp.arange(t, dtype=F32), b))
    tabs_s = _rotary_tables(jnp.tile(PAST_LEN + jnp.arange(dt, dtype=F32), db))
    tq = _tile(t)
    new_pad = SUBLANES - dt
    u_bf = peer_u.astype(BF16)
    vt_bf = jnp.swapaxes(peer_v.astype(BF16), 1, 2)
    mem = mem_prompt.reshape(b * mem_len, d)
    time_major = lambda a: a.reshape(db, dt, a.shape[-1]).transpose(1, 0, 2)
    seq_major = lambda a: a.transpose(1, 0, 2).reshape(n_s, a.shape[-1])
    k4 = lambda a, nb_, tt_: a.reshape(nb_, tt_, ATT_KV_HEADS, HEAD_DIM)

    outs = [[] for _ in range(12)]
    for l in range(depth):
        lp = {'rk_mu': rk_mu[l], 'rk_w0': rk_w0[l], 'rk_w2': rk_w2[l], 'rk_a0': rk_a0[l], 'rk_a2': rk_a2[l],
              'rk_g2': rk_g2[l], 'rk_kk': rk_kk[l], 'rk_ka': rk_ka[l], 'rk_rk': rk_rk[l].reshape(-1),
              'rk_gn_g': rk_gn_g[l], 'rk_gn_b': rk_gn_b[l], 'w_a_out': w_a_out[l], 'w_b_out': w_b_out[l],
              'b_b_out': b_b_out[l], 'w_c_out': w_c_out[l], 'w_o': w_o[l]}
        win = w_in[l].astype(BF16)
        c0, c1, c2 = A_COLS, A_COLS + B_COLS, A_COLS + B_COLS + C_COLS
        w_seg = [win[:, :c0], win[:, c0:c1], win[:, c1:c2], win[:, c2:]]
        pa_p, pb_p, pc_p, pg_p = (norm_matmul(xp, g_mix[l], w) for w in w_seg)
        pa_s, pb_s, pc_s, pg_s = (norm_matmul(xs, g_mix[l], w) for w in w_seg)

        q_p, k_p = rotary_qk(pa_p, tabs_p)
        q_s, k_s = rotary_qk(pa_s, tabs_s)
        oa_p = swa_prompt(q_p, k_p, pa_p, att_sinks[l], b, t)
        qs4 = q_s.reshape(db, dt, ATT_KV_HEADS, ATT_GROUP, HEAD_DIM).transpose(0, 2, 3, 1, 4).reshape(
            db, ATT_KV_HEADS, ATT_GROUP * dt, HEAD_DIM)
        k_new = k_s.reshape(db, dt, KV_W)
        v_new = pa_s[:, Q_W + KV_W:].reshape(db, dt, KV_W)
        padn = lambda a: jnp.pad(a, ((0, 0), (0, new_pad), (0, 0)))
        sink_rows = jnp.broadcast_to(
            jnp.repeat(att_sinks[l].reshape(ATT_KV_HEADS, ATT_GROUP), dt, axis=1)[:, :, None],
            (ATT_KV_HEADS, ATT_GROUP * dt, LANES))
        oa_s = swa_sample(qs4, cache_win_k[l].reshape(db, wb, KV_W), cache_win_v[l].reshape(db, wb, KV_W),
                          padn(k_new), padn(v_new), sink_rows)
        oa_s = oa_s.reshape(db, ATT_KV_HEADS, ATT_GROUP, dt, HEAD_DIM).transpose(0, 3, 1, 2, 4).reshape(n_s, Q_W)

        yb_p, conv_st = conv_prompt(pb_p, conv_dw[l], conv_db[l], conv_ln_g[l], conv_ln_b[l], b, t)
        yb_s, conv_new = conv_sample(cache_conv[l].transpose(1, 0, 2), time_major(pb_s),
                                     conv_dw[l], conv_db[l], conv_ln_g[l], conv_ln_b[l])
        yb_s = seq_major(yb_s)

        *x5_p, v_p, g_p, bonus_p = rwkv_pre_prompt(pc_p, lp, b, t)
        pre_s = [seq_major(a) for a in rwkv_pre_sample(time_major(pc_s), state_shift[l], lp)]
        *x5_s, v_s, g_s, bonus_s = pre_s
        oc_p, wkv_p = rwkv_scan(x5_p, v_p, jnp.zeros((b, RWKV_HEADS, RWKV_HEAD, RWKV_HEAD), F32), b, t)
        oc_s, wkv_s = rwkv_scan(x5_s, v_s, state_wkv[l], db, dt)

        xp = merge_out(xp, oa_p, yb_p, oc_p, bonus_p, g_p, pg_p, lp)
        xs = merge_out(xs, oa_s, yb_s, oc_s, bonus_s, g_s, pg_s, lp)

        mk = norm_matmul(mem, g_mem[l], w_mk[l].astype(BF16))
        mv = norm_matmul(mem, g_mem[l], w_mv[l].astype(BF16))
        wcq, wco = w_cq[l].astype(BF16), w_co[l].astype(BF16)
        ca_p = cross_attn(norm_matmul(xp, g_ca[l], wcq).reshape(n_p // tq, tq, d), mk.reshape(b, mem_len, d),
                          mv.reshape(b, mem_len, d), t // tq)
        ca_s = cross_attn(norm_matmul(xs, g_ca[l], wcq).reshape(db, dt, d), cache_mem_k[l].reshape(db, mem_len, d),
                          cache_mem_v[l].reshape(db, mem_len, d), 1)
        xp = matmul_residual(ca_p.reshape(n_p, d), wco, xp)
        xs = matmul_residual(ca_s.reshape(n_s, d), wco, xs)

        wpq = w_pq[l].astype(BF16)
        keys = peer_keys[l].reshape(2 * PEER_HEADS, N_KEYS, PEER_HALF).astype(BF16)
        xp = peer_dense(xp, g_ffn[l], peer_select(norm_matmul(xp, g_ffn[l], wpq), keys), u_bf, vt_bf, l)
        xs = peer_dense(xs, g_ffn[l], peer_select(norm_matmul(xs, g_ffn[l], wpq), keys), u_bf, vt_bf, l)

        outs[0].append(k4(k_p, b, t)[:, t - wb:])
        outs[1].append(k4(pa_p[:, Q_W + KV_W:], b, t)[:, t - wb:])
        outs[2].append(conv_st[:, CONV_HALO - (CONV_WIDTH - 1):])
        outs[3].append(pc_p.reshape(b, t, C_COLS)[:, -1])
        outs[4].append(wkv_p)
        outs[5].append(mk.reshape(b, mem_len, CA_HEADS, CA_HEAD_DIM))
        outs[6].append(mv.reshape(b, mem_len, CA_HEADS, CA_HEAD_DIM))
        outs[7].append(jnp.concatenate([cache_win_k[l][:, dt:], k4(k_new, db, dt)], axis=1))
        outs[8].append(jnp.concatenate([cache_win_v[l][:, dt:], k4(v_new, db, dt)], axis=1))
        outs[9].append(conv_new.transpose(1, 0, 2))
        outs[10].append(pc_s.reshape(db, dt, C_COLS)[:, -1])
        outs[11].append(wkv_s)

    yp = final_norm(xp, g_final)
    ys = final_norm(xs, g_final)
    return (yp.reshape(b, t, d), ys.reshape(db, dt, d)) + tuple(jnp.stack(o) for o in outs)
```

```python
import functools

import numpy as np
import jax
import jax.numpy as jnp
from jax import lax
from jax.experimental import pallas as pl
from jax.experimental.pallas import tpu as pltpu

F32 = jnp.float32
BF16 = jnp.bfloat16
HIGHEST = lax.Precision.HIGHEST

D_MODEL = 1024
PAST_LEN = 8192
ATT_HEADS = 8
ATT_KV_HEADS = 2
ATT_GROUP = ATT_HEADS // ATT_KV_HEADS
HEAD_DIM = 64
ROT_DIM = HEAD_DIM // 4
ROPE_THETA = 500000.0
WINDOW = 128
CONV_CH = 512
CONV_WIDTH = 31
LN_EPS = 1e-5
RWKV_HEADS = 8
RWKV_HEAD = 64
RWKV_W = RWKV_HEADS * RWKV_HEAD
DECAY_LORA = 64
ICLR_LORA = 64
GATE_LORA = 128
GN_EPS = 64e-5
CA_HEADS = 4
CA_HEAD_DIM = D_MODEL // CA_HEADS
PEER_HEADS = 8
N_KEYS = 128
PEER_TOPK = 16
PEER_HALF = 128
RMS_EPS = 1e-6
NEG = -1e30

Q_W = ATT_HEADS * HEAD_DIM
KV_W = ATT_KV_HEADS * HEAD_DIM
A_COLS = Q_W + 2 * KV_W
B_COLS = 2 * CONV_CH
C_COLS = 3 * RWKV_W + DECAY_LORA + ICLR_LORA + GATE_LORA
G_COLS = 3 * D_MODEL

VMEM_LIMIT_V7X = 56 * 1024 * 1024
LANES = 128
SUBLANES = 8
SCAN_PAIRS = 16
SCAN_TSUB = 8
NT_DIMS = (((1,), (1,)), ((), ()))


def _tile(n, prefs=(512, 256, 128, 64, 32, 16, 8)):
    for t in prefs:
        if n % t == 0:
            return t
    raise ValueError(f"no tile for {n}")


def _params(sem):
    return pltpu.CompilerParams(dimension_semantics=sem, vmem_limit_bytes=VMEM_LIMIT_V7X)


def _sigmoid(x):
    return 1.0 / (1.0 + jnp.exp(-x))


def _full(shape):
    nd = len(shape)
    return pl.BlockSpec(shape, lambda *_: (0,) * nd)


def _norm_matmul_kernel(x_ref, g_ref, w_ref, o_ref):
    x = x_ref[...]
    h = x * lax.rsqrt(jnp.mean(x * x, axis=-1, keepdims=True) + RMS_EPS) * g_ref[...]
    o_ref[...] = jnp.dot(h.astype(BF16), w_ref[...], preferred_element_type=F32)


def norm_matmul(x, g, w):
    n, d = x.shape
    c = w.shape[1]
    tm = _tile(n)
    return pl.pallas_call(
        _norm_matmul_kernel,
        grid=(n // tm,),
        in_specs=[pl.BlockSpec((tm, d), lambda i: (i, 0)), _full((1, d)), _full((d, c))],
        out_specs=pl.BlockSpec((tm, c), lambda i: (i, 0)),
        out_shape=jax.ShapeDtypeStruct((n, c), F32),
        compiler_params=_params(("parallel",)),
    )(x, g.reshape(1, d), w)


def _matmul_res_kernel(a_ref, w_ref, x_ref, o_ref):
    o_ref[...] = x_ref[...] + jnp.dot(a_ref[...].astype(BF16), w_ref[...], preferred_element_type=F32)


def matmul_residual(a, w, x):
    n, k = a.shape
    c = w.shape[1]
    tm = _tile(n)
    return pl.pallas_call(
        _matmul_res_kernel,
        grid=(n // tm,),
        in_specs=[pl.BlockSpec((tm, k), lambda i: (i, 0)), _full((k, c)), pl.BlockSpec((tm, c), lambda i: (i, 0))],
        out_specs=pl.BlockSpec((tm, c), lambda i: (i, 0)),
        out_shape=jax.ShapeDtypeStruct((n, c), F32),
        compiler_params=_params(("parallel",)),
    )(a, w, x)


def _final_norm_kernel(x_ref, g_ref, o_ref):
    x = x_ref[...]
    o_ref[...] = x * lax.rsqrt(jnp.mean(x * x, axis=-1, keepdims=True) + RMS_EPS) * g_ref[...]


def final_norm(x, g):
    n, d = x.shape
    tm = _tile(n)
    return pl.pallas_call(
        _final_norm_kernel,
        grid=(n // tm,),
        in_specs=[pl.BlockSpec((tm, d), lambda i: (i, 0)), _full((1, d))],
        out_specs=pl.BlockSpec((tm, d), lambda i: (i, 0)),
        out_shape=jax.ShapeDtypeStruct((n, d), F32),
        compiler_params=_params(("parallel",)),
    )(x, g.reshape(1, d))


def _rotary_tables(pos):
    half = ROT_DIM // 2
    inv = ROPE_THETA ** (-2.0 * jnp.arange(half, dtype=F32) / ROT_DIM)
    ang = pos[:, None] * inv[None, :]
    cos, sin = jnp.cos(ang), jnp.sin(ang)
    n = pos.shape[0]
    one = jnp.ones((n, HEAD_DIM - ROT_DIM), F32)
    zero = jnp.zeros((n, HEAD_DIM - ROT_DIM), F32)
    zh = jnp.zeros((n, half), F32)
    c = jnp.concatenate([cos, cos, one], axis=1)
    s_up = jnp.concatenate([-sin, zh, zero], axis=1)
    s_dn = jnp.concatenate([zh, sin, zero], axis=1)
    return tuple(jnp.concatenate([t, t], axis=1) for t in (c, s_up, s_dn))


def _rotary_kernel(pa_ref, c_ref, su_ref, sd_ref, q_ref, k_ref):
    c, su, sd = c_ref[...], su_ref[...], sd_ref[...]
    half = ROT_DIM // 2
    for j in range((Q_W + KV_W) // LANES):
        x = pa_ref[:, j * LANES:(j + 1) * LANES]
        y = x * c + pltpu.roll(x, LANES - half, 1) * su + pltpu.roll(x, half, 1) * sd
        if j < Q_W // LANES:
            q_ref[:, j * LANES:(j + 1) * LANES] = y
        else:
            k_ref[...] = y


def rotary_qk(pa, tabs):
    n = pa.shape[0]
    tm = _tile(n)
    row = lambda w: pl.BlockSpec((tm, w), lambda i: (i, 0))
    return pl.pallas_call(
        _rotary_kernel,
        grid=(n // tm,),
        in_specs=[row(A_COLS), row(LANES), row(LANES), row(LANES)],
        out_specs=[row(Q_W), row(KV_W)],
        out_shape=[jax.ShapeDtypeStruct((n, Q_W), F32), jax.ShapeDtypeStruct((n, KV_W), F32)],
        compiler_params=_params(("parallel",)),
    )(pa, *tabs)


def _sink_softmax_pv(parts, sink):
    m = sink
    for s, _ in parts:
        m = jnp.maximum(m, jnp.max(s, axis=-1, keepdims=True))
    ps = [jnp.exp(s - m) for s, _ in parts]
    den = jnp.exp(sink - m)
    for p in ps:
        den = den + jnp.sum(p, axis=-1, keepdims=True)
    inv = 1.0 / den
    out = None
    for p, (_, v) in zip(ps, parts):
        o = jnp.dot((p * inv).astype(BF16), v, preferred_element_type=F32)
        out = o if out is None else out + o
    return out


def _swa_prompt_kernel(sink_ref, q_ref, kc_ref, kp_ref, vc_ref, vp_ref, o_ref):
    n = pl.program_id(1)
    qi = lax.broadcasted_iota(jnp.int32, (WINDOW, WINDOW), 0)
    kj = lax.broadcasted_iota(jnp.int32, (WINDOW, WINDOW), 1)
    mask_c = kj <= qi
    mask_p = kj > qi + jnp.where(n > 0, 0, WINDOW)
    scale = HEAD_DIM ** -0.5
    lane = lax.broadcasted_iota(jnp.int32, (WINDOW, KV_W), 1)
    low = lane < HEAD_DIM

    def halves(x, g):
        other = pltpu.roll(x, HEAD_DIM, 1)
        in_low, in_high = (x, other) if g == 0 else (other, x)
        return jnp.where(low, in_low, 0.0).astype(BF16), jnp.where(low, 0.0, in_high).astype(BF16)

    for g in range(ATT_KV_HEADS):
        kc, kp = halves(kc_ref[...], g), halves(kp_ref[...], g)
        vc, vp = halves(vc_ref[...], g), halves(vp_ref[...], g)
        for pair in range(ATT_GROUP // 2):
            col = (g * (ATT_GROUP // 2) + pair) * KV_W
            q2 = q_ref[:, col:col + KV_W].astype(BF16)
            out = None
            for odd in range(2):
                sc = lax.dot_general(q2, kc[odd], NT_DIMS, preferred_element_type=F32) * scale
                sp = lax.dot_general(q2, kp[odd], NT_DIMS, preferred_element_type=F32) * scale
                sc = jnp.where(mask_c, sc, NEG)
                sp = jnp.where(mask_p, sp, NEG)
                o = _sink_softmax_pv([(sp, vp[odd]), (sc, vc[odd])], sink_ref[col // HEAD_DIM + odd])
                out = o if out is None else out + o
            o_ref[:, col:col + KV_W] = out


def swa_prompt(q, k, pa, sinks, b, t):
    nb = t // WINDOW
    cur = lambda w, c: pl.BlockSpec((WINDOW, w), lambda bi, ni: (bi * nb + ni, c))
    prev = lambda w, c: pl.BlockSpec((WINDOW, w), lambda bi, ni: (bi * nb + jnp.maximum(ni - 1, 0), c))
    vcol = (Q_W + KV_W) // KV_W
    return pl.pallas_call(
        _swa_prompt_kernel,
        grid=(b, nb),
        in_specs=[pl.BlockSpec(memory_space=pltpu.SMEM), cur(Q_W, 0), cur(KV_W, 0), prev(KV_W, 0),
                  cur(KV_W, vcol), prev(KV_W, vcol)],
        out_specs=cur(Q_W, 0),
        out_shape=jax.ShapeDtypeStruct((b * t, Q_W), F32),
        compiler_params=_params(("parallel", "parallel")),
    )(sinks, q, k, k, pa, pa)


def _swa_sample_kernel(sink_ref, q_ref, kb_ref, vb_ref, kn_ref, vn_ref, o_ref, *, bt, dt, wb):
    rows = ATT_GROUP * dt
    tb = lax.rem(lax.broadcasted_iota(jnp.int32, (rows, wb), 0), dt)
    jb = lax.broadcasted_iota(jnp.int32, (rows, wb), 1)
    mask_b = jb > tb + (wb - WINDOW)
    npad = kn_ref.shape[1]
    tn = lax.rem(lax.broadcasted_iota(jnp.int32, (rows, npad), 0), dt)
    jn = lax.broadcasted_iota(jnp.int32, (rows, npad), 1)
    mask_n = jn <= tn
    scale = HEAD_DIM ** -0.5
    for b in range(bt):
        for g in range(ATT_KV_HEADS):
            gs = slice(g * HEAD_DIM, (g + 1) * HEAD_DIM)
            q = q_ref[b, g].astype(BF16)
            sb = lax.dot_general(q, kb_ref[b, :, gs].astype(BF16), NT_DIMS, preferred_element_type=F32) * scale
            sn = lax.dot_general(q, kn_ref[b, :, gs].astype(BF16), NT_DIMS, preferred_element_type=F32) * scale
            sb = jnp.where(mask_b, sb, NEG)
            sn = jnp.where(mask_n, sn, NEG)
            o_ref[b, g] = _sink_softmax_pv(
                [(sb, vb_ref[b, :, gs].astype(BF16)), (sn, vn_ref[b, :, gs].astype(BF16))], sink_ref[g][:, :1])


def swa_sample(qs, kbuf, vbuf, knew, vnew, sink_rows):
    db, _, rows, _ = qs.shape
    dt = rows // ATT_GROUP
    wb = kbuf.shape[1]
    bt = _tile(db, (8, 4, 2, 1))
    blk = lambda a: pl.BlockSpec((bt,) + a.shape[1:], lambda i: (i,) + (0,) * (a.ndim - 1))
    return pl.pallas_call(
        functools.partial(_swa_sample_kernel, bt=bt, dt=dt, wb=wb),
        grid=(db // bt,),
        in_specs=[_full(sink_rows.shape), blk(qs), blk(kbuf), blk(vbuf), blk(knew), blk(vnew)],
        out_specs=blk(qs),
        out_shape=jax.ShapeDtypeStruct(qs.shape, F32),
        compiler_params=_params(("parallel",)),
    )(sink_rows, qs, kbuf, vbuf, knew, vnew)


CONV_HALO = 32


def _ln_swish(y, lg, lb):
    mu = jnp.mean(y, axis=-1, keepdims=True)
    var = jnp.mean(jnp.square(y - mu), axis=-1, keepdims=True)
    yn = (y - mu) * lax.rsqrt(var + LN_EPS) * lg + lb
    return yn * _sigmoid(yn)


def _conv_prompt_kernel(pb_ref, dw_ref, db_ref, lg_ref, lb_ref, y_ref, st_ref, ext_ref, *, tt):
    i = pl.program_id(1)

    @pl.when(i == 0)
    def _():
        ext_ref[0:CONV_HALO, :] = jnp.zeros((CONV_HALO, CONV_CH), F32)

    @pl.when(i > 0)
    def _():
        ext_ref[0:CONV_HALO, :] = ext_ref[tt:tt + CONV_HALO, :]

    pb = pb_ref[...]
    ext_ref[CONV_HALO:CONV_HALO + tt, :] = pb[:, :CONV_CH] * _sigmoid(pb[:, CONV_CH:])
    acc = jnp.zeros((tt, CONV_CH), F32) + db_ref[...]
    first = CONV_HALO - (CONV_WIDTH - 1)
    for j in range(CONV_WIDTH):
        acc = acc + ext_ref[pl.ds(first + j, tt), :] * dw_ref[j:j + 1, :]
    y_ref[...] = _ln_swish(acc, lg_ref[...], lb_ref[...])

    @pl.when(i == pl.num_programs(1) - 1)
    def _():
        st_ref[0] = ext_ref[tt:tt + CONV_HALO, :]


def conv_prompt(pb, dw, db, lg, lb, b, t):
    tt = _tile(t)
    nt = t // tt
    vec = lambda a: a.reshape(1, CONV_CH)
    return pl.pallas_call(
        functools.partial(_conv_prompt_kernel, tt=tt),
        grid=(b, nt),
        in_specs=[pl.BlockSpec((tt, B_COLS), lambda bi, i: (bi * nt + i, 0)), _full((CONV_WIDTH, CONV_CH)),
                  _full((1, CONV_CH)), _full((1, CONV_CH)), _full((1, CONV_CH))],
        out_specs=[pl.BlockSpec((tt, CONV_CH), lambda bi, i: (bi * nt + i, 0)),
                   pl.BlockSpec((1, CONV_HALO, CONV_CH), lambda bi, i: (bi, 0, 0))],
        out_shape=[jax.ShapeDtypeStruct((b * t, CONV_CH), F32), jax.ShapeDtypeStruct((b, CONV_HALO, CONV_CH), F32)],
        scratch_shapes=[pltpu.VMEM((tt + CONV_HALO, CONV_CH), F32)],
        compiler_params=_params(("arbitrary", "arbitrary")),
    )(pb, dw, vec(db), vec(lg), vec(lb))


def _conv_sample_kernel(c_ref, pb_ref, dw_ref, db_ref, lg_ref, lb_ref, y_ref, nc_ref, *, dt):
    nprev = CONV_WIDTH - 1
    us = []
    for t in range(dt):
        pb = pb_ref[t]
        us.append(pb[:, :CONV_CH] * _sigmoid(pb[:, CONV_CH:]))

    def ext(j):
        return c_ref[j] if j < nprev else us[j - nprev]

    for t in range(dt):
        acc = db_ref[...] + ext(t) * dw_ref[0:1, :]
        for j in range(1, CONV_WIDTH):
            acc = acc + ext(t + j) * dw_ref[j:j + 1, :]
        y_ref[t] = _ln_swish(acc, lg_ref[...], lb_ref[...])
    for j in range(nprev):
        nc_ref[j] = ext(j + dt)


def conv_sample(cache_t, pb_t, dw, db, lg, lb):
    nprev, dbt, _ = cache_t.shape
    dt = pb_t.shape[0]
    bt = _tile(dbt, (32, 16, 8))
    vec = lambda a: a.reshape(1, CONV_CH)
    blk = lambda lead, w: pl.BlockSpec((lead, bt, w), lambda i: (0, i, 0))
    return pl.pallas_call(
        functools.partial(_conv_sample_kernel, dt=dt),
        grid=(dbt // bt,),
        in_specs=[blk(nprev, CONV_CH), blk(dt, B_COLS), _full((CONV_WIDTH, CONV_CH)), _full((1, CONV_CH)),
                  _full((1, CONV_CH)), _full((1, CONV_CH))],
        out_specs=[blk(dt, CONV_CH), blk(nprev, CONV_CH)],
        out_shape=[jax.ShapeDtypeStruct((dt, dbt, CONV_CH), F32), jax.ShapeDtypeStruct((nprev, dbt, CONV_CH), F32)],
        compiler_params=_params(("parallel",)),
    )(cache_t, pb_t, dw, vec(db), vec(lg), vec(lb))


def _head_sum_matrix():
    h = np.arange(RWKV_W) // RWKV_HEAD
    return jnp.asarray((h[:, None] == h[None, :]).astype(np.float32))


def _rwkv_pre_math(pc, prev, c):
    mu_ref, w0_ref, w2_ref, a0_ref, a2_ref, g2_ref, kkp_ref, ka_ref, rk_ref, hs_ref = c
    xs = pc + (prev - pc) * mu_ref[...]
    o1, o2, o3 = RWKV_W, 2 * RWKV_W, 3 * RWKV_W
    o4 = o3 + DECAY_LORA
    o5 = o4 + ICLR_LORA
    r, k, v = xs[:, :o1], xs[:, o1:o2], xs[:, o2:o3]
    wd, ad, gd = xs[:, o3:o4], xs[:, o4:o5], xs[:, o5:]
    y = -(w0_ref[...] + jnp.dot(jnp.tanh(wd).astype(BF16), w2_ref[...], preferred_element_type=F32))
    softplus = jnp.maximum(y, 0.0) + jnp.log(1.0 + jnp.exp(-jnp.abs(y)))
    decay = jnp.exp(-jnp.exp(-softplus - 0.5))
    a = _sigmoid(a0_ref[...] + jnp.dot(ad.astype(BF16), a2_ref[...], preferred_element_type=F32))
    g = jnp.dot(_sigmoid(gd).astype(BF16), g2_ref[...], preferred_element_type=F32)
    hs = hs_ref[...]
    kk = k * kkp_ref[...]
    norm = jnp.sqrt(jnp.dot(kk * kk, hs, precision=HIGHEST, preferred_element_type=F32))
    kk = kk / jnp.maximum(norm, 1e-12)
    k_mod = k * (1.0 + (a - 1.0) * ka_ref[...])
    bonus = jnp.dot(r * k_mod * rk_ref[...], hs, precision=HIGHEST, preferred_element_type=F32) * v
    return r, decay, k_mod, -kk, kk * a, v, g, bonus


RWKV_PRE_OUTS = 8
SHIFT_HALO = SUBLANES


def _rwkv_pre_prompt_kernel(pc_ref, *refs, tt):
    consts, outs, ext_ref = refs[:10], refs[10:10 + RWKV_PRE_OUTS], refs[-1]
    i = pl.program_id(1)

    @pl.when(i == 0)
    def _():
        ext_ref[0:SHIFT_HALO, :] = jnp.zeros((SHIFT_HALO, C_COLS), F32)

    @pl.when(i > 0)
    def _():
        ext_ref[0:SHIFT_HALO, :] = ext_ref[tt:tt + SHIFT_HALO, :]

    pc = pc_ref[...]
    ext_ref[SHIFT_HALO:SHIFT_HALO + tt, :] = pc
    res = _rwkv_pre_math(pc, ext_ref[pl.ds(SHIFT_HALO - 1, tt), :], consts)
    for o_ref, val in zip(outs, res):
        o_ref[...] = val


def _rwkv_pre_sample_kernel(pc_ref, shift_ref, *refs, dt):
    consts, outs = refs[:10], refs[10:10 + RWKV_PRE_OUTS]
    for t in range(dt):
        res = _rwkv_pre_math(pc_ref[t], shift_ref[...] if t == 0 else pc_ref[t - 1], consts)
        for o_ref, val in zip(outs, res):
            o_ref[t] = val


def _rwkv_pre_consts(lp):
    vec = lambda a: a.reshape(1, -1)
    return [vec(lp['rk_mu']), vec(lp['rk_w0']), lp['rk_w2'].astype(BF16), vec(lp['rk_a0']), lp['rk_a2'].astype(BF16),
            lp['rk_g2'].astype(BF16), vec(lp['rk_kk']), vec(lp['rk_ka']), vec(lp['rk_rk']), _head_sum_matrix()]


def rwkv_pre_prompt(pc, lp, b, t):
    tt = _tile(t, (256, 128))
    nt = t // tt
    consts = _rwkv_pre_consts(lp)
    row = lambda w: pl.BlockSpec((tt, w), lambda bi, i: (bi * nt + i, 0))
    return pl.pallas_call(
        functools.partial(_rwkv_pre_prompt_kernel, tt=tt),
        grid=(b, nt),
        in_specs=[row(C_COLS)] + [_full(c.shape) for c in consts],
        out_specs=[row(RWKV_W)] * RWKV_PRE_OUTS,
        out_shape=[jax.ShapeDtypeStruct((b * t, RWKV_W), F32)] * RWKV_PRE_OUTS,
        scratch_shapes=[pltpu.VMEM((tt + SHIFT_HALO, C_COLS), F32)],
        compiler_params=_params(("arbitrary", "arbitrary")),
    )(pc, *consts)


def rwkv_pre_sample(pc_t, shift, lp):
    dt, dbt, _ = pc_t.shape
    bt = _tile(dbt, (64, 32, 16, 8))
    consts = _rwkv_pre_consts(lp)
    blk = lambda w: pl.BlockSpec((dt, bt, w), lambda i: (0, i, 0))
    return pl.pallas_call(
        functools.partial(_rwkv_pre_sample_kernel, dt=dt),
        grid=(dbt // bt,),
        in_specs=[blk(C_COLS), pl.BlockSpec((bt, C_COLS), lambda i: (i, 0))] + [_full(c.shape) for c in consts],
        out_specs=[blk(RWKV_W)] * RWKV_PRE_OUTS,
        out_shape=[jax.ShapeDtypeStruct((dt, dbt, RWKV_W), F32)] * RWKV_PRE_OUTS,
        compiler_params=_params(("parallel",)),
    )(pc_t, shift, *consts)


def _scan_select_matrices():
    e = np.zeros((SCAN_TSUB, LANES, LANES), np.float32)
    vl = LANES // SCAN_PAIRS
    for tl in range(SCAN_TSUB):
        for g in range(SCAN_PAIRS):
            e[tl, tl * SCAN_PAIRS + g, g * vl:(g + 1) * vl] = 1.0
    return jnp.asarray(np.concatenate([e, e, e], axis=1), dtype=BF16)


def _sublane_allsum(x):
    x = x + pltpu.roll(x, 4, 0)
    x = x + pltpu.roll(x, 2, 0)
    return x + pltpu.roll(x, 1, 0)


def _scan_kernel(xr_ref, xw_ref, xk_ref, xa_ref, xb_ref, v_ref, s0_ref, e_ref, o_ref, sf_ref, z_ref, tile_ref, xs_ref,
                 *, groups, steps):
    x_refs = (xr_ref, xw_ref, xk_ref, xa_ref, xb_ref)
    c = pl.program_id(1)
    nvh = z_ref.shape[0]
    nkb = RWKV_HEAD // SUBLANES
    sub = lax.broadcasted_iota(jnp.int32, (SUBLANES, LANES), 0)
    R_ROW, W_ROW, K_ROW, A_ROW, B_ROW = (j * RWKV_HEAD for j in range(5))

    @pl.when(c == 0)
    def _():
        z_ref[...] = s0_ref[0]

    def split_group(th):
        for j, x_ref in enumerate(x_refs):
            rows = slice(j * RWKV_HEAD, (j + 1) * RWKV_HEAD)
            xg = x_ref[0, th]
            hi = xg.astype(BF16)
            r1 = xg - hi.astype(F32)
            mid = r1.astype(BF16)
            lo = (r1 - mid.astype(F32)).astype(BF16)
            xs_ref[rows, 0:LANES] = hi
            xs_ref[rows, LANES:2 * LANES] = mid
            xs_ref[rows, 2 * LANES:3 * LANES] = lo

    def spread(tl, buf):
        tile_ref[buf] = jnp.dot(xs_ref[...], e_ref[tl], preferred_element_type=F32)

    split_group(0)
    spread(0, 0)

    def body(th, carry):
        for tl in range(steps):
            cur = tl % 2
            if tl + 1 < steps:
                spread(tl + 1, 1 - cur)
            else:
                split_group(jnp.minimum(th + 1, groups - 1))
                spread(0, 1 - cur)
            t = th * steps + tl
            vrow = v_ref[0, t]
            blk = lambda row0, kb: tile_ref[cur, row0 + kb * SUBLANES:row0 + (kb + 1) * SUBLANES, :]
            sa = []
            for vh in range(nvh):
                acc = None
                for kb in range(nkb):
                    p = z_ref[vh, kb * SUBLANES:(kb + 1) * SUBLANES, :] * blk(A_ROW, kb)
                    acc = p if kb == 0 else acc + p
                sa.append(_sublane_allsum(acc))
            orow = jnp.zeros((SUBLANES, LANES), F32)
            for vh in range(nvh):
                vb = jnp.broadcast_to(vrow[vh:vh + 1], (SUBLANES, LANES))
                acc = None
                for kb in range(nkb):
                    ks = slice(kb * SUBLANES, (kb + 1) * SUBLANES)
                    zn = z_ref[vh, ks, :] * blk(W_ROW, kb) + blk(B_ROW, kb) * sa[vh] + blk(K_ROW, kb) * vb
                    z_ref[vh, ks, :] = zn
                    p = zn * blk(R_ROW, kb)
                    acc = p if kb == 0 else acc + p
                orow = jnp.where(sub == vh, _sublane_allsum(acc), orow)
            o_ref[0, t] = orow
        return carry

    lax.fori_loop(0, groups, body, 0)

    @pl.when(c == pl.num_programs(1) - 1)
    def _():
        sf_ref[0] = z_ref[...]


def rwkv_scan(x5, v, s0, nseq, t):
    spb = SCAN_PAIRS // RWKV_HEADS
    assert nseq % spb == 0
    nblk = nseq // spb
    vl = LANES // SCAN_PAIRS
    vh = RWKV_HEAD // vl
    steps = min(SCAN_TSUB, t)
    assert t % steps == 0 and steps % 2 == 0
    t8 = t // steps
    def key_layout(a):
        a = a.reshape(nblk, spb, t8, steps, RWKV_HEADS, RWKV_HEAD)
        if steps < SCAN_TSUB:
            a = jnp.pad(a, ((0, 0),) * 3 + ((0, SCAN_TSUB - steps),) + ((0, 0),) * 2)
        return a.transpose(0, 2, 5, 3, 1, 4).reshape(nblk, t8, RWKV_HEAD, LANES)

    xk = [key_layout(a) for a in x5]
    vk = v.reshape(nblk, spb, t, RWKV_HEADS, vh, vl).transpose(0, 2, 4, 1, 3, 5).reshape(nblk, t, vh, LANES)
    sk = s0.reshape(nblk, spb, RWKV_HEADS, vh, vl, RWKV_HEAD).transpose(0, 3, 5, 1, 2, 4).reshape(
        nblk, vh, RWKV_HEAD, LANES)
    groups = _tile(t8, (32, 16, 8, 4, 2, 1))
    nchunk = t8 // groups
    ch = groups * steps
    o, sf = pl.pallas_call(
        functools.partial(_scan_kernel, groups=groups, steps=steps),
        grid=(nblk, nchunk),
        in_specs=[pl.BlockSpec((1, groups, RWKV_HEAD, LANES), lambda i, c: (i, c, 0, 0))] * 5 + [
                  pl.BlockSpec((1, ch, vh, LANES), lambda i, c: (i, c, 0, 0)),
                  pl.BlockSpec((1, vh, RWKV_HEAD, LANES), lambda i, c: (i, 0, 0, 0)),
                  _full((SCAN_TSUB, 3 * LANES, LANES))],
        out_specs=[pl.BlockSpec((1, ch, vh, LANES), lambda i, c: (i, c, 0, 0)),
                   pl.BlockSpec((1, vh, RWKV_HEAD, LANES), lambda i, c: (i, 0, 0, 0))],
        out_shape=[jax.ShapeDtypeStruct((nblk, t, vh, LANES), F32),
                   jax.ShapeDtypeStruct((nblk, vh, RWKV_HEAD, LANES), F32)],
        scratch_shapes=[pltpu.VMEM((vh, RWKV_HEAD, LANES), F32), pltpu.VMEM((2, 5 * RWKV_HEAD, LANES), F32),
                        pltpu.VMEM((5 * RWKV_HEAD, 3 * LANES), BF16)],
        compiler_params=_params(("arbitrary", "arbitrary")),
    )(*xk, vk, sk, _scan_select_matrices())
    o = o.reshape(nblk, t, vh, spb, RWKV_HEADS, vl).transpose(0, 3, 1, 4, 2, 5).reshape(nseq * t, RWKV_W)
    sf = sf.reshape(nblk, vh, RWKV_HEAD, spb, RWKV_HEADS, vl).transpose(0, 3, 4, 1, 5, 2).reshape(
        nseq, RWKV_HEADS, RWKV_HEAD, RWKV_HEAD)
    return o, sf


def _merge_kernel(x_ref, oa_ref, yb_ref, oc_ref, bonus_ref, g_ref, pg_ref, wa_ref, wb_ref, bb_ref, wc_ref, wo_ref,
                  gng_ref, gnb_ref, hs_ref, o_ref):
    hs = hs_ref[...] * (1.0 / RWKV_HEAD)
    oc = oc_ref[...]
    mu = jnp.dot(oc, hs, precision=HIGHEST, preferred_element_type=F32)
    dev = oc - mu
    var = jnp.dot(dev * dev, hs, precision=HIGHEST, preferred_element_type=F32)
    on = dev * lax.rsqrt(var + GN_EPS) * gng_ref[...] + gnb_ref[...]
    yc = (on + bonus_ref[...]) * g_ref[...]
    d = D_MODEL
    dot = lambda a, w: jnp.dot(a.astype(BF16), w[...], preferred_element_type=F32)
    merged = (_sigmoid(pg_ref[:, 0:d]) * dot(oa_ref[...], wa_ref)
              + _sigmoid(pg_ref[:, d:2 * d]) * (dot(yb_ref[...], wb_ref) + bb_ref[...])
              + _sigmoid(pg_ref[:, 2 * d:3 * d]) * dot(yc, wc_ref))
    o_ref[...] = x_ref[...] + dot(merged, wo_ref)


def merge_out(x, oa, yb, oc, bonus, g, pg, lp):
    n = x.shape[0]
    tm = _tile(n, (256, 128, 64, 32, 16, 8))
    row = lambda w: pl.BlockSpec((tm, w), lambda i: (i, 0))
    vec = lambda a: a.reshape(1, -1)
    consts = [lp['w_a_out'].astype(BF16), lp['w_b_out'].astype(BF16), vec(lp['b_b_out']), lp['w_c_out'].astype(BF16),
              lp['w_o'].astype(BF16), vec(lp['rk_gn_g']), vec(lp['rk_gn_b']), _head_sum_matrix()]
    return pl.pallas_call(
        _merge_kernel,
        grid=(n // tm,),
        in_specs=[row(D_MODEL), row(Q_W), row(CONV_CH), row(RWKV_W), row(RWKV_W), row(RWKV_W), row(G_COLS)]
        + [_full(c.shape) for c in consts],
        out_specs=row(D_MODEL),
        out_shape=jax.ShapeDtypeStruct((n, D_MODEL), F32),
        compiler_params=_params(("parallel",)),
    )(x, oa, yb, oc, bonus, g, pg, *consts)


def _cross_attn_kernel(q_ref, k_ref, v_ref, o_ref):
    scale = CA_HEAD_DIM ** -0.5
    outs = []
    for h in range(CA_HEADS):
        hs = slice(h * CA_HEAD_DIM, (h + 1) * CA_HEAD_DIM)
        q = q_ref[0, :, hs].astype(BF16)
        s = lax.dot_general(q, k_ref[0, :, hs].astype(BF16), NT_DIMS, preferred_element_type=F32) * scale
        m = jnp.max(s, axis=-1, keepdims=True)
        p = jnp.exp(s - m)
        p = p / jnp.sum(p, axis=-1, keepdims=True)
        outs.append(jnp.dot(p.astype(BF16), v_ref[0, :, hs].astype(BF16), preferred_element_type=F32))
    o_ref[0] = jnp.concatenate(outs, axis=1)


def cross_attn(q, mk, mv, tiles_per_seq, first_seq=0):
    nt, tq, d = q.shape
    m = mk.shape[1]
    kv = pl.BlockSpec((1, m, d), lambda i: (first_seq + i // tiles_per_seq, 0, 0))
    qs = pl.BlockSpec((1, tq, d), lambda i: (i, 0, 0))
    return pl.pallas_call(
        _cross_attn_kernel,
        grid=(nt,),
        in_specs=[qs, kv, kv],
        out_specs=qs,
        out_shape=jax.ShapeDtypeStruct(q.shape, F32),
        compiler_params=_params(("parallel",)),
    )(q, mk, mv)


def _staircase():
    return [(i, PEER_TOPK // (i + 1)) for i in range(PEER_TOPK)]


PEER_CAND = sum(nj for _, nj in _staircase())
PEER_CAND_PAD = -(-PEER_CAND // SUBLANES) * SUBLANES


def _extract_topk(cur, out_ref, base, with_rank=False):
    rank = jnp.full(cur.shape, float(PEER_TOPK), F32) if with_rank else None
    for k in range(PEER_TOPK):
        m = jnp.max(cur, axis=0, keepdims=True)
        out_ref[base + k:base + k + 1, :] = m
        hit = cur == m
        if with_rank:
            rank = jnp.where(hit, float(k), rank)
        if k + 1 < PEER_TOPK:
            cur = jnp.where(hit, -jnp.inf, cur)
    return rank


def _peer_select_kernel(qp_ref, keys_ref, r1_ref, rho_ref, m2_ref, e2_ref, s_ref, tv_ref, cand_ref, best_ref):
    nsub = 2 * PEER_HEADS
    for hc in range(nsub):
        q = qp_ref[:, hc * PEER_HALF:(hc + 1) * PEER_HALF].astype(BF16)
        st = lax.dot_general(keys_ref[hc], q, NT_DIMS, preferred_element_type=F32)
        s_ref[hc] = st
        rank = _extract_topk(st, tv_ref, hc * PEER_TOPK, with_rank=hc % 2 == 0)
        if hc % 2 == 0:
            r1_ref[hc // 2] = rank
    tm = qp_ref.shape[0]
    for h in range(PEER_HEADS):
        b1 = 2 * h * PEER_TOPK
        b2 = b1 + PEER_TOPK
        cand_ref[PEER_CAND_PAD - SUBLANES:PEER_CAND_PAD, :] = jnp.full((SUBLANES, tm), -jnp.inf, F32)
        off = 0
        for i, nj in _staircase():
            cand_ref[off:off + nj, :] = tv_ref[b1 + i:b1 + i + 1, :] + tv_ref[b2:b2 + nj, :]
            off += nj
        _extract_topk(cand_ref[...], best_ref, 0)
        best = best_ref[...]
        z = jnp.sum(jnp.exp(best - best[0:1, :]), axis=0, keepdims=True)
        thr = best[PEER_TOPK - 1:PEER_TOPK, :]
        s2 = s_ref[2 * h + 1]
        m2 = jnp.zeros(s2.shape, F32)
        for i in range(PEER_TOPK):
            m2 = jnp.where((tv_ref[b1 + i:b1 + i + 1, :] + s2) >= thr, float(i + 1), m2)
        m2_ref[h] = m2
        e2_ref[h] = jnp.exp(s2 - tv_ref[b2:b2 + 1, :])
        rho_ref[h] = jnp.exp(s_ref[2 * h] - tv_ref[b1:b1 + 1, :]) * (1.0 / z)


def peer_select(qp, keys):
    n = qp.shape[0]
    tm = _tile(n, (256, 128))
    nsub = 2 * PEER_HEADS
    blk = pl.BlockSpec((PEER_HEADS, N_KEYS, tm), lambda i: (0, 0, i))
    shp = lambda dt: jax.ShapeDtypeStruct((PEER_HEADS, N_KEYS, n), dt)
    return pl.pallas_call(
        _peer_select_kernel,
        grid=(n // tm,),
        in_specs=[pl.BlockSpec((tm, nsub * PEER_HALF), lambda i: (i, 0)), _full(keys.shape)],
        out_specs=[blk, blk, blk, blk],
        out_shape=[shp(F32)] * 4,
        scratch_shapes=[pltpu.VMEM((nsub, N_KEYS, tm), F32), pltpu.VMEM((nsub * PEER_TOPK, tm), F32),
                        pltpu.VMEM((PEER_CAND_PAD, tm), F32), pltpu.VMEM((PEER_TOPK, tm), F32)],
        compiler_params=_params(("parallel",)),
    )(qp, keys)


PEER_PACK_ROWS = 16
PEER_EXPERT_TILE = 4 * N_KEYS


def _gelu_tanh(x):
    return 0.5 * x * (1.0 + jnp.tanh(0.7978845608028654 * (x + 0.044715 * (x * x * x))))


def _peer_dense_kernel(x_ref, g_ref, r1_ref, rho_ref, m2_ref, e2_ref, u_ref, vt_ref, o_ref, xnt_ref, acc_ref, ht_ref,
                       wh_ref, m2b_ref, e2b_ref, rows_ref, *, te):
    e = pl.program_id(1)
    tm = x_ref.shape[0]
    pk = PEER_PACK_ROWS

    @pl.when(e == 0)
    def _():
        x = x_ref[...]
        xn = x * lax.rsqrt(jnp.mean(x * x, axis=-1, keepdims=True) + RMS_EPS) * g_ref[...]
        xnt_ref[...] = xn.T.astype(BF16)
        acc_ref[...] = jnp.zeros_like(acc_ref)
        m2b_ref[...] = m2_ref[...].astype(BF16)
        e2b_ref[...] = e2_ref[...].astype(BF16)

    n_a = te // N_KEYS
    for al in range(n_a):
        a = e * n_a + al
        for h in range(PEER_HEADS):
            row = 2 * (al * PEER_HEADS + h)
            rows_ref[row:row + 1, :] = r1_ref[h, pl.ds(a, 1), :]
            rows_ref[row + 1:row + 2, :] = rho_ref[h, pl.ds(a, 1), :]

    ht_ref[...] = jnp.dot(u_ref[...], xnt_ref[...], preferred_element_type=F32)
    zero = jnp.zeros((), BF16)
    for al in range(n_a):
        for cl in range(tm // LANES):
            cs = slice(cl * LANES, (cl + 1) * LANES)
            packed = lambda row: jnp.broadcast_to(rows_ref[row:row + 1, cs], (pk, LANES)).astype(BF16)
            r1_rows = [packed(2 * (al * PEER_HEADS + h)) for h in range(PEER_HEADS)]
            rho_rows = [packed(2 * (al * PEER_HEADS + h) + 1) for h in range(PEER_HEADS)]
            for rb in range(N_KEYS // pk):
                rs = slice(rb * pk, (rb + 1) * pk)
                w = None
                for h in range(PEER_HEADS):
                    sel = r1_rows[h] < m2b_ref[h, rs, cs]
                    contrib = jnp.where(sel, e2b_ref[h, rs, cs] * rho_rows[h], zero)
                    w = contrib if w is None else w + contrib
                hr = slice(al * N_KEYS + rb * pk, al * N_KEYS + (rb + 1) * pk)
                wh_ref[hr, cs] = w * _gelu_tanh(ht_ref[hr, cs]).astype(BF16)
    acc_ref[...] += jnp.dot(vt_ref[...], wh_ref[...], preferred_element_type=F32)

    @pl.when(e == pl.num_programs(1) - 1)
    def _():
        o_ref[...] = x_ref[...] + acc_ref[...].T


def peer_dense(x, g, sel, u, v_t, layer):
    n, d = x.shape
    ne = u.shape[1]
    tm = _tile(n, (512, 256, 128))
    te = PEER_EXPERT_TILE
    blk = pl.BlockSpec((PEER_HEADS, N_KEYS, tm), lambda i, e: (0, 0, i))
    return pl.pallas_call(
        functools.partial(_peer_dense_kernel, te=te),
        grid=(n // tm, ne // te),
        in_specs=[pl.BlockSpec((tm, d), lambda i, e: (i, 0)), _full((1, d)), blk, blk, blk, blk,
                  pl.BlockSpec((None, te, d), lambda i, e: (layer, e, 0)),
                  pl.BlockSpec((None, d, te), lambda i, e: (layer, 0, e))],
        out_specs=pl.BlockSpec((tm, d), lambda i, e: (i, 0)),
        out_shape=jax.ShapeDtypeStruct((n, d), F32),
        scratch_shapes=[pltpu.VMEM((d, tm), BF16), pltpu.VMEM((d, tm), F32), pltpu.VMEM((te, tm), F32),
                        pltpu.VMEM((te, tm), BF16), pltpu.VMEM((PEER_HEADS, N_KEYS, tm), BF16),
                        pltpu.VMEM((PEER_HEADS, N_KEYS, tm), BF16),
                        pltpu.VMEM((2 * PEER_HEADS * (te // N_KEYS), tm), F32)],
        compiler_params=_params(("parallel", "arbitrary")),
    )(x, g.reshape(1, d), *sel, u, v_t)


def kernel(x_prompt, x_sample, cache_win_k, cache_win_v, cache_conv, state_shift, state_wkv, cache_mem_k, cache_mem_v,
           mem_prompt, g_mix, w_in, att_sinks, w_a_out, conv_dw, conv_db, conv_ln_g, conv_ln_b, w_b_out, b_b_out,
           rk_mu, rk_w0, rk_w2, rk_a0, rk_a2, rk_g2, rk_kk, rk_ka, rk_rk, rk_gn_g, rk_gn_b, w_c_out, w_o,
           g_ca, g_mem, w_cq, w_mk, w_mv, w_co, g_ffn, w_pq, peer_keys, peer_u, peer_v, g_final):
    b, t, d = x_prompt.shape
    db, dt, _ = x_sample.shape
    depth = w_in.shape[0]
    wb = cache_win_k.shape[2]
    mem_len = mem_prompt.shape[1]
    n_p, n_s = b * t, db * dt
    assert t % WINDOW == 0 and wb == WINDOW and dt <= SUBLANES

    xp = x_prompt.reshape(n_p, d)
    xs = x_sample.reshape(n_s, d)
    tabs_p = _rotary_tables(jnp.tile(jnp.arange(t, dtype=F32), b))
    tabs_s = _rotary_tables(jnp.tile(PAST_LEN + jnp.arange(dt, dtype=F32), db))
    tq = _tile(t)
    new_pad = SUBLANES - dt
    u_bf = peer_u.astype(BF16)
    vt_bf = jnp.swapaxes(peer_v.astype(BF16), 1, 2)
    mem = mem_prompt.reshape(b * mem_len, d)
    mem_k_all = cache_mem_k.reshape(depth * db, mem_len, d)
    mem_v_all = cache_mem_v.reshape(depth * db, mem_len, d)
    time_major = lambda a: a.reshape(db, dt, a.shape[-1]).transpose(1, 0, 2)
    seq_major = lambda a: a.transpose(1, 0, 2).reshape(n_s, a.shape[-1])
    k4 = lambda a, nb_, tt_: a.reshape(nb_, tt_, ATT_KV_HEADS, HEAD_DIM)

    outs = [[] for _ in range(12)]
    for l in range(depth):
        lp = {'rk_mu': rk_mu[l], 'rk_w0': rk_w0[l], 'rk_w2': rk_w2[l], 'rk_a0': rk_a0[l], 'rk_a2': rk_a2[l],
              'rk_g2': rk_g2[l], 'rk_kk': rk_kk[l], 'rk_ka': rk_ka[l], 'rk_rk': rk_rk[l].reshape(-1),
              'rk_gn_g': rk_gn_g[l], 'rk_gn_b': rk_gn_b[l], 'w_a_out': w_a_out[l], 'w_b_out': w_b_out[l],
              'b_b_out': b_b_out[l], 'w_c_out': w_c_out[l], 'w_o': w_o[l]}
        win = w_in[l].astype(BF16)
        c0, c1, c2 = A_COLS, A_COLS + B_COLS, A_COLS + B_COLS + C_COLS
        w_seg = [win[:, :c0], win[:, c0:c1], win[:, c1:c2], win[:, c2:]]
        pa_p, pb_p, pc_p, pg_p = (norm_matmul(xp, g_mix[l], w) for w in w_seg)
        pa_s, pb_s, pc_s, pg_s = (norm_matmul(xs, g_mix[l], w) for w in w_seg)

        q_p, k_p = rotary_qk(pa_p, tabs_p)
        q_s, k_s = rotary_qk(pa_s, tabs_s)
        oa_p = swa_prompt(q_p, k_p, pa_p, att_sinks[l], b, t)
        qs4 = q_s.reshape(db, dt, ATT_KV_HEADS, ATT_GROUP, HEAD_DIM).transpose(0, 2, 3, 1, 4).reshape(
            db, ATT_KV_HEADS, ATT_GROUP * dt, HEAD_DIM)
        k_new = k_s.reshape(db, dt, KV_W)
        v_new = pa_s[:, Q_W + KV_W:].reshape(db, dt, KV_W)
        padn = lambda a: jnp.pad(a, ((0, 0), (0, new_pad), (0, 0)))
        sink_rows = jnp.broadcast_to(
            jnp.repeat(att_sinks[l].reshape(ATT_KV_HEADS, ATT_GROUP), dt, axis=1)[:, :, None],
            (ATT_KV_HEADS, ATT_GROUP * dt, LANES))
        oa_s = swa_sample(qs4, cache_win_k[l].reshape(db, wb, KV_W), cache_win_v[l].reshape(db, wb, KV_W),
                          padn(k_new), padn(v_new), sink_rows)
        oa_s = oa_s.reshape(db, ATT_KV_HEADS, ATT_GROUP, dt, HEAD_DIM).transpose(0, 3, 1, 2, 4).reshape(n_s, Q_W)

        yb_p, conv_st = conv_prompt(pb_p, conv_dw[l], conv_db[l], conv_ln_g[l], conv_ln_b[l], b, t)
        yb_s, conv_new = conv_sample(cache_conv[l].transpose(1, 0, 2), time_major(pb_s),
                                     conv_dw[l], conv_db[l], conv_ln_g[l], conv_ln_b[l])
        yb_s = seq_major(yb_s)

        *x5_p, v_p, g_p, bonus_p = rwkv_pre_prompt(pc_p, lp, b, t)
        pre_s = [seq_major(a) for a in rwkv_pre_sample(time_major(pc_s), state_shift[l], lp)]
        *x5_s, v_s, g_s, bonus_s = pre_s
        oc_p, wkv_p = rwkv_scan(x5_p, v_p, jnp.zeros((b, RWKV_HEADS, RWKV_HEAD, RWKV_HEAD), F32), b, t)
        oc_s, wkv_s = rwkv_scan(x5_s, v_s, state_wkv[l], db, dt)

        xp = merge_out(xp, oa_p, yb_p, oc_p, bonus_p, g_p, pg_p, lp)
        xs = merge_out(xs, oa_s, yb_s, oc_s, bonus_s, g_s, pg_s, lp)

        mk = norm_matmul(mem, g_mem[l], w_mk[l].astype(BF16))
        mv = norm_matmul(mem, g_mem[l], w_mv[l].astype(BF16))
        wcq, wco = w_cq[l].astype(BF16), w_co[l].astype(BF16)
        ca_p = cross_attn(norm_matmul(xp, g_ca[l], wcq).reshape(n_p // tq, tq, d), mk.reshape(b, mem_len, d),
                          mv.reshape(b, mem_len, d), t // tq)
        ca_s = cross_attn(norm_matmul(xs, g_ca[l], wcq).reshape(db, dt, d), mem_k_all, mem_v_all, 1, first_seq=l * db)
        xp = matmul_residual(ca_p.reshape(n_p, d), wco, xp)
        xs = matmul_residual(ca_s.reshape(n_s, d), wco, xs)

        wpq = w_pq[l].astype(BF16)
        keys = peer_keys[l].reshape(2 * PEER_HEADS, N_KEYS, PEER_HALF).astype(BF16)
        xp = peer_dense(xp, g_ffn[l], peer_select(norm_matmul(xp, g_ffn[l], wpq), keys), u_bf, vt_bf, l)
        xs = peer_dense(xs, g_ffn[l], peer_select(norm_matmul(xs, g_ffn[l], wpq), keys), u_bf, vt_bf, l)

        outs[0].append(k4(k_p, b, t)[:, t - wb:])
        outs[1].append(k4(pa_p[:, Q_W + KV_W:], b, t)[:, t - wb:])
        outs[2].append(conv_st[:, CONV_HALO - (CONV_WIDTH - 1):])
        outs[3].append(pc_p.reshape(b, t, C_COLS)[:, -1])
        outs[4].append(wkv_p)
        outs[5].append(mk.reshape(b, mem_len, CA_HEADS, CA_HEAD_DIM))
        outs[6].append(mv.reshape(b, mem_len, CA_HEADS, CA_HEAD_DIM))
        outs[7].append(jnp.concatenate([cache_win_k[l][:, dt:], k4(k_new, db, dt)], axis=1))
        outs[8].append(jnp.concatenate([cache_win_v[l][:, dt:], k4(v_new, db, dt)], axis=1))
        outs[9].append(conv_new.transpose(1, 0, 2))
        outs[10].append(pc_s.reshape(db, dt, C_COLS)[:, -1])
        outs[11].append(wkv_s)

    yp = final_norm(xp, g_final)
    ys = final_norm(xs, g_final)
    return (yp.reshape(b, t, d), ys.reshape(db, dt, d)) + tuple(jnp.stack(o) for o in outs)
```

```python
import functools

import numpy as np
import jax
import jax.numpy as jnp
from jax import lax
from jax.experimental import pallas as pl
from jax.experimental.pallas import tpu as pltpu

F32 = jnp.float32
BF16 = jnp.bfloat16

D_MODEL = 1024
PAST_LEN = 8192
ATT_HEADS = 8
ATT_KV_HEADS = 2
ATT_GROUP = ATT_HEADS // ATT_KV_HEADS
HEAD_DIM = 64
ROT_DIM = HEAD_DIM // 4
ROPE_THETA = 500000.0
WINDOW = 128
CONV_CH = 512
CONV_WIDTH = 31
LN_EPS = 1e-5
RWKV_HEADS = 8
RWKV_HEAD = 64
RWKV_W = RWKV_HEADS * RWKV_HEAD
DECAY_LORA = 64
ICLR_LORA = 64
GATE_LORA = 128
GN_EPS = 64e-5
CA_HEADS = 4
CA_HEAD_DIM = D_MODEL // CA_HEADS
PEER_HEADS = 8
N_KEYS = 128
PEER_TOPK = 16
PEER_HALF = 128
RMS_EPS = 1e-6
NEG = -1e30

Q_W = ATT_HEADS * HEAD_DIM
KV_W = ATT_KV_HEADS * HEAD_DIM
A_COLS = Q_W + 2 * KV_W
B_COLS = 2 * CONV_CH
C_COLS = 3 * RWKV_W + DECAY_LORA + ICLR_LORA + GATE_LORA
G_COLS = 3 * D_MODEL

VMEM_LIMIT_V7X = 56 * 1024 * 1024
LANES = 128
SUBLANES = 8
SCAN_PAIRS = 16
SCAN_TSUB = 8
NT_DIMS = (((1,), (1,)), ((), ()))


def _tile(n, prefs=(512, 256, 128, 64, 32, 16, 8)):
    for t in prefs:
        if n % t == 0:
            return t
    raise ValueError(f"no tile for {n}")


def _params(sem):
    return pltpu.CompilerParams(dimension_semantics=sem, vmem_limit_bytes=VMEM_LIMIT_V7X)


def _sigmoid(x):
    return 1.0 / (1.0 + jnp.exp(-x))


def _full(shape):
    nd = len(shape)
    return pl.BlockSpec(shape, lambda *_: (0,) * nd)


def _norm_matmul_kernel(x_ref, g_ref, w_ref, o_ref):
    x = x_ref[...]
    h = x * lax.rsqrt(jnp.mean(x * x, axis=-1, keepdims=True) + RMS_EPS) * g_ref[...]
    o_ref[...] = jnp.dot(h.astype(BF16), w_ref[...], preferred_element_type=F32)


def norm_matmul(x, g, w):
    n, d = x.shape
    c = w.shape[1]
    tm = _tile(n)
    return pl.pallas_call(
        _norm_matmul_kernel,
        grid=(n // tm,),
        in_specs=[pl.BlockSpec((tm, d), lambda i: (i, 0)), _full((1, d)), _full((d, c))],
        out_specs=pl.BlockSpec((tm, c), lambda i: (i, 0)),
        out_shape=jax.ShapeDtypeStruct((n, c), F32),
        compiler_params=_params(("parallel",)),
    )(x, g.reshape(1, d), w)


def _matmul_res_kernel(a_ref, w_ref, x_ref, o_ref):
    o_ref[...] = x_ref[...] + jnp.dot(a_ref[...].astype(BF16), w_ref[...], preferred_element_type=F32)


def matmul_residual(a, w, x):
    n, k = a.shape
    c = w.shape[1]
    tm = _tile(n)
    return pl.pallas_call(
        _matmul_res_kernel,
        grid=(n // tm,),
        in_specs=[pl.BlockSpec((tm, k), lambda i: (i, 0)), _full((k, c)), pl.BlockSpec((tm, c), lambda i: (i, 0))],
        out_specs=pl.BlockSpec((tm, c), lambda i: (i, 0)),
        out_shape=jax.ShapeDtypeStruct((n, c), F32),
        compiler_params=_params(("parallel",)),
    )(a, w, x)


def _final_norm_kernel(x_ref, g_ref, o_ref):
    x = x_ref[...]
    o_ref[...] = x * lax.rsqrt(jnp.mean(x * x, axis=-1, keepdims=True) + RMS_EPS) * g_ref[...]


def final_norm(x, g):
    n, d = x.shape
    tm = _tile(n)
    return pl.pallas_call(
        _final_norm_kernel,
        grid=(n // tm,),
        in_specs=[pl.BlockSpec((tm, d), lambda i: (i, 0)), _full((1, d))],
        out_specs=pl.BlockSpec((tm, d), lambda i: (i, 0)),
        out_shape=jax.ShapeDtypeStruct((n, d), F32),
        compiler_params=_params(("parallel",)),
    )(x, g.reshape(1, d))


def _rotary_tables(pos):
    half = ROT_DIM // 2
    inv = ROPE_THETA ** (-2.0 * jnp.arange(half, dtype=F32) / ROT_DIM)
    ang = pos[:, None] * inv[None, :]
    cos, sin = jnp.cos(ang), jnp.sin(ang)
    n = pos.shape[0]
    one = jnp.ones((n, HEAD_DIM - ROT_DIM), F32)
    zero = jnp.zeros((n, HEAD_DIM - ROT_DIM), F32)
    zh = jnp.zeros((n, half), F32)
    c = jnp.concatenate([cos, cos, one], axis=1)
    s_up = jnp.concatenate([-sin, zh, zero], axis=1)
    s_dn = jnp.concatenate([zh, sin, zero], axis=1)
    return tuple(jnp.concatenate([t, t], axis=1) for t in (c, s_up, s_dn))


def _rotary_kernel(pa_ref, c_ref, su_ref, sd_ref, q_ref, k_ref):
    c, su, sd = c_ref[...], su_ref[...], sd_ref[...]
    half = ROT_DIM // 2
    for j in range((Q_W + KV_W) // LANES):
        x = pa_ref[:, j * LANES:(j + 1) * LANES]
        y = x * c + pltpu.roll(x, LANES - half, 1) * su + pltpu.roll(x, half, 1) * sd
        if j < Q_W // LANES:
            q_ref[:, j * LANES:(j + 1) * LANES] = y
        else:
            k_ref[...] = y


def rotary_qk(pa, tabs):
    n = pa.shape[0]
    tm = _tile(n)
    row = lambda w: pl.BlockSpec((tm, w), lambda i: (i, 0))
    return pl.pallas_call(
        _rotary_kernel,
        grid=(n // tm,),
        in_specs=[row(A_COLS), row(LANES), row(LANES), row(LANES)],
        out_specs=[row(Q_W), row(KV_W)],
        out_shape=[jax.ShapeDtypeStruct((n, Q_W), F32), jax.ShapeDtypeStruct((n, KV_W), F32)],
        compiler_params=_params(("parallel",)),
    )(pa, *tabs)


def _sink_softmax_pv(parts, sink):
    m = sink
    for s, _ in parts:
        m = jnp.maximum(m, jnp.max(s, axis=-1, keepdims=True))
    ps = [jnp.exp(s - m) for s, _ in parts]
    den = jnp.exp(sink - m)
    for p in ps:
        den = den + jnp.sum(p, axis=-1, keepdims=True)
    inv = 1.0 / den
    out = None
    for p, (_, v) in zip(ps, parts):
        o = jnp.dot((p * inv).astype(BF16), v, preferred_element_type=F32)
        out = o if out is None else out + o
    return out


def _swa_prompt_kernel(sink_ref, q_ref, kc_ref, kp_ref, vc_ref, vp_ref, o_ref):
    n = pl.program_id(1)
    qi = lax.broadcasted_iota(jnp.int32, (WINDOW, WINDOW), 0)
    kj = lax.broadcasted_iota(jnp.int32, (WINDOW, WINDOW), 1)
    mask_c = kj <= qi
    mask_p = kj > qi + jnp.where(n > 0, 0, WINDOW)
    scale = HEAD_DIM ** -0.5
    lane = lax.broadcasted_iota(jnp.int32, (WINDOW, KV_W), 1)
    low = lane < HEAD_DIM

    def halves(x, g):
        other = pltpu.roll(x, HEAD_DIM, 1)
        in_low, in_high = (x, other) if g == 0 else (other, x)
        return jnp.where(low, in_low, 0.0).astype(BF16), jnp.where(low, 0.0, in_high).astype(BF16)

    for g in range(ATT_KV_HEADS):
        kc, kp = halves(kc_ref[...], g), halves(kp_ref[...], g)
        vc, vp = halves(vc_ref[...], g), halves(vp_ref[...], g)
        for pair in range(ATT_GROUP // 2):
            col = (g * (ATT_GROUP // 2) + pair) * KV_W
            q2 = q_ref[:, col:col + KV_W].astype(BF16)
            out = None
            for odd in range(2):
                sc = lax.dot_general(q2, kc[odd], NT_DIMS, preferred_element_type=F32) * scale
                sp = lax.dot_general(q2, kp[odd], NT_DIMS, preferred_element_type=F32) * scale
                sc = jnp.where(mask_c, sc, NEG)
                sp = jnp.where(mask_p, sp, NEG)
                o = _sink_softmax_pv([(sp, vp[odd]), (sc, vc[odd])], sink_ref[col // HEAD_DIM + odd])
                out = o if out is None else out + o
            o_ref[:, col:col + KV_W] = out


def swa_prompt(q, k, pa, sinks, b, t):
    nb = t // WINDOW
    cur = lambda w, c: pl.BlockSpec((WINDOW, w), lambda bi, ni: (bi * nb + ni, c))
    prev = lambda w, c: pl.BlockSpec((WINDOW, w), lambda bi, ni: (bi * nb + jnp.maximum(ni - 1, 0), c))
    vcol = (Q_W + KV_W) // KV_W
    return pl.pallas_call(
        _swa_prompt_kernel,
        grid=(b, nb),
        in_specs=[pl.BlockSpec(memory_space=pltpu.SMEM), cur(Q_W, 0), cur(KV_W, 0), prev(KV_W, 0),
                  cur(KV_W, vcol), prev(KV_W, vcol)],
        out_specs=cur(Q_W, 0),
        out_shape=jax.ShapeDtypeStruct((b * t, Q_W), F32),
        compiler_params=_params(("parallel", "parallel")),
    )(sinks, q, k, k, pa, pa)


def _swa_sample_kernel(sink_ref, q_ref, kb_ref, vb_ref, kn_ref, vn_ref, o_ref, *, bt, dt, wb):
    rows = ATT_GROUP * dt
    tb = lax.rem(lax.broadcasted_iota(jnp.int32, (rows, wb), 0), dt)
    jb = lax.broadcasted_iota(jnp.int32, (rows, wb), 1)
    mask_b = jb > tb + (wb - WINDOW)
    npad = kn_ref.shape[1]
    tn = lax.rem(lax.broadcasted_iota(jnp.int32, (rows, npad), 0), dt)
    jn = lax.broadcasted_iota(jnp.int32, (rows, npad), 1)
    mask_n = jn <= tn
    scale = HEAD_DIM ** -0.5
    for b in range(bt):
        for g in range(ATT_KV_HEADS):
            gs = slice(g * HEAD_DIM, (g + 1) * HEAD_DIM)
            q = q_ref[b, g].astype(BF16)
            sb = lax.dot_general(q, kb_ref[b, :, g, :].astype(BF16), NT_DIMS, preferred_element_type=F32) * scale
            sn = lax.dot_general(q, kn_ref[b, :, gs].astype(BF16), NT_DIMS, preferred_element_type=F32) * scale
            sb = jnp.where(mask_b, sb, NEG)
            sn = jnp.where(mask_n, sn, NEG)
            o_ref[b, g] = _sink_softmax_pv(
                [(sb, vb_ref[b, :, g, :].astype(BF16)), (sn, vn_ref[b, :, gs].astype(BF16))], sink_ref[g][:, :1])


def swa_sample(qs, cache_k, cache_v, layer, knew, vnew, sink_rows):
    db, _, rows, _ = qs.shape
    dt = rows // ATT_GROUP
    wb = cache_k.shape[2]
    bt = _tile(db, (8, 4, 2, 1))
    blk = lambda a: pl.BlockSpec((bt,) + a.shape[1:], lambda i: (i,) + (0,) * (a.ndim - 1))
    cache = pl.BlockSpec((None, bt) + cache_k.shape[2:], lambda i: (layer, i, 0, 0, 0))
    return pl.pallas_call(
        functools.partial(_swa_sample_kernel, bt=bt, dt=dt, wb=wb),
        grid=(db // bt,),
        in_specs=[_full(sink_rows.shape), blk(qs), cache, cache, blk(knew), blk(vnew)],
        out_specs=blk(qs),
        out_shape=jax.ShapeDtypeStruct(qs.shape, F32),
        compiler_params=_params(("parallel",)),
    )(sink_rows, qs, cache_k, cache_v, knew, vnew)


CONV_HALO = 32


def _ln_swish(y, lg, lb):
    mu = jnp.mean(y, axis=-1, keepdims=True)
    var = jnp.mean(jnp.square(y - mu), axis=-1, keepdims=True)
    yn = (y - mu) * lax.rsqrt(var + LN_EPS) * lg + lb
    return yn * _sigmoid(yn)


def _conv_prompt_kernel(pb_ref, dw_ref, db_ref, lg_ref, lb_ref, y_ref, st_ref, ext_ref, *, tt):
    i = pl.program_id(1)

    @pl.when(i == 0)
    def _():
        ext_ref[0:CONV_HALO, :] = jnp.zeros((CONV_HALO, CONV_CH), F32)

    @pl.when(i > 0)
    def _():
        ext_ref[0:CONV_HALO, :] = ext_ref[tt:tt + CONV_HALO, :]

    pb = pb_ref[...]
    ext_ref[CONV_HALO:CONV_HALO + tt, :] = pb[:, :CONV_CH] * _sigmoid(pb[:, CONV_CH:])
    acc = jnp.zeros((tt, CONV_CH), F32) + db_ref[...]
    first = CONV_HALO - (CONV_WIDTH - 1)
    for j in range(CONV_WIDTH):
        acc = acc + ext_ref[pl.ds(first + j, tt), :] * dw_ref[j:j + 1, :]
    y_ref[...] = _ln_swish(acc, lg_ref[...], lb_ref[...])

    @pl.when(i == pl.num_programs(1) - 1)
    def _():
        st_ref[0] = ext_ref[tt:tt + CONV_HALO, :]


def conv_prompt(pb, dw, db, lg, lb, b, t):
    tt = _tile(t)
    nt = t // tt
    vec = lambda a: a.reshape(1, CONV_CH)
    return pl.pallas_call(
        functools.partial(_conv_prompt_kernel, tt=tt),
        grid=(b, nt),
        in_specs=[pl.BlockSpec((tt, B_COLS), lambda bi, i: (bi * nt + i, 0)), _full((CONV_WIDTH, CONV_CH)),
                  _full((1, CONV_CH)), _full((1, CONV_CH)), _full((1, CONV_CH))],
        out_specs=[pl.BlockSpec((tt, CONV_CH), lambda bi, i: (bi * nt + i, 0)),
                   pl.BlockSpec((1, CONV_HALO, CONV_CH), lambda bi, i: (bi, 0, 0))],
        out_shape=[jax.ShapeDtypeStruct((b * t, CONV_CH), F32), jax.ShapeDtypeStruct((b, CONV_HALO, CONV_CH), F32)],
        scratch_shapes=[pltpu.VMEM((tt + CONV_HALO, CONV_CH), F32)],
        compiler_params=_params(("arbitrary", "arbitrary")),
    )(pb, dw, vec(db), vec(lg), vec(lb))


def _conv_sample_kernel(c_ref, pb_ref, dw_ref, db_ref, lg_ref, lb_ref, y_ref, nc_ref, *, dt):
    nprev = CONV_WIDTH - 1
    us = []
    for t in range(dt):
        pb = pb_ref[t]
        us.append(pb[:, :CONV_CH] * _sigmoid(pb[:, CONV_CH:]))

    def ext(j):
        return c_ref[j] if j < nprev else us[j - nprev]

    for t in range(dt):
        acc = db_ref[...] + ext(t) * dw_ref[0:1, :]
        for j in range(1, CONV_WIDTH):
            acc = acc + ext(t + j) * dw_ref[j:j + 1, :]
        y_ref[t] = _ln_swish(acc, lg_ref[...], lb_ref[...])
    for j in range(nprev):
        nc_ref[j] = ext(j + dt)


def conv_sample(cache_t, pb_t, dw, db, lg, lb):
    nprev, dbt, _ = cache_t.shape
    dt = pb_t.shape[0]
    bt = _tile(dbt, (32, 16, 8))
    vec = lambda a: a.reshape(1, CONV_CH)
    blk = lambda lead, w: pl.BlockSpec((lead, bt, w), lambda i: (0, i, 0))
    return pl.pallas_call(
        functools.partial(_conv_sample_kernel, dt=dt),
        grid=(dbt // bt,),
        in_specs=[blk(nprev, CONV_CH), blk(dt, B_COLS), _full((CONV_WIDTH, CONV_CH)), _full((1, CONV_CH)),
                  _full((1, CONV_CH)), _full((1, CONV_CH))],
        out_specs=[blk(dt, CONV_CH), blk(nprev, CONV_CH)],
        out_shape=[jax.ShapeDtypeStruct((dt, dbt, CONV_CH), F32), jax.ShapeDtypeStruct((nprev, dbt, CONV_CH), F32)],
        compiler_params=_params(("parallel",)),
    )(cache_t, pb_t, dw, vec(db), vec(lg), vec(lb))


def _head_sum_matrix():
    h = np.arange(RWKV_W) // RWKV_HEAD
    return jnp.asarray((h[:, None] == h[None, :]).astype(np.float32), dtype=BF16)


def _head_sum(x, hs_ref):
    hi = x.astype(BF16)
    lo = (x - hi.astype(F32)).astype(BF16)
    hs = hs_ref[...]
    return jnp.dot(hi, hs, preferred_element_type=F32) + jnp.dot(lo, hs, preferred_element_type=F32)


def _rwkv_pre_math(pc, prev, c):
    mu_ref, w0_ref, w2_ref, a0_ref, a2_ref, g2_ref, kkp_ref, ka_ref, rk_ref, hs_ref = c
    xs = pc + (prev - pc) * mu_ref[...]
    o1, o2, o3 = RWKV_W, 2 * RWKV_W, 3 * RWKV_W
    o4 = o3 + DECAY_LORA
    o5 = o4 + ICLR_LORA
    r, k, v = xs[:, :o1], xs[:, o1:o2], xs[:, o2:o3]
    wd, ad, gd = xs[:, o3:o4], xs[:, o4:o5], xs[:, o5:]
    y = -(w0_ref[...] + jnp.dot(jnp.tanh(wd).astype(BF16), w2_ref[...], preferred_element_type=F32))
    softplus = jnp.maximum(y, 0.0) + jnp.log(1.0 + jnp.exp(-jnp.abs(y)))
    decay = jnp.exp(-jnp.exp(-softplus - 0.5))
    a = _sigmoid(a0_ref[...] + jnp.dot(ad.astype(BF16), a2_ref[...], preferred_element_type=F32))
    g = jnp.dot(_sigmoid(gd).astype(BF16), g2_ref[...], preferred_element_type=F32)
    kk = k * kkp_ref[...]
    norm = jnp.sqrt(_head_sum(kk * kk, hs_ref))
    kk = kk / jnp.maximum(norm, 1e-12)
    k_mod = k * (1.0 + (a - 1.0) * ka_ref[...])
    bonus = _head_sum(r * k_mod * rk_ref[...], hs_ref) * v
    return r, decay, k_mod, -kk, kk * a, v, g, bonus


RWKV_PRE_OUTS = 8
SCAN_KEY_OPERANDS = 5
SCAN_CHUNK = 256
SHIFT_HALO = SUBLANES


def _rwkv_pre_prompt_kernel(pc_ref, *refs, tt, skew):
    consts, outs, ext_ref = refs[:10], refs[10:10 + RWKV_PRE_OUTS], refs[-1]
    i = pl.program_id(1)

    @pl.when(i == 0)
    def _():
        ext_ref[0:SHIFT_HALO, :] = jnp.zeros((SHIFT_HALO, C_COLS), F32)

    @pl.when(i > 0)
    def _():
        ext_ref[0:SHIFT_HALO, :] = ext_ref[tt:tt + SHIFT_HALO, :]

    pc = pc_ref[...]
    ext_ref[SHIFT_HALO:SHIFT_HALO + tt, :] = pc
    res = _rwkv_pre_math(pc, ext_ref[pl.ds(SHIFT_HALO - 1, tt), :], consts)
    for j, (o_ref, val) in enumerate(zip(outs, res)):
        if j < SCAN_KEY_OPERANDS and skew:
            vt = val.T
            for h in range(RWKV_HEADS):
                g = pl.program_id(0) * RWKV_HEADS + h
                rows = slice(h * RWKV_HEAD, (h + 1) * RWKV_HEAD)
                o_ref[rows, :] = pltpu.roll(vt[rows, :], g * SCAN_TSUB, 1)
        else:
            o_ref[...] = val


def _rwkv_pre_sample_kernel(pc_ref, shift_ref, *refs, dt):
    consts, outs = refs[:10], refs[10:10 + RWKV_PRE_OUTS]
    for t in range(dt):
        res = _rwkv_pre_math(pc_ref[t], shift_ref[...] if t == 0 else pc_ref[t - 1], consts)
        for o_ref, val in zip(outs, res):
            o_ref[t] = val


def _rwkv_pre_consts(lp):
    vec = lambda a: a.reshape(1, -1)
    return [vec(lp['rk_mu']), vec(lp['rk_w0']), lp['rk_w2'].astype(BF16), vec(lp['rk_a0']), lp['rk_a2'].astype(BF16),
            lp['rk_g2'].astype(BF16), vec(lp['rk_kk']), vec(lp['rk_ka']), vec(lp['rk_rk']), _head_sum_matrix()]


def scan_skewed(b, t):
    return b * RWKV_HEADS == SCAN_PAIRS and t % SCAN_CHUNK == 0


def rwkv_pre_prompt(pc, lp, b, t):
    skew = scan_skewed(b, t)
    tt = SCAN_CHUNK if skew else _tile(t, (256, 128))
    nt = t // tt
    consts = _rwkv_pre_consts(lp)
    row = lambda w: pl.BlockSpec((tt, w), lambda bi, i: (bi * nt + i, 0))
    nkey = SCAN_KEY_OPERANDS if skew else 0
    col = pl.BlockSpec((RWKV_W, tt), lambda bi, i: (bi, i))
    return pl.pallas_call(
        functools.partial(_rwkv_pre_prompt_kernel, tt=tt, skew=skew),
        grid=(b, nt),
        in_specs=[row(C_COLS)] + [_full(c.shape) for c in consts],
        out_specs=[col] * nkey + [row(RWKV_W)] * (RWKV_PRE_OUTS - nkey),
        out_shape=[jax.ShapeDtypeStruct((b * RWKV_W, t), F32)] * nkey
        + [jax.ShapeDtypeStruct((b * t, RWKV_W), F32)] * (RWKV_PRE_OUTS - nkey),
        scratch_shapes=[pltpu.VMEM((tt + SHIFT_HALO, C_COLS), F32)],
        compiler_params=_params(("arbitrary", "arbitrary")),
    )(pc, *consts)


def rwkv_pre_sample(pc_t, shift, lp):
    dt, dbt, _ = pc_t.shape
    bt = _tile(dbt, (64, 32, 16, 8))
    consts = _rwkv_pre_consts(lp)
    blk = lambda w: pl.BlockSpec((dt, bt, w), lambda i: (0, i, 0))
    return pl.pallas_call(
        functools.partial(_rwkv_pre_sample_kernel, dt=dt),
        grid=(dbt // bt,),
        in_specs=[blk(C_COLS), pl.BlockSpec((bt, C_COLS), lambda i: (i, 0))] + [_full(c.shape) for c in consts],
        out_specs=[blk(RWKV_W)] * RWKV_PRE_OUTS,
        out_shape=[jax.ShapeDtypeStruct((dt, dbt, RWKV_W), F32)] * RWKV_PRE_OUTS,
        compiler_params=_params(("parallel",)),
    )(pc_t, shift, *consts)


def _scan_select_matrices(pair_major):
    e = np.zeros((SCAN_TSUB, LANES, LANES), np.float32)
    vl = LANES // SCAN_PAIRS
    for tl in range(SCAN_TSUB):
        for g in range(SCAN_PAIRS):
            src = g * SCAN_TSUB + tl if pair_major else tl * SCAN_PAIRS + g
            e[tl, src, g * vl:(g + 1) * vl] = 1.0
    return jnp.asarray(np.concatenate([e, e, e], axis=1), dtype=BF16)


def _sublane_allsum(x):
    x = x + pltpu.roll(x, 4, 0)
    x = x + pltpu.roll(x, 2, 0)
    return x + pltpu.roll(x, 1, 0)


def _scan_kernel(xr_ref, xw_ref, xk_ref, xa_ref, xb_ref, v_ref, s0_ref, e_ref, o_ref, sf_ref, z_ref, tile_ref, xs_ref,
                 *, groups, steps, skewed):
    x_refs = (xr_ref, xw_ref, xk_ref, xa_ref, xb_ref)
    c = pl.program_id(1)
    nvh = z_ref.shape[0]
    nkb = RWKV_HEAD // SUBLANES
    sub = lax.broadcasted_iota(jnp.int32, (SUBLANES, LANES), 0)
    R_ROW, W_ROW, K_ROW, A_ROW, B_ROW = (j * RWKV_HEAD for j in range(5))

    @pl.when(c == 0)
    def _():
        z_ref[...] = s0_ref[0]

    lane_group = lax.broadcasted_iota(jnp.int32, (SUBLANES, LANES), 1) // SCAN_TSUB

    def gather_group(x_ref, th, kb):
        acc = jnp.zeros((SUBLANES, LANES), F32)
        for g in range(SCAN_PAIRS):
            pos = th + g
            tile = pl.multiple_of(((pos // SCAN_PAIRS) % 2) * LANES, LANES)
            rows = slice(g * RWKV_HEAD + kb * SUBLANES, g * RWKV_HEAD + (kb + 1) * SUBLANES)
            acc = jnp.where(lane_group == pos % SCAN_PAIRS, x_ref[rows, pl.ds(tile, LANES)], acc)
        return pltpu.roll(acc, ((SCAN_PAIRS - th % SCAN_PAIRS) % SCAN_PAIRS) * SCAN_TSUB, 1)

    def split_group(th):
        th = jnp.asarray(th, jnp.int32)
        for j, x_ref in enumerate(x_refs):
            rows = slice(j * RWKV_HEAD, (j + 1) * RWKV_HEAD)
            if skewed:
                xg = jnp.concatenate([gather_group(x_ref, th, kb) for kb in range(nkb)], axis=0)
            else:
                xg = x_ref[0, th]
            hi = xg.astype(BF16)
            r1 = xg - hi.astype(F32)
            mid = r1.astype(BF16)
            lo = (r1 - mid.astype(F32)).astype(BF16)
            xs_ref[rows, 0:LANES] = hi
            xs_ref[rows, LANES:2 * LANES] = mid
            xs_ref[rows, 2 * LANES:3 * LANES] = lo

    def spread(tl, buf):
        tile_ref[buf] = jnp.dot(xs_ref[...], e_ref[tl], preferred_element_type=F32)

    split_group(0)
    spread(0, 0)

    def body(th, carry):
        for tl in range(steps):
            cur = tl % 2
            if tl + 1 < steps:
                spread(tl + 1, 1 - cur)
            else:
                split_group(jnp.minimum(th + 1, groups - 1))
                spread(0, 1 - cur)
            t = th * steps + tl
            vrow = v_ref[0, t]
            blk = lambda row0, kb: tile_ref[cur, row0 + kb * SUBLANES:row0 + (kb + 1) * SUBLANES, :]
            sa = []
            for vh in range(nvh):
                acc = None
                for kb in range(nkb):
                    p = z_ref[vh, kb * SUBLANES:(kb + 1) * SUBLANES, :] * blk(A_ROW, kb)
                    acc = p if kb == 0 else acc + p
                sa.append(_sublane_allsum(acc))
            orow = jnp.zeros((SUBLANES, LANES), F32)
            for vh in range(nvh):
                vb = jnp.broadcast_to(vrow[vh:vh + 1], (SUBLANES, LANES))
                acc = None
                for kb in range(nkb):
                    ks = slice(kb * SUBLANES, (kb + 1) * SUBLANES)
                    zn = z_ref[vh, ks, :] * blk(W_ROW, kb) + blk(B_ROW, kb) * sa[vh] + blk(K_ROW, kb) * vb
                    z_ref[vh, ks, :] = zn
                    p = zn * blk(R_ROW, kb)
                    acc = p if kb == 0 else acc + p
                orow = jnp.where(sub == vh, _sublane_allsum(acc), orow)
            o_ref[0, t] = orow
        return carry

    lax.fori_loop(0, groups, body, 0)

    @pl.when(c == pl.num_programs(1) - 1)
    def _():
        sf_ref[0] = z_ref[...]


def rwkv_scan(x5, v, s0, nseq, t, skewed=False):
    spb = SCAN_PAIRS // RWKV_HEADS
    assert nseq % spb == 0
    nblk = nseq // spb
    vl = LANES // SCAN_PAIRS
    vh = RWKV_HEAD // vl
    steps = min(SCAN_TSUB, t)
    assert t % steps == 0 and steps % 2 == 0
    t8 = t // steps
    def key_layout(a):
        a = a.reshape(nblk, spb, t8, steps, RWKV_HEADS, RWKV_HEAD)
        if steps < SCAN_TSUB:
            a = jnp.pad(a, ((0, 0),) * 3 + ((0, SCAN_TSUB - steps),) + ((0, 0),) * 2)
        return a.transpose(0, 2, 5, 3, 1, 4).reshape(nblk, t8, RWKV_HEAD, LANES)

    vk = v.reshape(nblk, spb, t, RWKV_HEADS, vh, vl).transpose(0, 2, 4, 1, 3, 5).reshape(nblk, t, vh, LANES)
    sk = s0.reshape(nblk, spb, RWKV_HEADS, vh, vl, RWKV_HEAD).transpose(0, 3, 5, 1, 2, 4).reshape(
        nblk, vh, RWKV_HEAD, LANES)
    if skewed:
        assert steps == SCAN_TSUB and t % SCAN_CHUNK == 0
        xk = x5
        groups = SCAN_CHUNK // steps
        key_spec = pl.BlockSpec((SCAN_PAIRS * RWKV_HEAD, SCAN_CHUNK), lambda i, c: (i, c))
    else:
        xk = [key_layout(a) for a in x5]
        groups = _tile(t8, (32, 16, 8, 4, 2, 1))
        key_spec = pl.BlockSpec((1, groups, RWKV_HEAD, LANES), lambda i, c: (i, c, 0, 0))
    nchunk = t8 // groups
    ch = groups * steps
    o, sf = pl.pallas_call(
        functools.partial(_scan_kernel, groups=groups, steps=steps, skewed=skewed),
        grid=(nblk, nchunk),
        in_specs=[key_spec] * 5 + [
                  pl.BlockSpec((1, ch, vh, LANES), lambda i, c: (i, c, 0, 0)),
                  pl.BlockSpec((1, vh, RWKV_HEAD, LANES), lambda i, c: (i, 0, 0, 0)),
                  _full((SCAN_TSUB, 3 * LANES, LANES))],
        out_specs=[pl.BlockSpec((1, ch, vh, LANES), lambda i, c: (i, c, 0, 0)),
                   pl.BlockSpec((1, vh, RWKV_HEAD, LANES), lambda i, c: (i, 0, 0, 0))],
        out_shape=[jax.ShapeDtypeStruct((nblk, t, vh, LANES), F32),
                   jax.ShapeDtypeStruct((nblk, vh, RWKV_HEAD, LANES), F32)],
        scratch_shapes=[pltpu.VMEM((vh, RWKV_HEAD, LANES), F32), pltpu.VMEM((2, 5 * RWKV_HEAD, LANES), F32),
                        pltpu.VMEM((5 * RWKV_HEAD, 3 * LANES), BF16)],
        compiler_params=_params(("arbitrary", "arbitrary")),
    )(*xk, vk, sk, _scan_select_matrices(pair_major=skewed))
    o = o.reshape(nblk, t, vh, spb, RWKV_HEADS, vl).transpose(0, 3, 1, 4, 2, 5).reshape(nseq * t, RWKV_W)
    sf = sf.reshape(nblk, vh, RWKV_HEAD, spb, RWKV_HEADS, vl).transpose(0, 3, 4, 1, 5, 2).reshape(
        nseq, RWKV_HEADS, RWKV_HEAD, RWKV_HEAD)
    return o, sf


def _merge_kernel(x_ref, oa_ref, yb_ref, oc_ref, bonus_ref, g_ref, pg_ref, wa_ref, wb_ref, bb_ref, wc_ref, wo_ref,
                  gng_ref, gnb_ref, hs_ref, o_ref):
    oc = oc_ref[...]
    mu = _head_sum(oc, hs_ref) * (1.0 / RWKV_HEAD)
    dev = oc - mu
    var = _head_sum(dev * dev, hs_ref) * (1.0 / RWKV_HEAD)
    on = dev * lax.rsqrt(var + GN_EPS) * gng_ref[...] + gnb_ref[...]
    yc = (on + bonus_ref[...]) * g_ref[...]
    d = D_MODEL
    dot = lambda a, w: jnp.dot(a.astype(BF16), w[...], preferred_element_type=F32)
    merged = (_sigmoid(pg_ref[:, 0:d]) * dot(oa_ref[...], wa_ref)
              + _sigmoid(pg_ref[:, d:2 * d]) * (dot(yb_ref[...], wb_ref) + bb_ref[...])
              + _sigmoid(pg_ref[:, 2 * d:3 * d]) * dot(yc, wc_ref))
    o_ref[...] = x_ref[...] + dot(merged, wo_ref)


def merge_out(x, oa, yb, oc, bonus, g, pg, lp):
    n = x.shape[0]
    tm = _tile(n, (256, 128, 64, 32, 16, 8))
    row = lambda w: pl.BlockSpec((tm, w), lambda i: (i, 0))
    vec = lambda a: a.reshape(1, -1)
    consts = [lp['w_a_out'].astype(BF16), lp['w_b_out'].astype(BF16), vec(lp['b_b_out']), lp['w_c_out'].astype(BF16),
              lp['w_o'].astype(BF16), vec(lp['rk_gn_g']), vec(lp['rk_gn_b']), _head_sum_matrix()]
    return pl.pallas_call(
        _merge_kernel,
        grid=(n // tm,),
        in_specs=[row(D_MODEL), row(Q_W), row(CONV_CH), row(RWKV_W), row(RWKV_W), row(RWKV_W), row(G_COLS)]
        + [_full(c.shape) for c in consts],
        out_specs=row(D_MODEL),
        out_shape=jax.ShapeDtypeStruct((n, D_MODEL), F32),
        compiler_params=_params(("parallel",)),
    )(x, oa, yb, oc, bonus, g, pg, *consts)


def _cross_attn_kernel(q_ref, k_ref, v_ref, o_ref):
    scale = CA_HEAD_DIM ** -0.5
    outs = []
    for h in range(CA_HEADS):
        hs = slice(h * CA_HEAD_DIM, (h + 1) * CA_HEAD_DIM)
        q = q_ref[0, :, hs].astype(BF16)
        s = lax.dot_general(q, k_ref[0, :, hs].astype(BF16), NT_DIMS, preferred_element_type=F32) * scale
        m = jnp.max(s, axis=-1, keepdims=True)
        p = jnp.exp(s - m)
        p = p / jnp.sum(p, axis=-1, keepdims=True)
        outs.append(jnp.dot(p.astype(BF16), v_ref[0, :, hs].astype(BF16), preferred_element_type=F32))
    o_ref[0] = jnp.concatenate(outs, axis=1)


def cross_attn(q, mk, mv, tiles_per_seq, first_seq=0):
    nt, tq, d = q.shape
    m = mk.shape[1]
    kv = pl.BlockSpec((1, m, d), lambda i: (first_seq + i // tiles_per_seq, 0, 0))
    qs = pl.BlockSpec((1, tq, d), lambda i: (i, 0, 0))
    return pl.pallas_call(
        _cross_attn_kernel,
        grid=(nt,),
        in_specs=[qs, kv, kv],
        out_specs=qs,
        out_shape=jax.ShapeDtypeStruct(q.shape, F32),
        compiler_params=_params(("parallel",)),
    )(q, mk, mv)


def _staircase():
    return [(i, PEER_TOPK // (i + 1)) for i in range(PEER_TOPK)]


PEER_CAND = sum(nj for _, nj in _staircase())
PEER_CAND_PAD = -(-PEER_CAND // SUBLANES) * SUBLANES


def _extract_topk(cur, out_ref, base, with_rank=False):
    rank = jnp.full(cur.shape, float(PEER_TOPK), F32) if with_rank else None
    for k in range(PEER_TOPK):
        m = jnp.max(cur, axis=0, keepdims=True)
        out_ref[base + k:base + k + 1, :] = m
        hit = cur == m
        if with_rank:
            rank = jnp.where(hit, float(k), rank)
        if k + 1 < PEER_TOPK:
            cur = jnp.where(hit, -jnp.inf, cur)
    return rank


def _peer_select_kernel(qp_ref, keys_ref, r1_ref, rho_ref, m2_ref, e2_ref, s_ref, tv_ref, cand_ref, best_ref):
    nsub = 2 * PEER_HEADS
    for hc in range(nsub):
        q = qp_ref[:, hc * PEER_HALF:(hc + 1) * PEER_HALF].astype(BF16)
        st = lax.dot_general(keys_ref[hc], q, NT_DIMS, preferred_element_type=F32)
        s_ref[hc] = st
        rank = _extract_topk(st, tv_ref, hc * PEER_TOPK, with_rank=hc % 2 == 0)
        if hc % 2 == 0:
            r1_ref[hc // 2] = rank
    tm = qp_ref.shape[0]
    for h in range(PEER_HEADS):
        b1 = 2 * h * PEER_TOPK
        b2 = b1 + PEER_TOPK
        cand_ref[PEER_CAND_PAD - SUBLANES:PEER_CAND_PAD, :] = jnp.full((SUBLANES, tm), -jnp.inf, F32)
        off = 0
        for i, nj in _staircase():
            cand_ref[off:off + nj, :] = tv_ref[b1 + i:b1 + i + 1, :] + tv_ref[b2:b2 + nj, :]
            off += nj
        _extract_topk(cand_ref[...], best_ref, 0)
        best = best_ref[...]
        z = jnp.sum(jnp.exp(best - best[0:1, :]), axis=0, keepdims=True)
        thr = best[PEER_TOPK - 1:PEER_TOPK, :]
        s2 = s_ref[2 * h + 1]
        m2 = jnp.zeros(s2.shape, F32)
        for i in range(PEER_TOPK):
            m2 = jnp.where((tv_ref[b1 + i:b1 + i + 1, :] + s2) >= thr, float(i + 1), m2)
        m2_ref[h] = m2
        e2_ref[h] = jnp.exp(s2 - tv_ref[b2:b2 + 1, :])
        rho_ref[h] = jnp.exp(s_ref[2 * h] - tv_ref[b1:b1 + 1, :]) * (1.0 / z)


def peer_select(qp, keys):
    n = qp.shape[0]
    tm = _tile(n, (256, 128))
    nsub = 2 * PEER_HEADS
    blk = pl.BlockSpec((PEER_HEADS, N_KEYS, tm), lambda i: (0, 0, i))
    shp = lambda dt: jax.ShapeDtypeStruct((PEER_HEADS, N_KEYS, n), dt)
    return pl.pallas_call(
        _peer_select_kernel,
        grid=(n // tm,),
        in_specs=[pl.BlockSpec((tm, nsub * PEER_HALF), lambda i: (i, 0)), _full(keys.shape)],
        out_specs=[blk, blk, blk, blk],
        out_shape=[shp(F32)] * 4,
        scratch_shapes=[pltpu.VMEM((nsub, N_KEYS, tm), F32), pltpu.VMEM((nsub * PEER_TOPK, tm), F32),
                        pltpu.VMEM((PEER_CAND_PAD, tm), F32), pltpu.VMEM((PEER_TOPK, tm), F32)],
        compiler_params=_params(("parallel",)),
    )(qp, keys)


PEER_PACK_ROWS = 16
PEER_EXPERT_TILE = 4 * N_KEYS


def _gelu_tanh(x):
    return 0.5 * x * (1.0 + jnp.tanh(0.7978845608028654 * (x + 0.044715 * (x * x * x))))


def _peer_dense_kernel(x_ref, g_ref, r1_ref, rho_ref, m2_ref, e2_ref, u_ref, vt_ref, o_ref, xnt_ref, acc_ref, ht_ref,
                       wh_ref, m2b_ref, e2b_ref, rows_ref, *, te):
    e = pl.program_id(1)
    tm = x_ref.shape[0]
    pk = PEER_PACK_ROWS

    @pl.when(e == 0)
    def _():
        x = x_ref[...]
        xn = x * lax.rsqrt(jnp.mean(x * x, axis=-1, keepdims=True) + RMS_EPS) * g_ref[...]
        xnt_ref[...] = xn.T.astype(BF16)
        acc_ref[...] = jnp.zeros_like(acc_ref)
        m2b_ref[...] = m2_ref[...].astype(BF16)
        e2b_ref[...] = e2_ref[...].astype(BF16)

    n_a = te // N_KEYS
    for al in range(n_a):
        a = e * n_a + al
        for h in range(PEER_HEADS):
            row = 2 * (al * PEER_HEADS + h)
            rows_ref[row:row + 1, :] = r1_ref[h, pl.ds(a, 1), :]
            rows_ref[row + 1:row + 2, :] = rho_ref[h, pl.ds(a, 1), :]

    ht_ref[...] = jnp.dot(u_ref[...], xnt_ref[...], preferred_element_type=F32)
    zero = jnp.zeros((), BF16)
    for al in range(n_a):
        for cl in range(tm // LANES):
            cs = slice(cl * LANES, (cl + 1) * LANES)
            packed = lambda row: jnp.broadcast_to(rows_ref[row:row + 1, cs], (pk, LANES)).astype(BF16)
            r1_rows = [packed(2 * (al * PEER_HEADS + h)) for h in range(PEER_HEADS)]
            rho_rows = [packed(2 * (al * PEER_HEADS + h) + 1) for h in range(PEER_HEADS)]
            for rb in range(N_KEYS // pk):
                rs = slice(rb * pk, (rb + 1) * pk)
                w = None
                for h in range(PEER_HEADS):
                    sel = r1_rows[h] < m2b_ref[h, rs, cs]
                    contrib = jnp.where(sel, e2b_ref[h, rs, cs] * rho_rows[h], zero)
                    w = contrib if w is None else w + contrib
                hr = slice(al * N_KEYS + rb * pk, al * N_KEYS + (rb + 1) * pk)
                wh_ref[hr, cs] = w * _gelu_tanh(ht_ref[hr, cs]).astype(BF16)
    acc_ref[...] += jnp.dot(vt_ref[...], wh_ref[...], preferred_element_type=F32)

    @pl.when(e == pl.num_programs(1) - 1)
    def _():
        o_ref[...] = x_ref[...] + acc_ref[...].T


def peer_dense(x, g, sel, u, v_t, layer):
    n, d = x.shape
    ne = u.shape[1]
    tm = _tile(n, (512, 256, 128))
    te = PEER_EXPERT_TILE
    blk = pl.BlockSpec((PEER_HEADS, N_KEYS, tm), lambda i, e: (0, 0, i))
    return pl.pallas_call(
        functools.partial(_peer_dense_kernel, te=te),
        grid=(n // tm, ne // te),
        in_specs=[pl.BlockSpec((tm, d), lambda i, e: (i, 0)), _full((1, d)), blk, blk, blk, blk,
                  pl.BlockSpec((None, te, d), lambda i, e: (layer, e, 0)),
                  pl.BlockSpec((None, d, te), lambda i, e: (layer, 0, e))],
        out_specs=pl.BlockSpec((tm, d), lambda i, e: (i, 0)),
        out_shape=jax.ShapeDtypeStruct((n, d), F32),
        scratch_shapes=[pltpu.VMEM((d, tm), BF16), pltpu.VMEM((d, tm), F32), pltpu.VMEM((te, tm), F32),
                        pltpu.VMEM((te, tm), BF16), pltpu.VMEM((PEER_HEADS, N_KEYS, tm), BF16),
                        pltpu.VMEM((PEER_HEADS, N_KEYS, tm), BF16),
                        pltpu.VMEM((2 * PEER_HEADS * (te // N_KEYS), tm), F32)],
        compiler_params=_params(("parallel", "arbitrary")),
    )(x, g.reshape(1, d), *sel, u, v_t)


def kernel(x_prompt, x_sample, cache_win_k, cache_win_v, cache_conv, state_shift, state_wkv, cache_mem_k, cache_mem_v,
           mem_prompt, g_mix, w_in, att_sinks, w_a_out, conv_dw, conv_db, conv_ln_g, conv_ln_b, w_b_out, b_b_out,
           rk_mu, rk_w0, rk_w2, rk_a0, rk_a2, rk_g2, rk_kk, rk_ka, rk_rk, rk_gn_g, rk_gn_b, w_c_out, w_o,
           g_ca, g_mem, w_cq, w_mk, w_mv, w_co, g_ffn, w_pq, peer_keys, peer_u, peer_v, g_final):
    b, t, d = x_prompt.shape
    db, dt, _ = x_sample.shape
    depth = w_in.shape[0]
    wb = cache_win_k.shape[2]
    mem_len = mem_prompt.shape[1]
    n_p, n_s = b * t, db * dt
    assert t % WINDOW == 0 and wb == WINDOW and dt <= SUBLANES

    xp = x_prompt.reshape(n_p, d)
    xs = x_sample.reshape(n_s, d)
    tabs_p = _rotary_tables(jnp.tile(jnp.arange(t, dtype=F32), b))
    tabs_s = _rotary_tables(jnp.tile(PAST_LEN + jnp.arange(dt, dtype=F32), db))
    tq = _tile(t)
    new_pad = SUBLANES - dt
    u_bf = peer_u.astype(BF16)
    vt_bf = jnp.swapaxes(peer_v.astype(BF16), 1, 2)
    mem = mem_prompt.reshape(b * mem_len, d)
    mem_k_all = cache_mem_k.reshape(depth * db, mem_len, d)
    mem_v_all = cache_mem_v.reshape(depth * db, mem_len, d)
    time_major = lambda a: a.reshape(db, dt, a.shape[-1]).transpose(1, 0, 2)
    seq_major = lambda a: a.transpose(1, 0, 2).reshape(n_s, a.shape[-1])
    k4 = lambda a, nb_, tt_: a.reshape(nb_, tt_, ATT_KV_HEADS, HEAD_DIM)

    outs = [[] for _ in range(12)]
    for l in range(depth):
        lp = {'rk_mu': rk_mu[l], 'rk_w0': rk_w0[l], 'rk_w2': rk_w2[l], 'rk_a0': rk_a0[l], 'rk_a2': rk_a2[l],
              'rk_g2': rk_g2[l], 'rk_kk': rk_kk[l], 'rk_ka': rk_ka[l], 'rk_rk': rk_rk[l].reshape(-1),
              'rk_gn_g': rk_gn_g[l], 'rk_gn_b': rk_gn_b[l], 'w_a_out': w_a_out[l], 'w_b_out': w_b_out[l],
              'b_b_out': b_b_out[l], 'w_c_out': w_c_out[l], 'w_o': w_o[l]}
        win = w_in[l].astype(BF16)
        c0, c1, c2 = A_COLS, A_COLS + B_COLS, A_COLS + B_COLS + C_COLS
        w_seg = [win[:, :c0], win[:, c0:c1], win[:, c1:c2], win[:, c2:]]
        pa_p, pb_p, pc_p, pg_p = (norm_matmul(xp, g_mix[l], w) for w in w_seg)
        pa_s, pb_s, pc_s, pg_s = (norm_matmul(xs, g_mix[l], w) for w in w_seg)

        q_p, k_p = rotary_qk(pa_p, tabs_p)
        q_s, k_s = rotary_qk(pa_s, tabs_s)
        oa_p = swa_prompt(q_p, k_p, pa_p, att_sinks[l], b, t)
        qs4 = q_s.reshape(db, dt, ATT_KV_HEADS, ATT_GROUP, HEAD_DIM).transpose(0, 2, 3, 1, 4).reshape(
            db, ATT_KV_HEADS, ATT_GROUP * dt, HEAD_DIM)
        k_new = k_s.reshape(db, dt, KV_W)
        v_new = pa_s[:, Q_W + KV_W:].reshape(db, dt, KV_W)
        padn = lambda a: jnp.pad(a, ((0, 0), (0, new_pad), (0, 0)))
        sink_rows = jnp.broadcast_to(
            jnp.repeat(att_sinks[l].reshape(ATT_KV_HEADS, ATT_GROUP), dt, axis=1)[:, :, None],
            (ATT_KV_HEADS, ATT_GROUP * dt, LANES))
        oa_s = swa_sample(qs4, cache_win_k, cache_win_v, l, padn(k_new), padn(v_new), sink_rows)
        oa_s = oa_s.reshape(db, ATT_KV_HEADS, ATT_GROUP, dt, HEAD_DIM).transpose(0, 3, 1, 2, 4).reshape(n_s, Q_W)

        yb_p, conv_st = conv_prompt(pb_p, conv_dw[l], conv_db[l], conv_ln_g[l], conv_ln_b[l], b, t)
        yb_s, conv_new = conv_sample(cache_conv[l].transpose(1, 0, 2), time_major(pb_s),
                                     conv_dw[l], conv_db[l], conv_ln_g[l], conv_ln_b[l])
        yb_s = seq_major(yb_s)

        *x5_p, v_p, g_p, bonus_p = rwkv_pre_prompt(pc_p, lp, b, t)
        pre_s = [seq_major(a) for a in rwkv_pre_sample(time_major(pc_s), state_shift[l], lp)]
        *x5_s, v_s, g_s, bonus_s = pre_s
        oc_p, wkv_p = rwkv_scan(x5_p, v_p, jnp.zeros((b, RWKV_HEADS, RWKV_HEAD, RWKV_HEAD), F32), b, t,
                                skewed=scan_skewed(b, t))
        oc_s, wkv_s = rwkv_scan(x5_s, v_s, state_wkv[l], db, dt)

        xp = merge_out(xp, oa_p, yb_p, oc_p, bonus_p, g_p, pg_p, lp)
        xs = merge_out(xs, oa_s, yb_s, oc_s, bonus_s, g_s, pg_s, lp)

        mk = norm_matmul(mem, g_mem[l], w_mk[l].astype(BF16))
        mv = norm_matmul(mem, g_mem[l], w_mv[l].astype(BF16))
        wcq, wco = w_cq[l].astype(BF16), w_co[l].astype(BF16)
        ca_p = cross_attn(norm_matmul(xp, g_ca[l], wcq).reshape(n_p // tq, tq, d), mk.reshape(b, mem_len, d),
                          mv.reshape(b, mem_len, d), t // tq)
        ca_s = cross_attn(norm_matmul(xs, g_ca[l], wcq).reshape(db, dt, d), mem_k_all, mem_v_all, 1, first_seq=l * db)
        xp = matmul_residual(ca_p.reshape(n_p, d), wco, xp)
        xs = matmul_residual(ca_s.reshape(n_s, d), wco, xs)

        wpq = w_pq[l].astype(BF16)
        keys = peer_keys[l].reshape(2 * PEER_HEADS, N_KEYS, PEER_HALF).astype(BF16)
        xp = peer_dense(xp, g_ffn[l], peer_select(norm_matmul(xp, g_ffn[l], wpq), keys), u_bf, vt_bf, l)
        xs = peer_dense(xs, g_ffn[l], peer_select(norm_matmul(xs, g_ffn[l], wpq), keys), u_bf, vt_bf, l)

        outs[0].append(k4(k_p, b, t)[:, t - wb:])
        outs[1].append(k4(pa_p[:, Q_W + KV_W:], b, t)[:, t - wb:])
        outs[2].append(conv_st[:, CONV_HALO - (CONV_WIDTH - 1):])
        outs[3].append(pc_p.reshape(b, t, C_COLS)[:, -1])
        outs[4].append(wkv_p)
        outs[5].append(mk.reshape(b, mem_len, CA_HEADS, CA_HEAD_DIM))
        outs[6].append(mv.reshape(b, mem_len, CA_HEADS, CA_HEAD_DIM))
        outs[7].append(jnp.concatenate([cache_win_k[l][:, dt:], k4(k_new, db, dt)], axis=1))
        outs[8].append(jnp.concatenate([cache_win_v[l][:, dt:], k4(v_new, db, dt)], axis=1))
        outs[9].append(conv_new.transpose(1, 0, 2))
        outs[10].append(pc_s.reshape(db, dt, C_COLS)[:, -1])
        outs[11].append(wkv_s)

    yp = final_norm(xp, g_final)
    ys = final_norm(xs, g_final)
    return (yp.reshape(b, t, d), ys.reshape(db, dt, d)) + tuple(jnp.stack(o) for o in outs)
```

```python
import functools

import numpy as np
import jax
import jax.numpy as jnp
from jax import lax
from jax.experimental import pallas as pl
from jax.experimental.pallas import tpu as pltpu

F32 = jnp.float32
BF16 = jnp.bfloat16

D_MODEL = 1024
PAST_LEN = 8192
ATT_HEADS = 8
ATT_KV_HEADS = 2
ATT_GROUP = ATT_HEADS // ATT_KV_HEADS
HEAD_DIM = 64
ROT_DIM = HEAD_DIM // 4
ROPE_THETA = 500000.0
WINDOW = 128
CONV_CH = 512
CONV_WIDTH = 31
LN_EPS = 1e-5
RWKV_HEADS = 8
RWKV_HEAD = 64
RWKV_W = RWKV_HEADS * RWKV_HEAD
DECAY_LORA = 64
ICLR_LORA = 64
GATE_LORA = 128
GN_EPS = 64e-5
CA_HEADS = 4
CA_HEAD_DIM = D_MODEL // CA_HEADS
PEER_HEADS = 8
N_KEYS = 128
PEER_TOPK = 16
PEER_HALF = 128
RMS_EPS = 1e-6
NEG = -1e30

Q_W = ATT_HEADS * HEAD_DIM
KV_W = ATT_KV_HEADS * HEAD_DIM
A_COLS = Q_W + 2 * KV_W
B_COLS = 2 * CONV_CH
C_COLS = 3 * RWKV_W + DECAY_LORA + ICLR_LORA + GATE_LORA
G_COLS = 3 * D_MODEL

VMEM_LIMIT_V7X = 56 * 1024 * 1024
LANES = 128
SUBLANES = 8
SCAN_PAIRS = 16
SCAN_TSUB = 8
NT_DIMS = (((1,), (1,)), ((), ()))


def _tile(n, prefs=(512, 256, 128, 64, 32, 16, 8)):
    for t in prefs:
        if n % t == 0:
            return t
    raise ValueError(f"no tile for {n}")


def _params(sem):
    return pltpu.CompilerParams(dimension_semantics=sem, vmem_limit_bytes=VMEM_LIMIT_V7X)


def _sigmoid(x):
    return 1.0 / (1.0 + jnp.exp(-x))


def _full(shape):
    nd = len(shape)
    return pl.BlockSpec(shape, lambda *_: (0,) * nd)


def _norm_matmul_kernel(x_ref, g_ref, w_ref, o_ref):
    x = x_ref[...]
    h = x * lax.rsqrt(jnp.mean(x * x, axis=-1, keepdims=True) + RMS_EPS) * g_ref[...]
    o_ref[...] = jnp.dot(h.astype(BF16), w_ref[...], preferred_element_type=F32)


def norm_matmul(x, g, w):
    n, d = x.shape
    c = w.shape[1]
    tm = _tile(n)
    return pl.pallas_call(
        _norm_matmul_kernel,
        grid=(n // tm,),
        in_specs=[pl.BlockSpec((tm, d), lambda i: (i, 0)), _full((1, d)), _full((d, c))],
        out_specs=pl.BlockSpec((tm, c), lambda i: (i, 0)),
        out_shape=jax.ShapeDtypeStruct((n, c), F32),
        compiler_params=_params(("parallel",)),
    )(x, g.reshape(1, d), w)


def _matmul_res_kernel(a_ref, w_ref, x_ref, o_ref):
    o_ref[...] = x_ref[...] + jnp.dot(a_ref[...].astype(BF16), w_ref[...], preferred_element_type=F32)


def matmul_residual(a, w, x):
    n, k = a.shape
    c = w.shape[1]
    tm = _tile(n)
    return pl.pallas_call(
        _matmul_res_kernel,
        grid=(n // tm,),
        in_specs=[pl.BlockSpec((tm, k), lambda i: (i, 0)), _full((k, c)), pl.BlockSpec((tm, c), lambda i: (i, 0))],
        out_specs=pl.BlockSpec((tm, c), lambda i: (i, 0)),
        out_shape=jax.ShapeDtypeStruct((n, c), F32),
        compiler_params=_params(("parallel",)),
    )(a, w, x)


def _final_norm_kernel(x_ref, g_ref, o_ref):
    x = x_ref[...]
    o_ref[...] = x * lax.rsqrt(jnp.mean(x * x, axis=-1, keepdims=True) + RMS_EPS) * g_ref[...]


def final_norm(x, g):
    n, d = x.shape
    tm = _tile(n)
    return pl.pallas_call(
        _final_norm_kernel,
        grid=(n // tm,),
        in_specs=[pl.BlockSpec((tm, d), lambda i: (i, 0)), _full((1, d))],
        out_specs=pl.BlockSpec((tm, d), lambda i: (i, 0)),
        out_shape=jax.ShapeDtypeStruct((n, d), F32),
        compiler_params=_params(("parallel",)),
    )(x, g.reshape(1, d))


def _rotary_tables(pos):
    half = ROT_DIM // 2
    inv = ROPE_THETA ** (-2.0 * jnp.arange(half, dtype=F32) / ROT_DIM)
    ang = pos[:, None] * inv[None, :]
    cos, sin = jnp.cos(ang), jnp.sin(ang)
    n = pos.shape[0]
    one = jnp.ones((n, HEAD_DIM - ROT_DIM), F32)
    zero = jnp.zeros((n, HEAD_DIM - ROT_DIM), F32)
    zh = jnp.zeros((n, half), F32)
    c = jnp.concatenate([cos, cos, one], axis=1)
    s_up = jnp.concatenate([-sin, zh, zero], axis=1)
    s_dn = jnp.concatenate([zh, sin, zero], axis=1)
    return tuple(jnp.concatenate([t, t], axis=1) for t in (c, s_up, s_dn))


def _rotary_kernel(pa_ref, c_ref, su_ref, sd_ref, q_ref, k_ref):
    c, su, sd = c_ref[...], su_ref[...], sd_ref[...]
    half = ROT_DIM // 2
    for j in range((Q_W + KV_W) // LANES):
        x = pa_ref[:, j * LANES:(j + 1) * LANES]
        y = x * c + pltpu.roll(x, LANES - half, 1) * su + pltpu.roll(x, half, 1) * sd
        if j < Q_W // LANES:
            q_ref[:, j * LANES:(j + 1) * LANES] = y
        else:
            k_ref[...] = y


def rotary_qk(pa, tabs):
    n = pa.shape[0]
    tm = _tile(n)
    row = lambda w: pl.BlockSpec((tm, w), lambda i: (i, 0))
    return pl.pallas_call(
        _rotary_kernel,
        grid=(n // tm,),
        in_specs=[row(A_COLS), row(LANES), row(LANES), row(LANES)],
        out_specs=[row(Q_W), row(KV_W)],
        out_shape=[jax.ShapeDtypeStruct((n, Q_W), F32), jax.ShapeDtypeStruct((n, KV_W), F32)],
        compiler_params=_params(("parallel",)),
    )(pa, *tabs)


def _sink_softmax_pv(parts, sink):
    m = sink
    for s, _ in parts:
        m = jnp.maximum(m, jnp.max(s, axis=-1, keepdims=True))
    ps = [jnp.exp(s - m) for s, _ in parts]
    den = jnp.exp(sink - m)
    for p in ps:
        den = den + jnp.sum(p, axis=-1, keepdims=True)
    inv = 1.0 / den
    out = None
    for p, (_, v) in zip(ps, parts):
        o = jnp.dot((p * inv).astype(BF16), v, preferred_element_type=F32)
        out = o if out is None else out + o
    return out


def _swa_prompt_kernel(sink_ref, q_ref, kc_ref, kp_ref, vc_ref, vp_ref, o_ref):
    n = pl.program_id(1)
    qi = lax.broadcasted_iota(jnp.int32, (WINDOW, WINDOW), 0)
    kj = lax.broadcasted_iota(jnp.int32, (WINDOW, WINDOW), 1)
    mask_c = kj <= qi
    mask_p = kj > qi + jnp.where(n > 0, 0, WINDOW)
    scale = HEAD_DIM ** -0.5
    lane = lax.broadcasted_iota(jnp.int32, (WINDOW, KV_W), 1)
    low = lane < HEAD_DIM

    def halves(x, g):
        other = pltpu.roll(x, HEAD_DIM, 1)
        in_low, in_high = (x, other) if g == 0 else (other, x)
        return jnp.where(low, in_low, 0.0).astype(BF16), jnp.where(low, 0.0, in_high).astype(BF16)

    for g in range(ATT_KV_HEADS):
        kc, kp = halves(kc_ref[...], g), halves(kp_ref[...], g)
        vc, vp = halves(vc_ref[...], g), halves(vp_ref[...], g)
        for pair in range(ATT_GROUP // 2):
            col = (g * (ATT_GROUP // 2) + pair) * KV_W
            q2 = q_ref[:, col:col + KV_W].astype(BF16)
            out = None
            for odd in range(2):
                sc = lax.dot_general(q2, kc[odd], NT_DIMS, preferred_element_type=F32) * scale
                sp = lax.dot_general(q2, kp[odd], NT_DIMS, preferred_element_type=F32) * scale
                sc = jnp.where(mask_c, sc, NEG)
                sp = jnp.where(mask_p, sp, NEG)
                o = _sink_softmax_pv([(sp, vp[odd]), (sc, vc[odd])], sink_ref[col // HEAD_DIM + odd])
                out = o if out is None else out + o
            o_ref[:, col:col + KV_W] = out


def swa_prompt(q, k, pa, sinks, b, t):
    nb = t // WINDOW
    cur = lambda w, c: pl.BlockSpec((WINDOW, w), lambda bi, ni: (bi * nb + ni, c))
    prev = lambda w, c: pl.BlockSpec((WINDOW, w), lambda bi, ni: (bi * nb + jnp.maximum(ni - 1, 0), c))
    vcol = (Q_W + KV_W) // KV_W
    return pl.pallas_call(
        _swa_prompt_kernel,
        grid=(b, nb),
        in_specs=[pl.BlockSpec(memory_space=pltpu.SMEM), cur(Q_W, 0), cur(KV_W, 0), prev(KV_W, 0),
                  cur(KV_W, vcol), prev(KV_W, vcol)],
        out_specs=cur(Q_W, 0),
        out_shape=jax.ShapeDtypeStruct((b * t, Q_W), F32),
        compiler_params=_params(("parallel", "parallel")),
    )(sinks, q, k, k, pa, pa)


def _swa_sample_kernel(sink_ref, q_ref, kb_ref, vb_ref, kn_ref, vn_ref, o_ref, *, bt, dt, wb):
    rows = ATT_GROUP * dt
    tb = lax.rem(lax.broadcasted_iota(jnp.int32, (rows, wb), 0), dt)
    jb = lax.broadcasted_iota(jnp.int32, (rows, wb), 1)
    mask_b = jb > tb + (wb - WINDOW)
    npad = kn_ref.shape[1]
    tn = lax.rem(lax.broadcasted_iota(jnp.int32, (rows, npad), 0), dt)
    jn = lax.broadcasted_iota(jnp.int32, (rows, npad), 1)
    mask_n = jn <= tn
    scale = HEAD_DIM ** -0.5
    for b in range(bt):
        for g in range(ATT_KV_HEADS):
            gs = slice(g * HEAD_DIM, (g + 1) * HEAD_DIM)
            q = q_ref[b, g].astype(BF16)
            sb = lax.dot_general(q, kb_ref[b, :, g, :].astype(BF16), NT_DIMS, preferred_element_type=F32) * scale
            sn = lax.dot_general(q, kn_ref[b, :, gs].astype(BF16), NT_DIMS, preferred_element_type=F32) * scale
            sb = jnp.where(mask_b, sb, NEG)
            sn = jnp.where(mask_n, sn, NEG)
            o_ref[b, g] = _sink_softmax_pv(
                [(sb, vb_ref[b, :, g, :].astype(BF16)), (sn, vn_ref[b, :, gs].astype(BF16))], sink_ref[g][:, :1])


def swa_sample(qs, cache_k, cache_v, layer, knew, vnew, sink_rows):
    db, _, rows, _ = qs.shape
    dt = rows // ATT_GROUP
    wb = cache_k.shape[2]
    bt = _tile(db, (8, 4, 2, 1))
    blk = lambda a: pl.BlockSpec((bt,) + a.shape[1:], lambda i: (i,) + (0,) * (a.ndim - 1))
    cache = pl.BlockSpec((None, bt) + cache_k.shape[2:], lambda i: (layer, i, 0, 0, 0))
    return pl.pallas_call(
        functools.partial(_swa_sample_kernel, bt=bt, dt=dt, wb=wb),
        grid=(db // bt,),
        in_specs=[_full(sink_rows.shape), blk(qs), cache, cache, blk(knew), blk(vnew)],
        out_specs=blk(qs),
        out_shape=jax.ShapeDtypeStruct(qs.shape, F32),
        compiler_params=_params(("parallel",)),
    )(sink_rows, qs, cache_k, cache_v, knew, vnew)


CONV_HALO = 32


def _ln_swish(y, lg, lb):
    mu = jnp.mean(y, axis=-1, keepdims=True)
    var = jnp.mean(jnp.square(y - mu), axis=-1, keepdims=True)
    yn = (y - mu) * lax.rsqrt(var + LN_EPS) * lg + lb
    return yn * _sigmoid(yn)


def _conv_prompt_kernel(pb_ref, dw_ref, db_ref, lg_ref, lb_ref, y_ref, st_ref, ext_ref, *, tt):
    i = pl.program_id(1)

    @pl.when(i == 0)
    def _():
        ext_ref[0:CONV_HALO, :] = jnp.zeros((CONV_HALO, CONV_CH), F32)

    @pl.when(i > 0)
    def _():
        ext_ref[0:CONV_HALO, :] = ext_ref[tt:tt + CONV_HALO, :]

    pb = pb_ref[...]
    ext_ref[CONV_HALO:CONV_HALO + tt, :] = pb[:, :CONV_CH] * _sigmoid(pb[:, CONV_CH:])
    acc = jnp.zeros((tt, CONV_CH), F32) + db_ref[...]
    first = CONV_HALO - (CONV_WIDTH - 1)
    for j in range(CONV_WIDTH):
        acc = acc + ext_ref[pl.ds(first + j, tt), :] * dw_ref[j:j + 1, :]
    y_ref[...] = _ln_swish(acc, lg_ref[...], lb_ref[...])

    @pl.when(i == pl.num_programs(1) - 1)
    def _():
        st_ref[0] = ext_ref[tt:tt + CONV_HALO, :]


def conv_prompt(pb, dw, db, lg, lb, b, t):
    tt = _tile(t)
    nt = t // tt
    vec = lambda a: a.reshape(1, CONV_CH)
    return pl.pallas_call(
        functools.partial(_conv_prompt_kernel, tt=tt),
        grid=(b, nt),
        in_specs=[pl.BlockSpec((tt, B_COLS), lambda bi, i: (bi * nt + i, 0)), _full((CONV_WIDTH, CONV_CH)),
                  _full((1, CONV_CH)), _full((1, CONV_CH)), _full((1, CONV_CH))],
        out_specs=[pl.BlockSpec((tt, CONV_CH), lambda bi, i: (bi * nt + i, 0)),
                   pl.BlockSpec((1, CONV_HALO, CONV_CH), lambda bi, i: (bi, 0, 0))],
        out_shape=[jax.ShapeDtypeStruct((b * t, CONV_CH), F32), jax.ShapeDtypeStruct((b, CONV_HALO, CONV_CH), F32)],
        scratch_shapes=[pltpu.VMEM((tt + CONV_HALO, CONV_CH), F32)],
        compiler_params=_params(("arbitrary", "arbitrary")),
    )(pb, dw, vec(db), vec(lg), vec(lb))


def _conv_sample_kernel(c_ref, pb_ref, dw_ref, db_ref, lg_ref, lb_ref, y_ref, nc_ref, *, dt):
    nprev = CONV_WIDTH - 1
    us = []
    for t in range(dt):
        pb = pb_ref[t]
        us.append(pb[:, :CONV_CH] * _sigmoid(pb[:, CONV_CH:]))

    def ext(j):
        return c_ref[j] if j < nprev else us[j - nprev]

    for t in range(dt):
        acc = db_ref[...] + ext(t) * dw_ref[0:1, :]
        for j in range(1, CONV_WIDTH):
            acc = acc + ext(t + j) * dw_ref[j:j + 1, :]
        y_ref[t] = _ln_swish(acc, lg_ref[...], lb_ref[...])
    for j in range(nprev):
        nc_ref[j] = ext(j + dt)


def conv_sample(cache_t, pb_t, dw, db, lg, lb):
    nprev, dbt, _ = cache_t.shape
    dt = pb_t.shape[0]
    bt = _tile(dbt, (32, 16, 8))
    vec = lambda a: a.reshape(1, CONV_CH)
    blk = lambda lead, w: pl.BlockSpec((lead, bt, w), lambda i: (0, i, 0))
    return pl.pallas_call(
        functools.partial(_conv_sample_kernel, dt=dt),
        grid=(dbt // bt,),
        in_specs=[blk(nprev, CONV_CH), blk(dt, B_COLS), _full((CONV_WIDTH, CONV_CH)), _full((1, CONV_CH)),
                  _full((1, CONV_CH)), _full((1, CONV_CH))],
        out_specs=[blk(dt, CONV_CH), blk(nprev, CONV_CH)],
        out_shape=[jax.ShapeDtypeStruct((dt, dbt, CONV_CH), F32), jax.ShapeDtypeStruct((nprev, dbt, CONV_CH), F32)],
        compiler_params=_params(("parallel",)),
    )(cache_t, pb_t, dw, vec(db), vec(lg), vec(lb))


def _head_sum_matrix():
    h = np.arange(RWKV_W) // RWKV_HEAD
    return jnp.asarray((h[:, None] == h[None, :]).astype(np.float32), dtype=BF16)


def _head_sum(x, hs_ref):
    hi = x.astype(BF16)
    lo = (x - hi.astype(F32)).astype(BF16)
    hs = hs_ref[...]
    return jnp.dot(hi, hs, preferred_element_type=F32) + jnp.dot(lo, hs, preferred_element_type=F32)


def _rwkv_pre_math(pc, prev, c):
    mu_ref, w0_ref, w2_ref, a0_ref, a2_ref, g2_ref, kkp_ref, ka_ref, rk_ref, hs_ref = c
    xs = pc + (prev - pc) * mu_ref[...]
    o1, o2, o3 = RWKV_W, 2 * RWKV_W, 3 * RWKV_W
    o4 = o3 + DECAY_LORA
    o5 = o4 + ICLR_LORA
    r, k, v = xs[:, :o1], xs[:, o1:o2], xs[:, o2:o3]
    wd, ad, gd = xs[:, o3:o4], xs[:, o4:o5], xs[:, o5:]
    y = -(w0_ref[...] + jnp.dot(jnp.tanh(wd).astype(BF16), w2_ref[...], preferred_element_type=F32))
    softplus = jnp.maximum(y, 0.0) + jnp.log(1.0 + jnp.exp(-jnp.abs(y)))
    decay = jnp.exp(-jnp.exp(-softplus - 0.5))
    a = _sigmoid(a0_ref[...] + jnp.dot(ad.astype(BF16), a2_ref[...], preferred_element_type=F32))
    g = jnp.dot(_sigmoid(gd).astype(BF16), g2_ref[...], preferred_element_type=F32)
    kk = k * kkp_ref[...]
    norm = jnp.sqrt(_head_sum(kk * kk, hs_ref))
    kk = kk / jnp.maximum(norm, 1e-12)
    k_mod = k * (1.0 + (a - 1.0) * ka_ref[...])
    bonus = _head_sum(r * k_mod * rk_ref[...], hs_ref) * v
    return r, decay, k_mod, -kk, kk * a, v, g, bonus


RWKV_PRE_OUTS = 8
SCAN_KEY_OPERANDS = 5
SCAN_CHUNK = 256
SHIFT_HALO = SUBLANES


def _rwkv_pre_prompt_kernel(pc_ref, *refs, tt, skew):
    consts, outs, ext_ref = refs[:10], refs[10:10 + RWKV_PRE_OUTS], refs[-1]
    i = pl.program_id(1)

    @pl.when(i == 0)
    def _():
        ext_ref[0:SHIFT_HALO, :] = jnp.zeros((SHIFT_HALO, C_COLS), F32)

    @pl.when(i > 0)
    def _():
        ext_ref[0:SHIFT_HALO, :] = ext_ref[tt:tt + SHIFT_HALO, :]

    pc = pc_ref[...]
    ext_ref[SHIFT_HALO:SHIFT_HALO + tt, :] = pc
    res = _rwkv_pre_math(pc, ext_ref[pl.ds(SHIFT_HALO - 1, tt), :], consts)
    for j, (o_ref, val) in enumerate(zip(outs, res)):
        if j < SCAN_KEY_OPERANDS and skew:
            vt = val.T
            for h in range(RWKV_HEADS):
                g = pl.program_id(0) * RWKV_HEADS + h
                rows = slice(h * RWKV_HEAD, (h + 1) * RWKV_HEAD)
                o_ref[rows, :] = pltpu.roll(vt[rows, :], g * SCAN_TSUB, 1)
        else:
            o_ref[...] = val


def _rwkv_pre_sample_kernel(pc_ref, shift_ref, *refs, dt):
    consts, outs = refs[:10], refs[10:10 + RWKV_PRE_OUTS]
    for t in range(dt):
        res = _rwkv_pre_math(pc_ref[t], shift_ref[...] if t == 0 else pc_ref[t - 1], consts)
        for o_ref, val in zip(outs, res):
            o_ref[t] = val


def _rwkv_pre_consts(lp):
    vec = lambda a: a.reshape(1, -1)
    return [vec(lp['rk_mu']), vec(lp['rk_w0']), lp['rk_w2'].astype(BF16), vec(lp['rk_a0']), lp['rk_a2'].astype(BF16),
            lp['rk_g2'].astype(BF16), vec(lp['rk_kk']), vec(lp['rk_ka']), vec(lp['rk_rk']), _head_sum_matrix()]


def scan_skewed(b, t):
    return b * RWKV_HEADS == SCAN_PAIRS and t % SCAN_CHUNK == 0


def rwkv_pre_prompt(pc, lp, b, t):
    skew = scan_skewed(b, t)
    tt = SCAN_CHUNK if skew else _tile(t, (256, 128))
    nt = t // tt
    consts = _rwkv_pre_consts(lp)
    row = lambda w: pl.BlockSpec((tt, w), lambda bi, i: (bi * nt + i, 0))
    nkey = SCAN_KEY_OPERANDS if skew else 0
    col = pl.BlockSpec((RWKV_W, tt), lambda bi, i: (bi, i))
    return pl.pallas_call(
        functools.partial(_rwkv_pre_prompt_kernel, tt=tt, skew=skew),
        grid=(b, nt),
        in_specs=[row(C_COLS)] + [_full(c.shape) for c in consts],
        out_specs=[col] * nkey + [row(RWKV_W)] * (RWKV_PRE_OUTS - nkey),
        out_shape=[jax.ShapeDtypeStruct((b * RWKV_W, t), F32)] * nkey
        + [jax.ShapeDtypeStruct((b * t, RWKV_W), F32)] * (RWKV_PRE_OUTS - nkey),
        scratch_shapes=[pltpu.VMEM((tt + SHIFT_HALO, C_COLS), F32)],
        compiler_params=_params(("arbitrary", "arbitrary")),
    )(pc, *consts)


def rwkv_pre_sample(pc_t, shift, lp):
    dt, dbt, _ = pc_t.shape
    bt = _tile(dbt, (64, 32, 16, 8))
    consts = _rwkv_pre_consts(lp)
    blk = lambda w: pl.BlockSpec((dt, bt, w), lambda i: (0, i, 0))
    return pl.pallas_call(
        functools.partial(_rwkv_pre_sample_kernel, dt=dt),
        grid=(dbt // bt,),
        in_specs=[blk(C_COLS), pl.BlockSpec((bt, C_COLS), lambda i: (i, 0))] + [_full(c.shape) for c in consts],
        out_specs=[blk(RWKV_W)] * RWKV_PRE_OUTS,
        out_shape=[jax.ShapeDtypeStruct((dt, dbt, RWKV_W), F32)] * RWKV_PRE_OUTS,
        compiler_params=_params(("parallel",)),
    )(pc_t, shift, *consts)


def _scan_select_matrices(pair_major):
    e = np.zeros((SCAN_TSUB, LANES, LANES), np.float32)
    vl = LANES // SCAN_PAIRS
    for tl in range(SCAN_TSUB):
        for g in range(SCAN_PAIRS):
            src = g * SCAN_TSUB + tl if pair_major else tl * SCAN_PAIRS + g
            e[tl, src, g * vl:(g + 1) * vl] = 1.0
    return jnp.asarray(np.concatenate([e, e, e], axis=1), dtype=BF16)


def _sublane_allsum(x):
    x = x + pltpu.roll(x, 4, 0)
    x = x + pltpu.roll(x, 2, 0)
    return x + pltpu.roll(x, 1, 0)


def _scan_kernel(xr_ref, xw_ref, xk_ref, xa_ref, xb_ref, v_ref, s0_ref, e_ref, o_ref, sf_ref, z_ref, tile_ref, xs_ref,
                 *, groups, steps, skewed):
    x_refs = (xr_ref, xw_ref, xk_ref, xa_ref, xb_ref)
    c = pl.program_id(1)
    nvh = z_ref.shape[0]
    nkb = RWKV_HEAD // SUBLANES
    sub = lax.broadcasted_iota(jnp.int32, (SUBLANES, LANES), 0)
    R_ROW, W_ROW, K_ROW, A_ROW, B_ROW = (j * RWKV_HEAD for j in range(5))

    @pl.when(c == 0)
    def _():
        z_ref[...] = s0_ref[0]

    lane_group = lax.broadcasted_iota(jnp.int32, (SUBLANES, LANES), 1) // SCAN_TSUB

    def gather_group(x_ref, th, kb):
        acc = jnp.zeros((SUBLANES, LANES), F32)
        for g in range(SCAN_PAIRS):
            pos = th + g
            tile = pl.multiple_of(((pos // SCAN_PAIRS) % 2) * LANES, LANES)
            rows = slice(g * RWKV_HEAD + kb * SUBLANES, g * RWKV_HEAD + (kb + 1) * SUBLANES)
            acc = jnp.where(lane_group == pos % SCAN_PAIRS, x_ref[rows, pl.ds(tile, LANES)], acc)
        return pltpu.roll(acc, ((SCAN_PAIRS - th % SCAN_PAIRS) % SCAN_PAIRS) * SCAN_TSUB, 1)

    def split_group(th):
        th = jnp.asarray(th, jnp.int32)
        for j, x_ref in enumerate(x_refs):
            rows = slice(j * RWKV_HEAD, (j + 1) * RWKV_HEAD)
            if skewed:
                xg = jnp.concatenate([gather_group(x_ref, th, kb) for kb in range(nkb)], axis=0)
            else:
                xg = x_ref[0, th]
            hi = xg.astype(BF16)
            r1 = xg - hi.astype(F32)
            mid = r1.astype(BF16)
            lo = (r1 - mid.astype(F32)).astype(BF16)
            xs_ref[rows, 0:LANES] = hi
            xs_ref[rows, LANES:2 * LANES] = mid
            xs_ref[rows, 2 * LANES:3 * LANES] = lo

    def spread(tl, buf):
        tile_ref[buf] = jnp.dot(xs_ref[...], e_ref[tl], preferred_element_type=F32)

    split_group(0)
    spread(0, 0)

    def body(th, carry):
        for tl in range(steps):
            cur = tl % 2
            t = th * steps + tl
            vrow = v_ref[0, t]
            blk = lambda row0, kb: tile_ref[cur, row0 + kb * SUBLANES:row0 + (kb + 1) * SUBLANES, :]
            sa = []
            for vh in range(nvh):
                acc = None
                for kb in range(nkb):
                    p = z_ref[vh, kb * SUBLANES:(kb + 1) * SUBLANES, :] * blk(A_ROW, kb)
                    acc = p if kb == 0 else acc + p
                sa.append(_sublane_allsum(acc))
            if tl + 1 < steps:
                spread(tl + 1, 1 - cur)
            else:
                split_group(jnp.minimum(th + 1, groups - 1))
                spread(0, 1 - cur)
            orow = jnp.zeros((SUBLANES, LANES), F32)
            for vh in range(nvh):
                vb = jnp.broadcast_to(vrow[vh:vh + 1], (SUBLANES, LANES))
                acc = None
                for kb in range(nkb):
                    ks = slice(kb * SUBLANES, (kb + 1) * SUBLANES)
                    zn = z_ref[vh, ks, :] * blk(W_ROW, kb) + blk(B_ROW, kb) * sa[vh] + blk(K_ROW, kb) * vb
                    z_ref[vh, ks, :] = zn
                    p = zn * blk(R_ROW, kb)
                    acc = p if kb == 0 else acc + p
                orow = jnp.where(sub == vh, _sublane_allsum(acc), orow)
            o_ref[0, t] = orow
        return carry

    lax.fori_loop(0, groups, body, 0)

    @pl.when(c == pl.num_programs(1) - 1)
    def _():
        sf_ref[0] = z_ref[...]


def rwkv_scan(x5, v, s0, nseq, t, skewed=False):
    spb = SCAN_PAIRS // RWKV_HEADS
    assert nseq % spb == 0
    nblk = nseq // spb
    vl = LANES // SCAN_PAIRS
    vh = RWKV_HEAD // vl
    steps = min(SCAN_TSUB, t)
    assert t % steps == 0 and steps % 2 == 0
    t8 = t // steps
    def key_layout(a):
        a = a.reshape(nblk, spb, t8, steps, RWKV_HEADS, RWKV_HEAD)
        if steps < SCAN_TSUB:
            a = jnp.pad(a, ((0, 0),) * 3 + ((0, SCAN_TSUB - steps),) + ((0, 0),) * 2)
        return a.transpose(0, 2, 5, 3, 1, 4).reshape(nblk, t8, RWKV_HEAD, LANES)

    vk = v.reshape(nblk, spb, t, RWKV_HEADS, vh, vl).transpose(0, 2, 4, 1, 3, 5).reshape(nblk, t, vh, LANES)
    sk = s0.reshape(nblk, spb, RWKV_HEADS, vh, vl, RWKV_HEAD).transpose(0, 3, 5, 1, 2, 4).reshape(
        nblk, vh, RWKV_HEAD, LANES)
    if skewed:
        assert steps == SCAN_TSUB and t % SCAN_CHUNK == 0
        xk = x5
        groups = SCAN_CHUNK // steps
        key_spec = pl.BlockSpec((SCAN_PAIRS * RWKV_HEAD, SCAN_CHUNK), lambda i, c: (i, c))
    else:
        xk = [key_layout(a) for a in x5]
        groups = _tile(t8, (32, 16, 8, 4, 2, 1))
        key_spec = pl.BlockSpec((1, groups, RWKV_HEAD, LANES), lambda i, c: (i, c, 0, 0))
    nchunk = t8 // groups
    ch = groups * steps
    o, sf = pl.pallas_call(
        functools.partial(_scan_kernel, groups=groups, steps=steps, skewed=skewed),
        grid=(nblk, nchunk),
        in_specs=[key_spec] * 5 + [
                  pl.BlockSpec((1, ch, vh, LANES), lambda i, c: (i, c, 0, 0)),
                  pl.BlockSpec((1, vh, RWKV_HEAD, LANES), lambda i, c: (i, 0, 0, 0)),
                  _full((SCAN_TSUB, 3 * LANES, LANES))],
        out_specs=[pl.BlockSpec((1, ch, vh, LANES), lambda i, c: (i, c, 0, 0)),
                   pl.BlockSpec((1, vh, RWKV_HEAD, LANES), lambda i, c: (i, 0, 0, 0))],
        out_shape=[jax.ShapeDtypeStruct((nblk, t, vh, LANES), F32),
                   jax.ShapeDtypeStruct((nblk, vh, RWKV_HEAD, LANES), F32)],
        scratch_shapes=[pltpu.VMEM((vh, RWKV_HEAD, LANES), F32), pltpu.VMEM((2, 5 * RWKV_HEAD, LANES), F32),
                        pltpu.VMEM((5 * RWKV_HEAD, 3 * LANES), BF16)],
        compiler_params=_params(("arbitrary", "arbitrary")),
    )(*xk, vk, sk, _scan_select_matrices(pair_major=skewed))
    o = o.reshape(nblk, t, vh, spb, RWKV_HEADS, vl).transpose(0, 3, 1, 4, 2, 5).reshape(nseq * t, RWKV_W)
    sf = sf.reshape(nblk, vh, RWKV_HEAD, spb, RWKV_HEADS, vl).transpose(0, 3, 4, 1, 5, 2).reshape(
        nseq, RWKV_HEADS, RWKV_HEAD, RWKV_HEAD)
    return o, sf


def _merge_kernel(x_ref, oa_ref, yb_ref, oc_ref, bonus_ref, g_ref, pg_ref, wa_ref, wb_ref, bb_ref, wc_ref, wo_ref,
                  gng_ref, gnb_ref, hs_ref, o_ref):
    oc = oc_ref[...]
    mu = _head_sum(oc, hs_ref) * (1.0 / RWKV_HEAD)
    dev = oc - mu
    var = _head_sum(dev * dev, hs_ref) * (1.0 / RWKV_HEAD)
    on = dev * lax.rsqrt(var + GN_EPS) * gng_ref[...] + gnb_ref[...]
    yc = (on + bonus_ref[...]) * g_ref[...]
    d = D_MODEL
    dot = lambda a, w: jnp.dot(a.astype(BF16), w[...], preferred_element_type=F32)
    merged = (_sigmoid(pg_ref[:, 0:d]) * dot(oa_ref[...], wa_ref)
              + _sigmoid(pg_ref[:, d:2 * d]) * (dot(yb_ref[...], wb_ref) + bb_ref[...])
              + _sigmoid(pg_ref[:, 2 * d:3 * d]) * dot(yc, wc_ref))
    o_ref[...] = x_ref[...] + dot(merged, wo_ref)


def merge_out(x, oa, yb, oc, bonus, g, pg, lp):
    n = x.shape[0]
    tm = _tile(n, (256, 128, 64, 32, 16, 8))
    row = lambda w: pl.BlockSpec((tm, w), lambda i: (i, 0))
    vec = lambda a: a.reshape(1, -1)
    consts = [lp['w_a_out'].astype(BF16), lp['w_b_out'].astype(BF16), vec(lp['b_b_out']), lp['w_c_out'].astype(BF16),
              lp['w_o'].astype(BF16), vec(lp['rk_gn_g']), vec(lp['rk_gn_b']), _head_sum_matrix()]
    return pl.pallas_call(
        _merge_kernel,
        grid=(n // tm,),
        in_specs=[row(D_MODEL), row(Q_W), row(CONV_CH), row(RWKV_W), row(RWKV_W), row(RWKV_W), row(G_COLS)]
        + [_full(c.shape) for c in consts],
        out_specs=row(D_MODEL),
        out_shape=jax.ShapeDtypeStruct((n, D_MODEL), F32),
        compiler_params=_params(("parallel",)),
    )(x, oa, yb, oc, bonus, g, pg, *consts)


def _cross_attn_kernel(q_ref, k_ref, v_ref, o_ref):
    scale = CA_HEAD_DIM ** -0.5
    outs = []
    for h in range(CA_HEADS):
        hs = slice(h * CA_HEAD_DIM, (h + 1) * CA_HEAD_DIM)
        q = q_ref[0, :, hs].astype(BF16)
        s = lax.dot_general(q, k_ref[0, :, hs].astype(BF16), NT_DIMS, preferred_element_type=F32) * scale
        m = jnp.max(s, axis=-1, keepdims=True)
        p = jnp.exp(s - m)
        p = p / jnp.sum(p, axis=-1, keepdims=True)
        outs.append(jnp.dot(p.astype(BF16), v_ref[0, :, hs].astype(BF16), preferred_element_type=F32))
    o_ref[0] = jnp.concatenate(outs, axis=1)


def cross_attn(q, mk, mv, tiles_per_seq, first_seq=0):
    nt, tq, d = q.shape
    m = mk.shape[1]
    kv = pl.BlockSpec((1, m, d), lambda i: (first_seq + i // tiles_per_seq, 0, 0))
    qs = pl.BlockSpec((1, tq, d), lambda i: (i, 0, 0))
    return pl.pallas_call(
        _cross_attn_kernel,
        grid=(nt,),
        in_specs=[qs, kv, kv],
        out_specs=qs,
        out_shape=jax.ShapeDtypeStruct(q.shape, F32),
        compiler_params=_params(("parallel",)),
    )(q, mk, mv)


def _staircase():
    return [(i, PEER_TOPK // (i + 1)) for i in range(PEER_TOPK)]


PEER_CAND = sum(nj for _, nj in _staircase())
PEER_CAND_PAD = -(-PEER_CAND // SUBLANES) * SUBLANES


def _extract_topk(cur, out_ref, base, with_rank=False):
    rank = jnp.full(cur.shape, float(PEER_TOPK), F32) if with_rank else None
    for k in range(PEER_TOPK):
        m = jnp.max(cur, axis=0, keepdims=True)
        out_ref[base + k:base + k + 1, :] = m
        hit = cur == m
        if with_rank:
            rank = jnp.where(hit, float(k), rank)
        if k + 1 < PEER_TOPK:
            cur = jnp.where(hit, -jnp.inf, cur)
    return rank


def _peer_select_kernel(qp_ref, keys_ref, r1_ref, rho_ref, m2_ref, e2_ref, s_ref, tv_ref, cand_ref, best_ref):
    nsub = 2 * PEER_HEADS
    for hc in range(nsub):
        q = qp_ref[:, hc * PEER_HALF:(hc + 1) * PEER_HALF].astype(BF16)
        st = lax.dot_general(keys_ref[hc], q, NT_DIMS, preferred_element_type=F32)
        s_ref[hc] = st
        rank = _extract_topk(st, tv_ref, hc * PEER_TOPK, with_rank=hc % 2 == 0)
        if hc % 2 == 0:
            r1_ref[hc // 2] = rank
    tm = qp_ref.shape[0]
    for h in range(PEER_HEADS):
        b1 = 2 * h * PEER_TOPK
        b2 = b1 + PEER_TOPK
        cand_ref[PEER_CAND_PAD - SUBLANES:PEER_CAND_PAD, :] = jnp.full((SUBLANES, tm), -jnp.inf, F32)
        off = 0
        for i, nj in _staircase():
            cand_ref[off:off + nj, :] = tv_ref[b1 + i:b1 + i + 1, :] + tv_ref[b2:b2 + nj, :]
            off += nj
        _extract_topk(cand_ref[...], best_ref, 0)
        best = best_ref[...]
        z = jnp.sum(jnp.exp(best - best[0:1, :]), axis=0, keepdims=True)
        thr = best[PEER_TOPK - 1:PEER_TOPK, :]
        s2 = s_ref[2 * h + 1]
        m2 = jnp.zeros(s2.shape, F32)
        for i in range(PEER_TOPK):
            m2 = jnp.where((tv_ref[b1 + i:b1 + i + 1, :] + s2) >= thr, float(i + 1), m2)
        m2_ref[h] = m2
        e2_ref[h] = jnp.exp(s2 - tv_ref[b2:b2 + 1, :])
        rho_ref[h] = jnp.exp(s_ref[2 * h] - tv_ref[b1:b1 + 1, :]) * (1.0 / z)


def peer_select(qp, keys):
    n = qp.shape[0]
    tm = _tile(n, (256, 128))
    nsub = 2 * PEER_HEADS
    blk = pl.BlockSpec((PEER_HEADS, N_KEYS, tm), lambda i: (0, 0, i))
    shp = lambda dt: jax.ShapeDtypeStruct((PEER_HEADS, N_KEYS, n), dt)
    return pl.pallas_call(
        _peer_select_kernel,
        grid=(n // tm,),
        in_specs=[pl.BlockSpec((tm, nsub * PEER_HALF), lambda i: (i, 0)), _full(keys.shape)],
        out_specs=[blk, blk, blk, blk],
        out_shape=[shp(F32)] * 4,
        scratch_shapes=[pltpu.VMEM((nsub, N_KEYS, tm), F32), pltpu.VMEM((nsub * PEER_TOPK, tm), F32),
                        pltpu.VMEM((PEER_CAND_PAD, tm), F32), pltpu.VMEM((PEER_TOPK, tm), F32)],
        compiler_params=_params(("parallel",)),
    )(qp, keys)


PEER_PACK_ROWS = 16
PEER_EXPERT_TILE = 8 * N_KEYS


def _gelu_tanh(x):
    return 0.5 * x * (1.0 + jnp.tanh(0.7978845608028654 * (x + 0.044715 * (x * x * x))))


def _peer_dense_kernel(x_ref, g_ref, r1_ref, rho_ref, m2_ref, e2_ref, u_ref, vt_ref, o_ref, xnt_ref, acc_ref, ht_ref,
                       wh_ref, m2b_ref, e2b_ref, rows_ref, *, te):
    e = pl.program_id(1)
    tm = x_ref.shape[0]
    pk = PEER_PACK_ROWS

    @pl.when(e == 0)
    def _():
        x = x_ref[...]
        xn = x * lax.rsqrt(jnp.mean(x * x, axis=-1, keepdims=True) + RMS_EPS) * g_ref[...]
        xnt_ref[...] = xn.T.astype(BF16)
        acc_ref[...] = jnp.zeros_like(acc_ref)
        m2b_ref[...] = m2_ref[...].astype(BF16)
        e2b_ref[...] = e2_ref[...].astype(BF16)

    n_a = te // N_KEYS
    for al in range(n_a):
        a = e * n_a + al
        for h in range(PEER_HEADS):
            row = 2 * (al * PEER_HEADS + h)
            rows_ref[row:row + 1, :] = r1_ref[h, pl.ds(a, 1), :]
            rows_ref[row + 1:row + 2, :] = rho_ref[h, pl.ds(a, 1), :]

    ht_ref[...] = jnp.dot(u_ref[...], xnt_ref[...], preferred_element_type=F32)
    zero = jnp.zeros((), BF16)
    for al in range(n_a):
        for cl in range(tm // LANES):
            cs = slice(cl * LANES, (cl + 1) * LANES)
            packed = lambda row: jnp.broadcast_to(rows_ref[row:row + 1, cs], (pk, LANES)).astype(BF16)
            r1_rows = [packed(2 * (al * PEER_HEADS + h)) for h in range(PEER_HEADS)]
            rho_rows = [packed(2 * (al * PEER_HEADS + h) + 1) for h in range(PEER_HEADS)]
            for rb in range(N_KEYS // pk):
                rs = slice(rb * pk, (rb + 1) * pk)
                w = None
                for h in range(PEER_HEADS):
                    sel = r1_rows[h] < m2b_ref[h, rs, cs]
                    contrib = jnp.where(sel, e2b_ref[h, rs, cs] * rho_rows[h], zero)
                    w = contrib if w is None else w + contrib
                hr = slice(al * N_KEYS + rb * pk, al * N_KEYS + (rb + 1) * pk)
                wh_ref[hr, cs] = w * _gelu_tanh(ht_ref[hr, cs].astype(BF16))
    acc_ref[...] += jnp.dot(vt_ref[...], wh_ref[...], preferred_element_type=F32)

    @pl.when(e == pl.num_programs(1) - 1)
    def _():
        o_ref[...] = x_ref[...] + acc_ref[...].T


def peer_dense(x, g, sel, u, v_t, layer):
    n, d = x.shape
    ne = u.shape[1]
    tm = _tile(n, (512, 256, 128))
    te = PEER_EXPERT_TILE
    blk = pl.BlockSpec((PEER_HEADS, N_KEYS, tm), lambda i, e: (0, 0, i))
    return pl.pallas_call(
        functools.partial(_peer_dense_kernel, te=te),
        grid=(n // tm, ne // te),
        in_specs=[pl.BlockSpec((tm, d), lambda i, e: (i, 0)), _full((1, d)), blk, blk, blk, blk,
                  pl.BlockSpec((None, te, d), lambda i, e: (layer, e, 0)),
                  pl.BlockSpec((None, d, te), lambda i, e: (layer, 0, e))],
        out_specs=pl.BlockSpec((tm, d), lambda i, e: (i, 0)),
        out_shape=jax.ShapeDtypeStruct((n, d), F32),
        scratch_shapes=[pltpu.VMEM((d, tm), BF16), pltpu.VMEM((d, tm), F32), pltpu.VMEM((te, tm), F32),
                        pltpu.VMEM((te, tm), BF16), pltpu.VMEM((PEER_HEADS, N_KEYS, tm), BF16),
                        pltpu.VMEM((PEER_HEADS, N_KEYS, tm), BF16),
                        pltpu.VMEM((2 * PEER_HEADS * (te // N_KEYS), tm), F32)],
        compiler_params=_params(("parallel", "arbitrary")),
    )(x, g.reshape(1, d), *sel, u, v_t)


def kernel(x_prompt, x_sample, cache_win_k, cache_win_v, cache_conv, state_shift, state_wkv, cache_mem_k, cache_mem_v,
           mem_prompt, g_mix, w_in, att_sinks, w_a_out, conv_dw, conv_db, conv_ln_g, conv_ln_b, w_b_out, b_b_out,
           rk_mu, rk_w0, rk_w2, rk_a0, rk_a2, rk_g2, rk_kk, rk_ka, rk_rk, rk_gn_g, rk_gn_b, w_c_out, w_o,
           g_ca, g_mem, w_cq, w_mk, w_mv, w_co, g_ffn, w_pq, peer_keys, peer_u, peer_v, g_final):
    b, t, d = x_prompt.shape
    db, dt, _ = x_sample.shape
    depth = w_in.shape[0]
    wb = cache_win_k.shape[2]
    mem_len = mem_prompt.shape[1]
    n_p, n_s = b * t, db * dt
    assert t % WINDOW == 0 and wb == WINDOW and dt <= SUBLANES

    xp = x_prompt.reshape(n_p, d)
    xs = x_sample.reshape(n_s, d)
    tabs_p = _rotary_tables(jnp.tile(jnp.arange(t, dtype=F32), b))
    tabs_s = _rotary_tables(jnp.tile(PAST_LEN + jnp.arange(dt, dtype=F32), db))
    tq = _tile(t)
    new_pad = SUBLANES - dt
    u_bf = peer_u.astype(BF16)
    vt_bf = jnp.swapaxes(peer_v.astype(BF16), 1, 2)
    mem = mem_prompt.reshape(b * mem_len, d)
    mem_k_all = cache_mem_k.reshape(depth * db, mem_len, d)
    mem_v_all = cache_mem_v.reshape(depth * db, mem_len, d)
    time_major = lambda a: a.reshape(db, dt, a.shape[-1]).transpose(1, 0, 2)
    seq_major = lambda a: a.transpose(1, 0, 2).reshape(n_s, a.shape[-1])
    k4 = lambda a, nb_, tt_: a.reshape(nb_, tt_, ATT_KV_HEADS, HEAD_DIM)

    outs = [[] for _ in range(12)]
    for l in range(depth):
        lp = {'rk_mu': rk_mu[l], 'rk_w0': rk_w0[l], 'rk_w2': rk_w2[l], 'rk_a0': rk_a0[l], 'rk_a2': rk_a2[l],
              'rk_g2': rk_g2[l], 'rk_kk': rk_kk[l], 'rk_ka': rk_ka[l], 'rk_rk': rk_rk[l].reshape(-1),
              'rk_gn_g': rk_gn_g[l], 'rk_gn_b': rk_gn_b[l], 'w_a_out': w_a_out[l], 'w_b_out': w_b_out[l],
              'b_b_out': b_b_out[l], 'w_c_out': w_c_out[l], 'w_o': w_o[l]}
        win = w_in[l].astype(BF16)
        c0, c1, c2 = A_COLS, A_COLS + B_COLS, A_COLS + B_COLS + C_COLS
        w_seg = [win[:, :c0], win[:, c0:c1], win[:, c1:c2], win[:, c2:]]
        pa_p, pb_p, pc_p, pg_p = (norm_matmul(xp, g_mix[l], w) for w in w_seg)
        pa_s, pb_s, pc_s, pg_s = (norm_matmul(xs, g_mix[l], w) for w in w_seg)

        q_p, k_p = rotary_qk(pa_p, tabs_p)
        q_s, k_s = rotary_qk(pa_s, tabs_s)
        oa_p = swa_prompt(q_p, k_p, pa_p, att_sinks[l], b, t)
        qs4 = q_s.reshape(db, dt, ATT_KV_HEADS, ATT_GROUP, HEAD_DIM).transpose(0, 2, 3, 1, 4).reshape(
            db, ATT_KV_HEADS, ATT_GROUP * dt, HEAD_DIM)
        k_new = k_s.reshape(db, dt, KV_W)
        v_new = pa_s[:, Q_W + KV_W:].reshape(db, dt, KV_W)
        padn = lambda a: jnp.pad(a, ((0, 0), (0, new_pad), (0, 0)))
        sink_rows = jnp.broadcast_to(
            jnp.repeat(att_sinks[l].reshape(ATT_KV_HEADS, ATT_GROUP), dt, axis=1)[:, :, None],
            (ATT_KV_HEADS, ATT_GROUP * dt, LANES))
        oa_s = swa_sample(qs4, cache_win_k, cache_win_v, l, padn(k_new), padn(v_new), sink_rows)
        oa_s = oa_s.reshape(db, ATT_KV_HEADS, ATT_GROUP, dt, HEAD_DIM).transpose(0, 3, 1, 2, 4).reshape(n_s, Q_W)

        yb_p, conv_st = conv_prompt(pb_p, conv_dw[l], conv_db[l], conv_ln_g[l], conv_ln_b[l], b, t)
        yb_s, conv_new = conv_sample(cache_conv[l].transpose(1, 0, 2), time_major(pb_s),
                                     conv_dw[l], conv_db[l], conv_ln_g[l], conv_ln_b[l])
        yb_s = seq_major(yb_s)

        *x5_p, v_p, g_p, bonus_p = rwkv_pre_prompt(pc_p, lp, b, t)
        pre_s = [seq_major(a) for a in rwkv_pre_sample(time_major(pc_s), state_shift[l], lp)]
        *x5_s, v_s, g_s, bonus_s = pre_s
        oc_p, wkv_p = rwkv_scan(x5_p, v_p, jnp.zeros((b, RWKV_HEADS, RWKV_HEAD, RWKV_HEAD), F32), b, t,
                                skewed=scan_skewed(b, t))
        oc_s, wkv_s = rwkv_scan(x5_s, v_s, state_wkv[l], db, dt)

        xp = merge_out(xp, oa_p, yb_p, oc_p, bonus_p, g_p, pg_p, lp)
        xs = merge_out(xs, oa_s, yb_s, oc_s, bonus_s, g_s, pg_s, lp)

        mk = norm_matmul(mem, g_mem[l], w_mk[l].astype(BF16))
        mv = norm_matmul(mem, g_mem[l], w_mv[l].astype(BF16))
        wcq, wco = w_cq[l].astype(BF16), w_co[l].astype(BF16)
        ca_p = cross_attn(norm_matmul(xp, g_ca[l], wcq).reshape(n_p // tq, tq, d), mk.reshape(b, mem_len, d),
                          mv.reshape(b, mem_len, d), t // tq)
        ca_s = cross_attn(norm_matmul(xs, g_ca[l], wcq).reshape(db, dt, d), mem_k_all, mem_v_all, 1, first_seq=l * db)
        xp = matmul_residual(ca_p.reshape(n_p, d), wco, xp)
        xs = matmul_residual(ca_s.reshape(n_s, d), wco, xs)

        wpq = w_pq[l].astype(BF16)
        keys = peer_keys[l].reshape(2 * PEER_HEADS, N_KEYS, PEER_HALF).astype(BF16)
        xp = peer_dense(xp, g_ffn[l], peer_select(norm_matmul(xp, g_ffn[l], wpq), keys), u_bf, vt_bf, l)
        xs = peer_dense(xs, g_ffn[l], peer_select(norm_matmul(xs, g_ffn[l], wpq), keys), u_bf, vt_bf, l)

        outs[0].append(k4(k_p, b, t)[:, t - wb:])
        outs[1].append(k4(pa_p[:, Q_W + KV_W:], b, t)[:, t - wb:])
        outs[2].append(conv_st[:, CONV_HALO - (CONV_WIDTH - 1):])
        outs[3].append(pc_p.reshape(b, t, C_COLS)[:, -1])
        outs[4].append(wkv_p)
        outs[5].append(mk.reshape(b, mem_len, CA_HEADS, CA_HEAD_DIM))
        outs[6].append(mv.reshape(b, mem_len, CA_HEADS, CA_HEAD_DIM))
        outs[7].append(jnp.concatenate([cache_win_k[l][:, dt:], k4(k_new, db, dt)], axis=1))
        outs[8].append(jnp.concatenate([cache_win_v[l][:, dt:], k4(v_new, db, dt)], axis=1))
        outs[9].append(conv_new.transpose(1, 0, 2))
        outs[10].append(pc_s.reshape(db, dt, C_COLS)[:, -1])
        outs[11].append(wkv_s)

    yp = final_norm(xp, g_final)
    ys = final_norm(xs, g_final)
    return (yp.reshape(b, t, d), ys.reshape(db, dt, d)) + tuple(jnp.stack(o) for o in outs)
```

```python
import functools

import numpy as np
import jax
import jax.numpy as jnp
from jax import lax
from jax.experimental import pallas as pl
from jax.experimental.pallas import tpu as pltpu

F32 = jnp.float32
BF16 = jnp.bfloat16

D_MODEL = 1024
PAST_LEN = 8192
ATT_HEADS = 8
ATT_KV_HEADS = 2
ATT_GROUP = ATT_HEADS // ATT_KV_HEADS
HEAD_DIM = 64
ROT_DIM = HEAD_DIM // 4
ROPE_THETA = 500000.0
WINDOW = 128
CONV_CH = 512
CONV_WIDTH = 31
LN_EPS = 1e-5
RWKV_HEADS = 8
RWKV_HEAD = 64
RWKV_W = RWKV_HEADS * RWKV_HEAD
DECAY_LORA = 64
ICLR_LORA = 64
GATE_LORA = 128
GN_EPS = 64e-5
CA_HEADS = 4
CA_HEAD_DIM = D_MODEL // CA_HEADS
PEER_HEADS = 8
N_KEYS = 128
PEER_TOPK = 16
PEER_HALF = 128
RMS_EPS = 1e-6
NEG = -1e30

Q_W = ATT_HEADS * HEAD_DIM
KV_W = ATT_KV_HEADS * HEAD_DIM
A_COLS = Q_W + 2 * KV_W
B_COLS = 2 * CONV_CH
C_COLS = 3 * RWKV_W + DECAY_LORA + ICLR_LORA + GATE_LORA
G_COLS = 3 * D_MODEL

VMEM_LIMIT_V7X = 56 * 1024 * 1024
LANES = 128
SUBLANES = 8
SCAN_PAIRS = 16
SCAN_TSUB = 8
NT_DIMS = (((1,), (1,)), ((), ()))


def _tile(n, prefs=(512, 256, 128, 64, 32, 16, 8)):
    for t in prefs:
        if n % t == 0:
            return t
    raise ValueError(f"no tile for {n}")


def _params(sem):
    return pltpu.CompilerParams(dimension_semantics=sem, vmem_limit_bytes=VMEM_LIMIT_V7X)


def _sigmoid(x):
    return 1.0 / (1.0 + jnp.exp(-x))


def _full(shape):
    nd = len(shape)
    return pl.BlockSpec(shape, lambda *_: (0,) * nd)


def _norm_matmul_kernel(x_ref, g_ref, w_ref, o_ref):
    x = x_ref[...]
    h = x * lax.rsqrt(jnp.mean(x * x, axis=-1, keepdims=True) + RMS_EPS) * g_ref[...]
    o_ref[...] = jnp.dot(h.astype(BF16), w_ref[...], preferred_element_type=F32)


def norm_matmul(x, g, w):
    n, d = x.shape
    c = w.shape[1]
    tm = _tile(n)
    return pl.pallas_call(
        _norm_matmul_kernel,
        grid=(n // tm,),
        in_specs=[pl.BlockSpec((tm, d), lambda i: (i, 0)), _full((1, d)), _full((d, c))],
        out_specs=pl.BlockSpec((tm, c), lambda i: (i, 0)),
        out_shape=jax.ShapeDtypeStruct((n, c), F32),
        compiler_params=_params(("parallel",)),
    )(x, g.reshape(1, d), w)


def _matmul_res_kernel(a_ref, w_ref, x_ref, o_ref):
    o_ref[...] = x_ref[...] + jnp.dot(a_ref[...].astype(BF16), w_ref[...], preferred_element_type=F32)


def matmul_residual(a, w, x):
    n, k = a.shape
    c = w.shape[1]
    tm = _tile(n)
    return pl.pallas_call(
        _matmul_res_kernel,
        grid=(n // tm,),
        in_specs=[pl.BlockSpec((tm, k), lambda i: (i, 0)), _full((k, c)), pl.BlockSpec((tm, c), lambda i: (i, 0))],
        out_specs=pl.BlockSpec((tm, c), lambda i: (i, 0)),
        out_shape=jax.ShapeDtypeStruct((n, c), F32),
        compiler_params=_params(("parallel",)),
    )(a, w, x)


def _final_norm_kernel(x_ref, g_ref, o_ref):
    x = x_ref[...]
    o_ref[...] = x * lax.rsqrt(jnp.mean(x * x, axis=-1, keepdims=True) + RMS_EPS) * g_ref[...]


def final_norm(x, g):
    n, d = x.shape
    tm = _tile(n)
    return pl.pallas_call(
        _final_norm_kernel,
        grid=(n // tm,),
        in_specs=[pl.BlockSpec((tm, d), lambda i: (i, 0)), _full((1, d))],
        out_specs=pl.BlockSpec((tm, d), lambda i: (i, 0)),
        out_shape=jax.ShapeDtypeStruct((n, d), F32),
        compiler_params=_params(("parallel",)),
    )(x, g.reshape(1, d))


def _rotary_tables(pos):
    half = ROT_DIM // 2
    inv = ROPE_THETA ** (-2.0 * jnp.arange(half, dtype=F32) / ROT_DIM)
    ang = pos[:, None] * inv[None, :]
    cos, sin = jnp.cos(ang), jnp.sin(ang)
    n = pos.shape[0]
    one = jnp.ones((n, HEAD_DIM - ROT_DIM), F32)
    zero = jnp.zeros((n, HEAD_DIM - ROT_DIM), F32)
    zh = jnp.zeros((n, half), F32)
    c = jnp.concatenate([cos, cos, one], axis=1)
    s_up = jnp.concatenate([-sin, zh, zero], axis=1)
    s_dn = jnp.concatenate([zh, sin, zero], axis=1)
    return tuple(jnp.concatenate([t, t], axis=1) for t in (c, s_up, s_dn))


def _rotary_kernel(pa_ref, c_ref, su_ref, sd_ref, q_ref, k_ref):
    c, su, sd = c_ref[...], su_ref[...], sd_ref[...]
    half = ROT_DIM // 2
    for j in range((Q_W + KV_W) // LANES):
        x = pa_ref[:, j * LANES:(j + 1) * LANES]
        y = x * c + pltpu.roll(x, LANES - half, 1) * su + pltpu.roll(x, half, 1) * sd
        if j < Q_W // LANES:
            q_ref[:, j * LANES:(j + 1) * LANES] = y
        else:
            k_ref[...] = y


def rotary_qk(pa, tabs):
    n = pa.shape[0]
    tm = _tile(n)
    row = lambda w: pl.BlockSpec((tm, w), lambda i: (i, 0))
    return pl.pallas_call(
        _rotary_kernel,
        grid=(n // tm,),
        in_specs=[row(A_COLS), row(LANES), row(LANES), row(LANES)],
        out_specs=[row(Q_W), row(KV_W)],
        out_shape=[jax.ShapeDtypeStruct((n, Q_W), F32), jax.ShapeDtypeStruct((n, KV_W), F32)],
        compiler_params=_params(("parallel",)),
    )(pa, *tabs)


def _sink_softmax_pv(parts, sink):
    m = sink
    for s, _ in parts:
        m = jnp.maximum(m, jnp.max(s, axis=-1, keepdims=True))
    ps = [jnp.exp(s - m) for s, _ in parts]
    den = jnp.exp(sink - m)
    for p in ps:
        den = den + jnp.sum(p, axis=-1, keepdims=True)
    inv = 1.0 / den
    out = None
    for p, (_, v) in zip(ps, parts):
        o = jnp.dot((p * inv).astype(BF16), v, preferred_element_type=F32)
        out = o if out is None else out + o
    return out


def _swa_prompt_kernel(sink_ref, q_ref, kc_ref, kp_ref, vc_ref, vp_ref, o_ref):
    n = pl.program_id(1)
    qi = lax.broadcasted_iota(jnp.int32, (WINDOW, WINDOW), 0)
    kj = lax.broadcasted_iota(jnp.int32, (WINDOW, WINDOW), 1)
    mask_c = kj <= qi
    mask_p = kj > qi + jnp.where(n > 0, 0, WINDOW)
    scale = HEAD_DIM ** -0.5
    lane = lax.broadcasted_iota(jnp.int32, (WINDOW, KV_W), 1)
    low = lane < HEAD_DIM

    def halves(x, g):
        other = pltpu.roll(x, HEAD_DIM, 1)
        in_low, in_high = (x, other) if g == 0 else (other, x)
        return jnp.where(low, in_low, 0.0).astype(BF16), jnp.where(low, 0.0, in_high).astype(BF16)

    for g in range(ATT_KV_HEADS):
        kc, kp = halves(kc_ref[...], g), halves(kp_ref[...], g)
        vc, vp = halves(vc_ref[...], g), halves(vp_ref[...], g)
        for pair in range(ATT_GROUP // 2):
            col = (g * (ATT_GROUP // 2) + pair) * KV_W
            q2 = q_ref[:, col:col + KV_W].astype(BF16)
            out = None
            for odd in range(2):
                sc = lax.dot_general(q2, kc[odd], NT_DIMS, preferred_element_type=F32) * scale
                sp = lax.dot_general(q2, kp[odd], NT_DIMS, preferred_element_type=F32) * scale
                sc = jnp.where(mask_c, sc, NEG)
                sp = jnp.where(mask_p, sp, NEG)
                o = _sink_softmax_pv([(sp, vp[odd]), (sc, vc[odd])], sink_ref[col // HEAD_DIM + odd])
                out = o if out is None else out + o
            o_ref[:, col:col + KV_W] = out


def swa_prompt(q, k, pa, sinks, b, t):
    nb = t // WINDOW
    cur = lambda w, c: pl.BlockSpec((WINDOW, w), lambda bi, ni: (bi * nb + ni, c))
    prev = lambda w, c: pl.BlockSpec((WINDOW, w), lambda bi, ni: (bi * nb + jnp.maximum(ni - 1, 0), c))
    vcol = (Q_W + KV_W) // KV_W
    return pl.pallas_call(
        _swa_prompt_kernel,
        grid=(b, nb),
        in_specs=[pl.BlockSpec(memory_space=pltpu.SMEM), cur(Q_W, 0), cur(KV_W, 0), prev(KV_W, 0),
                  cur(KV_W, vcol), prev(KV_W, vcol)],
        out_specs=cur(Q_W, 0),
        out_shape=jax.ShapeDtypeStruct((b * t, Q_W), F32),
        compiler_params=_params(("parallel", "parallel")),
    )(sinks, q, k, k, pa, pa)


def _swa_sample_kernel(sink_ref, q_ref, kb_ref, vb_ref, kn_ref, vn_ref, o_ref, *, bt, dt, wb):
    rows = ATT_GROUP * dt
    tb = lax.rem(lax.broadcasted_iota(jnp.int32, (rows, wb), 0), dt)
    jb = lax.broadcasted_iota(jnp.int32, (rows, wb), 1)
    mask_b = jb > tb + (wb - WINDOW)
    npad = kn_ref.shape[1]
    tn = lax.rem(lax.broadcasted_iota(jnp.int32, (rows, npad), 0), dt)
    jn = lax.broadcasted_iota(jnp.int32, (rows, npad), 1)
    mask_n = jn <= tn
    scale = HEAD_DIM ** -0.5
    for b in range(bt):
        for g in range(ATT_KV_HEADS):
            gs = slice(g * HEAD_DIM, (g + 1) * HEAD_DIM)
            q = q_ref[b, g].astype(BF16)
            sb = lax.dot_general(q, kb_ref[b, :, g, :].astype(BF16), NT_DIMS, preferred_element_type=F32) * scale
            sn = lax.dot_general(q, kn_ref[b, :, gs].astype(BF16), NT_DIMS, preferred_element_type=F32) * scale
            sb = jnp.where(mask_b, sb, NEG)
            sn = jnp.where(mask_n, sn, NEG)
            o_ref[b, g] = _sink_softmax_pv(
                [(sb, vb_ref[b, :, g, :].astype(BF16)), (sn, vn_ref[b, :, gs].astype(BF16))], sink_ref[g][:, :1])


def swa_sample(qs, cache_k, cache_v, layer, knew, vnew, sink_rows):
    db, _, rows, _ = qs.shape
    dt = rows // ATT_GROUP
    wb = cache_k.shape[2]
    bt = _tile(db, (8, 4, 2, 1))
    blk = lambda a: pl.BlockSpec((bt,) + a.shape[1:], lambda i: (i,) + (0,) * (a.ndim - 1))
    cache = pl.BlockSpec((None, bt) + cache_k.shape[2:], lambda i: (layer, i, 0, 0, 0))
    return pl.pallas_call(
        functools.partial(_swa_sample_kernel, bt=bt, dt=dt, wb=wb),
        grid=(db // bt,),
        in_specs=[_full(sink_rows.shape), blk(qs), cache, cache, blk(knew), blk(vnew)],
        out_specs=blk(qs),
        out_shape=jax.ShapeDtypeStruct(qs.shape, F32),
        compiler_params=_params(("parallel",)),
    )(sink_rows, qs, cache_k, cache_v, knew, vnew)


CONV_HALO = 32


def _ln_swish(y, lg, lb):
    mu = jnp.mean(y, axis=-1, keepdims=True)
    var = jnp.mean(jnp.square(y - mu), axis=-1, keepdims=True)
    yn = (y - mu) * lax.rsqrt(var + LN_EPS) * lg + lb
    return yn * _sigmoid(yn)


def _conv_prompt_kernel(pb_ref, dw_ref, db_ref, lg_ref, lb_ref, y_ref, st_ref, ext_ref, sh_ref, *, tt):
    i = pl.program_id(1)

    @pl.when(i == 0)
    def _():
        ext_ref[0:CONV_HALO, :] = jnp.zeros((CONV_HALO, CONV_CH), F32)

    @pl.when(i > 0)
    def _():
        ext_ref[0:CONV_HALO, :] = ext_ref[tt:tt + CONV_HALO, :]

    pb = pb_ref[...]
    ext_ref[CONV_HALO:CONV_HALO + tt, :] = pb[:, :CONV_CH] * _sigmoid(pb[:, CONV_CH:])
    acc = jnp.zeros((tt, CONV_CH), F32) + db_ref[...]
    first = CONV_HALO - (CONV_WIDTH - 1)
    span = tt + CONV_HALO - SUBLANES
    for r in range(SUBLANES):
        taps = [j for j in range(CONV_WIDTH) if (first + j) % SUBLANES == r]
        if r:
            sh_ref[0:span, :] = ext_ref[pl.ds(r, span), :]
        src = sh_ref if r else ext_ref
        for j in taps:
            q = first + j - r
            acc = acc + src[q:q + tt, :] * dw_ref[j:j + 1, :]
    y_ref[...] = _ln_swish(acc, lg_ref[...], lb_ref[...])

    @pl.when(i == pl.num_programs(1) - 1)
    def _():
        st_ref[0] = ext_ref[tt:tt + CONV_HALO, :]


def conv_prompt(pb, dw, db, lg, lb, b, t):
    tt = _tile(t)
    nt = t // tt
    vec = lambda a: a.reshape(1, CONV_CH)
    return pl.pallas_call(
        functools.partial(_conv_prompt_kernel, tt=tt),
        grid=(b, nt),
        in_specs=[pl.BlockSpec((tt, B_COLS), lambda bi, i: (bi * nt + i, 0)), _full((CONV_WIDTH, CONV_CH)),
                  _full((1, CONV_CH)), _full((1, CONV_CH)), _full((1, CONV_CH))],
        out_specs=[pl.BlockSpec((tt, CONV_CH), lambda bi, i: (bi * nt + i, 0)),
                   pl.BlockSpec((1, CONV_HALO, CONV_CH), lambda bi, i: (bi, 0, 0))],
        out_shape=[jax.ShapeDtypeStruct((b * t, CONV_CH), F32), jax.ShapeDtypeStruct((b, CONV_HALO, CONV_CH), F32)],
        scratch_shapes=[pltpu.VMEM((tt + CONV_HALO, CONV_CH), F32), pltpu.VMEM((tt + CONV_HALO, CONV_CH), F32)],
        compiler_params=_params(("arbitrary", "arbitrary")),
    )(pb, dw, vec(db), vec(lg), vec(lb))


def _conv_sample_kernel(c_ref, pb_ref, dw_ref, db_ref, lg_ref, lb_ref, y_ref, nc_ref, *, dt):
    nprev = CONV_WIDTH - 1
    us = []
    for t in range(dt):
        pb = pb_ref[t]
        us.append(pb[:, :CONV_CH] * _sigmoid(pb[:, CONV_CH:]))

    def ext(j):
        return c_ref[j] if j < nprev else us[j - nprev]

    for t in range(dt):
        acc = db_ref[...] + ext(t) * dw_ref[0:1, :]
        for j in range(1, CONV_WIDTH):
            acc = acc + ext(t + j) * dw_ref[j:j + 1, :]
        y_ref[t] = _ln_swish(acc, lg_ref[...], lb_ref[...])
    for j in range(nprev):
        nc_ref[j] = ext(j + dt)


def conv_sample(cache_t, pb_t, dw, db, lg, lb):
    nprev, dbt, _ = cache_t.shape
    dt = pb_t.shape[0]
    bt = _tile(dbt, (32, 16, 8))
    vec = lambda a: a.reshape(1, CONV_CH)
    blk = lambda lead, w: pl.BlockSpec((lead, bt, w), lambda i: (0, i, 0))
    return pl.pallas_call(
        functools.partial(_conv_sample_kernel, dt=dt),
        grid=(dbt // bt,),
        in_specs=[blk(nprev, CONV_CH), blk(dt, B_COLS), _full((CONV_WIDTH, CONV_CH)), _full((1, CONV_CH)),
                  _full((1, CONV_CH)), _full((1, CONV_CH))],
        out_specs=[blk(dt, CONV_CH), blk(nprev, CONV_CH)],
        out_shape=[jax.ShapeDtypeStruct((dt, dbt, CONV_CH), F32), jax.ShapeDtypeStruct((nprev, dbt, CONV_CH), F32)],
        compiler_params=_params(("parallel",)),
    )(cache_t, pb_t, dw, vec(db), vec(lg), vec(lb))


def _head_sum_matrix():
    h = np.arange(RWKV_W) // RWKV_HEAD
    return jnp.asarray((h[:, None] == h[None, :]).astype(np.float32), dtype=BF16)


def _head_sum(x, hs_ref):
    hi = x.astype(BF16)
    lo = (x - hi.astype(F32)).astype(BF16)
    hs = hs_ref[...]
    return jnp.dot(hi, hs, preferred_element_type=F32) + jnp.dot(lo, hs, preferred_element_type=F32)


def _rwkv_pre_math(pc, prev, c):
    mu_ref, w0_ref, w2_ref, a0_ref, a2_ref, g2_ref, kkp_ref, ka_ref, rk_ref, hs_ref = c
    xs = pc + (prev - pc) * mu_ref[...]
    o1, o2, o3 = RWKV_W, 2 * RWKV_W, 3 * RWKV_W
    o4 = o3 + DECAY_LORA
    o5 = o4 + ICLR_LORA
    r, k, v = xs[:, :o1], xs[:, o1:o2], xs[:, o2:o3]
    wd, ad, gd = xs[:, o3:o4], xs[:, o4:o5], xs[:, o5:]
    y = -(w0_ref[...] + jnp.dot(jnp.tanh(wd).astype(BF16), w2_ref[...], preferred_element_type=F32))
    softplus = jnp.maximum(y, 0.0) + jnp.log(1.0 + jnp.exp(-jnp.abs(y)))
    decay = jnp.exp(-jnp.exp(-softplus - 0.5))
    a = _sigmoid(a0_ref[...] + jnp.dot(ad.astype(BF16), a2_ref[...], preferred_element_type=F32))
    g = jnp.dot(_sigmoid(gd).astype(BF16), g2_ref[...], preferred_element_type=F32)
    kk = k * kkp_ref[...]
    norm = jnp.sqrt(_head_sum(kk * kk, hs_ref))
    kk = kk / jnp.maximum(norm, 1e-12)
    k_mod = k * (1.0 + (a - 1.0) * ka_ref[...])
    bonus = _head_sum(r * k_mod * rk_ref[...], hs_ref) * v
    return r, decay, k_mod, -kk, kk * a, v, g, bonus


RWKV_PRE_OUTS = 8
SCAN_KEY_OPERANDS = 5
SCAN_CHUNK = 256
SCAN_GROUP_UNROLL = 1
SHIFT_HALO = SUBLANES


def _rwkv_pre_prompt_kernel(pc_ref, *refs, tt, skew):
    consts, outs, ext_ref = refs[:10], refs[10:10 + RWKV_PRE_OUTS], refs[-1]
    i = pl.program_id(1)

    @pl.when(i == 0)
    def _():
        ext_ref[0:SHIFT_HALO, :] = jnp.zeros((SHIFT_HALO, C_COLS), F32)

    @pl.when(i > 0)
    def _():
        ext_ref[0:SHIFT_HALO, :] = ext_ref[tt:tt + SHIFT_HALO, :]

    pc = pc_ref[...]
    ext_ref[SHIFT_HALO:SHIFT_HALO + tt, :] = pc
    res = _rwkv_pre_math(pc, ext_ref[pl.ds(SHIFT_HALO - 1, tt), :], consts)
    for j, (o_ref, val) in enumerate(zip(outs, res)):
        if j < SCAN_KEY_OPERANDS and skew:
            vt = val.T
            for h in range(RWKV_HEADS):
                g = pl.program_id(0) * RWKV_HEADS + h
                rows = slice(h * RWKV_HEAD, (h + 1) * RWKV_HEAD)
                o_ref[rows, :] = pltpu.roll(vt[rows, :], g * SCAN_TSUB, 1)
        else:
            o_ref[...] = val


def _rwkv_pre_sample_kernel(pc_ref, shift_ref, *refs, dt):
    consts, outs = refs[:10], refs[10:10 + RWKV_PRE_OUTS]
    for t in range(dt):
        res = _rwkv_pre_math(pc_ref[t], shift_ref[...] if t == 0 else pc_ref[t - 1], consts)
        for o_ref, val in zip(outs, res):
            o_ref[t] = val


def _rwkv_pre_consts(lp):
    vec = lambda a: a.reshape(1, -1)
    return [vec(lp['rk_mu']), vec(lp['rk_w0']), lp['rk_w2'].astype(BF16), vec(lp['rk_a0']), lp['rk_a2'].astype(BF16),
            lp['rk_g2'].astype(BF16), vec(lp['rk_kk']), vec(lp['rk_ka']), vec(lp['rk_rk']), _head_sum_matrix()]


def scan_skewed(b, t):
    return b * RWKV_HEADS == SCAN_PAIRS and t % SCAN_CHUNK == 0


def rwkv_pre_prompt(pc, lp, b, t):
    skew = scan_skewed(b, t)
    tt = SCAN_CHUNK if skew else _tile(t, (256, 128))
    nt = t // tt
    consts = _rwkv_pre_consts(lp)
    row = lambda w: pl.BlockSpec((tt, w), lambda bi, i: (bi * nt + i, 0))
    nkey = SCAN_KEY_OPERANDS if skew else 0
    col = pl.BlockSpec((RWKV_W, tt), lambda bi, i: (bi, i))
    return pl.pallas_call(
        functools.partial(_rwkv_pre_prompt_kernel, tt=tt, skew=skew),
        grid=(b, nt),
        in_specs=[row(C_COLS)] + [_full(c.shape) for c in consts],
        out_specs=[col] * nkey + [row(RWKV_W)] * (RWKV_PRE_OUTS - nkey),
        out_shape=[jax.ShapeDtypeStruct((b * RWKV_W, t), F32)] * nkey
        + [jax.ShapeDtypeStruct((b * t, RWKV_W), F32)] * (RWKV_PRE_OUTS - nkey),
        scratch_shapes=[pltpu.VMEM((tt + SHIFT_HALO, C_COLS), F32)],
        compiler_params=_params(("arbitrary", "arbitrary")),
    )(pc, *consts)


def rwkv_pre_sample(pc_t, shift, lp):
    dt, dbt, _ = pc_t.shape
    bt = _tile(dbt, (64, 32, 16, 8))
    consts = _rwkv_pre_consts(lp)
    blk = lambda w: pl.BlockSpec((dt, bt, w), lambda i: (0, i, 0))
    return pl.pallas_call(
        functools.partial(_rwkv_pre_sample_kernel, dt=dt),
        grid=(dbt // bt,),
        in_specs=[blk(C_COLS), pl.BlockSpec((bt, C_COLS), lambda i: (i, 0))] + [_full(c.shape) for c in consts],
        out_specs=[blk(RWKV_W)] * RWKV_PRE_OUTS,
        out_shape=[jax.ShapeDtypeStruct((dt, dbt, RWKV_W), F32)] * RWKV_PRE_OUTS,
        compiler_params=_params(("parallel",)),
    )(pc_t, shift, *consts)


def _scan_select_matrices(pair_major):
    e = np.zeros((SCAN_TSUB, LANES, LANES), np.float32)
    vl = LANES // SCAN_PAIRS
    for tl in range(SCAN_TSUB):
        for g in range(SCAN_PAIRS):
            src = g * SCAN_TSUB + tl if pair_major else tl * SCAN_PAIRS + g
            e[tl, src, g * vl:(g + 1) * vl] = 1.0
    return jnp.asarray(np.concatenate([e, e, e], axis=1), dtype=BF16)


def _sublane_allsum(x):
    x = x + pltpu.roll(x, 4, 0)
    x = x + pltpu.roll(x, 2, 0)
    return x + pltpu.roll(x, 1, 0)


def _scan_kernel(xr_ref, xw_ref, xk_ref, xa_ref, xb_ref, v_ref, s0_ref, e_ref, o_ref, sf_ref, z_ref, tile_ref, xs_ref,
                 *, groups, steps, skewed):
    x_refs = (xr_ref, xw_ref, xk_ref, xa_ref, xb_ref)
    c = pl.program_id(1)
    nvh = z_ref.shape[0]
    nkb = RWKV_HEAD // SUBLANES
    sub = lax.broadcasted_iota(jnp.int32, (SUBLANES, LANES), 0)
    R_ROW, W_ROW, K_ROW, A_ROW, B_ROW = (j * RWKV_HEAD for j in range(5))

    @pl.when(c == 0)
    def _():
        z_ref[...] = s0_ref[0]

    lane_group = lax.broadcasted_iota(jnp.int32, (SUBLANES, LANES), 1) // SCAN_TSUB

    def gather_group(x_ref, th, kb):
        acc = jnp.zeros((SUBLANES, LANES), F32)
        for g in range(SCAN_PAIRS):
            pos = th + g
            tile = pl.multiple_of(((pos // SCAN_PAIRS) % 2) * LANES, LANES)
            rows = slice(g * RWKV_HEAD + kb * SUBLANES, g * RWKV_HEAD + (kb + 1) * SUBLANES)
            acc = jnp.where(lane_group == pos % SCAN_PAIRS, x_ref[rows, pl.ds(tile, LANES)], acc)
        return pltpu.roll(acc, ((SCAN_PAIRS - th % SCAN_PAIRS) % SCAN_PAIRS) * SCAN_TSUB, 1)

    def split_group(th):
        th = jnp.asarray(th, jnp.int32)
        for j, x_ref in enumerate(x_refs):
            rows = slice(j * RWKV_HEAD, (j + 1) * RWKV_HEAD)
            if skewed:
                xg = jnp.concatenate([gather_group(x_ref, th, kb) for kb in range(nkb)], axis=0)
            else:
                xg = x_ref[0, th]
            hi = xg.astype(BF16)
            r1 = xg - hi.astype(F32)
            mid = r1.astype(BF16)
            lo = (r1 - mid.astype(F32)).astype(BF16)
            xs_ref[rows, 0:LANES] = hi
            xs_ref[rows, LANES:2 * LANES] = mid
            xs_ref[rows, 2 * LANES:3 * LANES] = lo

    def spread(tl, buf):
        tile_ref[buf] = jnp.dot(xs_ref[...], e_ref[tl], preferred_element_type=F32)

    split_group(0)
    spread(0, 0)

    def group_steps(th):
        for tl in range(steps):
            cur = tl % 2
            t = th * steps + tl
            vrow = v_ref[0, t]
            blk = lambda row0, kb: tile_ref[cur, row0 + kb * SUBLANES:row0 + (kb + 1) * SUBLANES, :]
            sa = []
            for vh in range(nvh):
                acc = None
                for kb in range(nkb):
                    p = z_ref[vh, kb * SUBLANES:(kb + 1) * SUBLANES, :] * blk(A_ROW, kb)
                    acc = p if kb == 0 else acc + p
                sa.append(_sublane_allsum(acc))
            if tl + 1 < steps:
                spread(tl + 1, 1 - cur)
            else:
                split_group(jnp.minimum(th + 1, groups - 1))
                spread(0, 1 - cur)
            orow = jnp.zeros((SUBLANES, LANES), F32)
            for vh in range(nvh):
                vb = jnp.broadcast_to(vrow[vh:vh + 1], (SUBLANES, LANES))
                acc = None
                for kb in range(nkb):
                    ks = slice(kb * SUBLANES, (kb + 1) * SUBLANES)
                    zn = z_ref[vh, ks, :] * blk(W_ROW, kb) + blk(B_ROW, kb) * sa[vh] + blk(K_ROW, kb) * vb
                    z_ref[vh, ks, :] = zn
                    p = zn * blk(R_ROW, kb)
                    acc = p if kb == 0 else acc + p
                orow = jnp.where(sub == vh, _sublane_allsum(acc), orow)
            o_ref[0, t] = orow

    unroll = SCAN_GROUP_UNROLL if groups % SCAN_GROUP_UNROLL == 0 else 1

    def body(i, carry):
        for u in range(unroll):
            group_steps(i * unroll + u)
        return carry

    lax.fori_loop(0, groups // unroll, body, 0)

    @pl.when(c == pl.num_programs(1) - 1)
    def _():
        sf_ref[0] = z_ref[...]


def rwkv_scan(x5, v, s0, nseq, t, skewed=False):
    spb = SCAN_PAIRS // RWKV_HEADS
    assert nseq % spb == 0
    nblk = nseq // spb
    vl = LANES // SCAN_PAIRS
    vh = RWKV_HEAD // vl
    steps = min(SCAN_TSUB, t)
    assert t % steps == 0 and steps % 2 == 0
    t8 = t // steps
    def key_layout(a):
        a = a.reshape(nblk, spb, t8, steps, RWKV_HEADS, RWKV_HEAD)
        if steps < SCAN_TSUB:
            a = jnp.pad(a, ((0, 0),) * 3 + ((0, SCAN_TSUB - steps),) + ((0, 0),) * 2)
        return a.transpose(0, 2, 5, 3, 1, 4).reshape(nblk, t8, RWKV_HEAD, LANES)

    vk = v.reshape(nblk, spb, t, RWKV_HEADS, vh, vl).transpose(0, 2, 4, 1, 3, 5).reshape(nblk, t, vh, LANES)
    sk = s0.reshape(nblk, spb, RWKV_HEADS, vh, vl, RWKV_HEAD).transpose(0, 3, 5, 1, 2, 4).reshape(
        nblk, vh, RWKV_HEAD, LANES)
    if skewed:
        assert steps == SCAN_TSUB and t % SCAN_CHUNK == 0
        xk = x5
        groups = SCAN_CHUNK // steps
        key_spec = pl.BlockSpec((SCAN_PAIRS * RWKV_HEAD, SCAN_CHUNK), lambda i, c: (i, c))
    else:
        xk = [key_layout(a) for a in x5]
        groups = _tile(t8, (32, 16, 8, 4, 2, 1))
        key_spec = pl.BlockSpec((1, groups, RWKV_HEAD, LANES), lambda i, c: (i, c, 0, 0))
    nchunk = t8 // groups
    ch = groups * steps
    o, sf = pl.pallas_call(
        functools.partial(_scan_kernel, groups=groups, steps=steps, skewed=skewed),
        grid=(nblk, nchunk),
        in_specs=[key_spec] * 5 + [
                  pl.BlockSpec((1, ch, vh, LANES), lambda i, c: (i, c, 0, 0)),
                  pl.BlockSpec((1, vh, RWKV_HEAD, LANES), lambda i, c: (i, 0, 0, 0)),
                  _full((SCAN_TSUB, 3 * LANES, LANES))],
        out_specs=[pl.BlockSpec((1, ch, vh, LANES), lambda i, c: (i, c, 0, 0)),
                   pl.BlockSpec((1, vh, RWKV_HEAD, LANES), lambda i, c: (i, 0, 0, 0))],
        out_shape=[jax.ShapeDtypeStruct((nblk, t, vh, LANES), F32),
                   jax.ShapeDtypeStruct((nblk, vh, RWKV_HEAD, LANES), F32)],
        scratch_shapes=[pltpu.VMEM((vh, RWKV_HEAD, LANES), F32), pltpu.VMEM((2, 5 * RWKV_HEAD, LANES), F32),
                        pltpu.VMEM((5 * RWKV_HEAD, 3 * LANES), BF16)],
        compiler_params=_params(("arbitrary", "arbitrary")),
    )(*xk, vk, sk, _scan_select_matrices(pair_major=skewed))
    o = o.reshape(nblk, t, vh, spb, RWKV_HEADS, vl).transpose(0, 3, 1, 4, 2, 5).reshape(nseq * t, RWKV_W)
    sf = sf.reshape(nblk, vh, RWKV_HEAD, spb, RWKV_HEADS, vl).transpose(0, 3, 4, 1, 5, 2).reshape(
        nseq, RWKV_HEADS, RWKV_HEAD, RWKV_HEAD)
    return o, sf


def _merge_kernel(x_ref, oa_ref, yb_ref, oc_ref, bonus_ref, g_ref, pg_ref, wa_ref, wb_ref, bb_ref, wc_ref, wo_ref,
                  gng_ref, gnb_ref, hs_ref, o_ref):
    oc = oc_ref[...]
    mu = _head_sum(oc, hs_ref) * (1.0 / RWKV_HEAD)
    dev = oc - mu
    var = _head_sum(dev * dev, hs_ref) * (1.0 / RWKV_HEAD)
    on = dev * lax.rsqrt(var + GN_EPS) * gng_ref[...] + gnb_ref[...]
    yc = (on + bonus_ref[...]) * g_ref[...]
    d = D_MODEL
    dot = lambda a, w: jnp.dot(a.astype(BF16), w[...], preferred_element_type=F32)
    merged = (_sigmoid(pg_ref[:, 0:d]) * dot(oa_ref[...], wa_ref)
              + _sigmoid(pg_ref[:, d:2 * d]) * (dot(yb_ref[...], wb_ref) + bb_ref[...])
              + _sigmoid(pg_ref[:, 2 * d:3 * d]) * dot(yc, wc_ref))
    o_ref[...] = x_ref[...] + dot(merged, wo_ref)


def merge_out(x, oa, yb, oc, bonus, g, pg, lp):
    n = x.shape[0]
    tm = _tile(n, (256, 128, 64, 32, 16, 8))
    row = lambda w: pl.BlockSpec((tm, w), lambda i: (i, 0))
    vec = lambda a: a.reshape(1, -1)
    consts = [lp['w_a_out'].astype(BF16), lp['w_b_out'].astype(BF16), vec(lp['b_b_out']), lp['w_c_out'].astype(BF16),
              lp['w_o'].astype(BF16), vec(lp['rk_gn_g']), vec(lp['rk_gn_b']), _head_sum_matrix()]
    return pl.pallas_call(
        _merge_kernel,
        grid=(n // tm,),
        in_specs=[row(D_MODEL), row(Q_W), row(CONV_CH), row(RWKV_W), row(RWKV_W), row(RWKV_W), row(G_COLS)]
        + [_full(c.shape) for c in consts],
        out_specs=row(D_MODEL),
        out_shape=jax.ShapeDtypeStruct((n, D_MODEL), F32),
        compiler_params=_params(("parallel",)),
    )(x, oa, yb, oc, bonus, g, pg, *consts)


def _cross_attn_kernel(q_ref, k_ref, v_ref, o_ref):
    scale = CA_HEAD_DIM ** -0.5
    for b in range(q_ref.shape[0]):
        outs = []
        for h in range(CA_HEADS):
            hs = slice(h * CA_HEAD_DIM, (h + 1) * CA_HEAD_DIM)
            q = q_ref[b, :, hs].astype(BF16)
            s = lax.dot_general(q, k_ref[b, :, hs].astype(BF16), NT_DIMS, preferred_element_type=F32) * scale
            m = jnp.max(s, axis=-1, keepdims=True)
            p = jnp.exp(s - m)
            p = p / jnp.sum(p, axis=-1, keepdims=True)
            outs.append(jnp.dot(p.astype(BF16), v_ref[b, :, hs].astype(BF16), preferred_element_type=F32))
        o_ref[b] = jnp.concatenate(outs, axis=1)


def cross_attn(q, mk, mv, tiles_per_seq, first_seq=0, per_step=1):
    nt, tq, d = q.shape
    m = mk.shape[1]
    assert per_step == 1 or (tiles_per_seq == 1 and nt % per_step == 0 and first_seq % per_step == 0)
    nt //= per_step
    kv = pl.BlockSpec((per_step, m, d), lambda i: (first_seq // per_step + i // tiles_per_seq, 0, 0))
    qs = pl.BlockSpec((per_step, tq, d), lambda i: (i, 0, 0))
    return pl.pallas_call(
        _cross_attn_kernel,
        grid=(nt,),
        in_specs=[qs, kv, kv],
        out_specs=qs,
        out_shape=jax.ShapeDtypeStruct(q.shape, F32),
        compiler_params=_params(("parallel",)),
    )(q, mk, mv)


def _staircase():
    return [(i, PEER_TOPK // (i + 1)) for i in range(PEER_TOPK)]


PEER_CAND = sum(nj for _, nj in _staircase())
PEER_CAND_PAD = -(-PEER_CAND // SUBLANES) * SUBLANES


def _extract_topk(cur, out_ref, base, with_rank=False):
    rank = jnp.full(cur.shape, float(PEER_TOPK), F32) if with_rank else None
    for k in range(PEER_TOPK):
        m = jnp.max(cur, axis=0, keepdims=True)
        out_ref[base + k:base + k + 1, :] = m
        hit = cur == m
        if with_rank:
            rank = jnp.where(hit, float(k), rank)
        if k + 1 < PEER_TOPK:
            cur = jnp.where(hit, -jnp.inf, cur)
    return rank


def _peer_select_kernel(x_ref, g_ref, wq_ref, keys_ref, r1_ref, rho_ref, m2_ref, e2_ref, qp_ref, s_ref, tv_ref,
                        cand_ref, best_ref):
    nsub = 2 * PEER_HEADS
    x = x_ref[...]
    xn = x * lax.rsqrt(jnp.mean(x * x, axis=-1, keepdims=True) + RMS_EPS) * g_ref[...]
    qp_ref[...] = jnp.dot(xn.astype(BF16), wq_ref[...], preferred_element_type=F32)
    for hc in range(nsub):
        q = qp_ref[:, hc * PEER_HALF:(hc + 1) * PEER_HALF].astype(BF16)
        st = lax.dot_general(keys_ref[hc], q, NT_DIMS, preferred_element_type=F32)
        s_ref[hc] = st
        rank = _extract_topk(st, tv_ref, hc * PEER_TOPK, with_rank=hc % 2 == 0)
        if hc % 2 == 0:
            r1_ref[hc // 2] = rank
    tm = qp_ref.shape[0]
    for h in range(PEER_HEADS):
        b1 = 2 * h * PEER_TOPK
        b2 = b1 + PEER_TOPK
        cand_ref[PEER_CAND_PAD - SUBLANES:PEER_CAND_PAD, :] = jnp.full((SUBLANES, tm), -jnp.inf, F32)
        off = 0
        for i, nj in _staircase():
            cand_ref[off:off + nj, :] = tv_ref[b1 + i:b1 + i + 1, :] + tv_ref[b2:b2 + nj, :]
            off += nj
        _extract_topk(cand_ref[...], best_ref, 0)
        best = best_ref[...]
        z = jnp.sum(jnp.exp(best - best[0:1, :]), axis=0, keepdims=True)
        thr = best[PEER_TOPK - 1:PEER_TOPK, :]
        s2 = s_ref[2 * h + 1]
        m2 = jnp.zeros(s2.shape, F32)
        for i in range(PEER_TOPK):
            m2 = jnp.where((tv_ref[b1 + i:b1 + i + 1, :] + s2) >= thr, float(i + 1), m2)
        m2_ref[h] = m2
        e2_ref[h] = jnp.exp(s2 - tv_ref[b2:b2 + 1, :])
        rho_ref[h] = jnp.exp(s_ref[2 * h] - tv_ref[b1:b1 + 1, :]) * (1.0 / z)


def peer_select(x, g, wq, keys):
    n, d = x.shape
    tm = _tile(n, (256, 128))
    nsub = 2 * PEER_HEADS
    blk = pl.BlockSpec((PEER_HEADS, N_KEYS, tm), lambda i: (0, 0, i))
    shp = lambda dt: jax.ShapeDtypeStruct((PEER_HEADS, N_KEYS, n), dt)
    return pl.pallas_call(
        _peer_select_kernel,
        grid=(n // tm,),
        in_specs=[pl.BlockSpec((tm, d), lambda i: (i, 0)), _full((1, d)), _full(wq.shape), _full(keys.shape)],
        out_specs=[blk, blk, blk, blk],
        out_shape=[shp(F32)] * 4,
        scratch_shapes=[pltpu.VMEM((tm, nsub * PEER_HALF), F32), pltpu.VMEM((nsub, N_KEYS, tm), F32),
                        pltpu.VMEM((nsub * PEER_TOPK, tm), F32), pltpu.VMEM((PEER_CAND_PAD, tm), F32),
                        pltpu.VMEM((PEER_TOPK, tm), F32)],
        compiler_params=_params(("parallel",)),
    )(x, g.reshape(1, d), wq, keys)


PEER_PACK_ROWS = 16
PEER_EXPERT_TILE = 8 * N_KEYS


def _gelu_tanh(x):
    return 0.5 * x * (1.0 + jnp.tanh(0.7978845608028654 * (x + 0.044715 * (x * x * x))))


def _peer_dense_kernel(x_ref, g_ref, r1_ref, rho_ref, m2_ref, e2_ref, u_ref, vt_ref, o_ref, xnt_ref, acc_ref, ht_ref,
                       wh_ref, m2b_ref, e2b_ref, rows_ref, *, te):
    e = pl.program_id(1)
    tm = x_ref.shape[0]
    pk = PEER_PACK_ROWS

    @pl.when(e == 0)
    def _():
        x = x_ref[...]
        xn = x * lax.rsqrt(jnp.mean(x * x, axis=-1, keepdims=True) + RMS_EPS) * g_ref[...]
        xnt_ref[...] = xn.T.astype(BF16)
        acc_ref[...] = jnp.zeros_like(acc_ref)
        m2b_ref[...] = m2_ref[...].astype(BF16)
        e2b_ref[...] = e2_ref[...].astype(BF16)

    n_a = te // N_KEYS
    for al in range(n_a):
        a = e * n_a + al
        for h in range(PEER_HEADS):
            row = 2 * (al * PEER_HEADS + h)
            rows_ref[row:row + 1, :] = r1_ref[h, pl.ds(a, 1), :]
            rows_ref[row + 1:row + 2, :] = rho_ref[h, pl.ds(a, 1), :]

    ht_ref[...] = jnp.dot(u_ref[...], xnt_ref[...], preferred_element_type=F32)
    zero = jnp.zeros((), BF16)
    for al in range(n_a):
        for cl in range(tm // LANES):
            cs = slice(cl * LANES, (cl + 1) * LANES)
            packed = lambda row: jnp.broadcast_to(rows_ref[row:row + 1, cs], (pk, LANES)).astype(BF16)
            r1_rows = [packed(2 * (al * PEER_HEADS + h)) for h in range(PEER_HEADS)]
            rho_rows = [packed(2 * (al * PEER_HEADS + h) + 1) for h in range(PEER_HEADS)]
            for rb in range(N_KEYS // pk):
                rs = slice(rb * pk, (rb + 1) * pk)
                w = None
                for h in range(PEER_HEADS):
                    sel = r1_rows[h] < m2b_ref[h, rs, cs]
                    contrib = jnp.where(sel, e2b_ref[h, rs, cs] * rho_rows[h], zero)
                    w = contrib if w is None else w + contrib
                hr = slice(al * N_KEYS + rb * pk, al * N_KEYS + (rb + 1) * pk)
                wh_ref[hr, cs] = w * _gelu_tanh(ht_ref[hr, cs].astype(BF16))
    acc_ref[...] += jnp.dot(vt_ref[...], wh_ref[...], preferred_element_type=F32)

    @pl.when(e == pl.num_programs(1) - 1)
    def _():
        o_ref[...] = x_ref[...] + acc_ref[...].T


def peer_dense(x, g, sel, u, v_t, layer):
    n, d = x.shape
    ne = u.shape[1]
    tm = _tile(n, (512, 256, 128))
    te = PEER_EXPERT_TILE
    blk = pl.BlockSpec((PEER_HEADS, N_KEYS, tm), lambda i, e: (0, 0, i))
    return pl.pallas_call(
        functools.partial(_peer_dense_kernel, te=te),
        grid=(n // tm, ne // te),
        in_specs=[pl.BlockSpec((tm, d), lambda i, e: (i, 0)), _full((1, d)), blk, blk, blk, blk,
                  pl.BlockSpec((None, te, d), lambda i, e: (layer, e, 0)),
                  pl.BlockSpec((None, d, te), lambda i, e: (layer, 0, e))],
        out_specs=pl.BlockSpec((tm, d), lambda i, e: (i, 0)),
        out_shape=jax.ShapeDtypeStruct((n, d), F32),
        scratch_shapes=[pltpu.VMEM((d, tm), BF16), pltpu.VMEM((d, tm), F32), pltpu.VMEM((te, tm), F32),
                        pltpu.VMEM((te, tm), BF16), pltpu.VMEM((PEER_HEADS, N_KEYS, tm), BF16),
                        pltpu.VMEM((PEER_HEADS, N_KEYS, tm), BF16),
                        pltpu.VMEM((2 * PEER_HEADS * (te // N_KEYS), tm), F32)],
        compiler_params=_params(("parallel", "arbitrary")),
    )(x, g.reshape(1, d), *sel, u, v_t)


def kernel(x_prompt, x_sample, cache_win_k, cache_win_v, cache_conv, state_shift, state_wkv, cache_mem_k, cache_mem_v,
           mem_prompt, g_mix, w_in, att_sinks, w_a_out, conv_dw, conv_db, conv_ln_g, conv_ln_b, w_b_out, b_b_out,
           rk_mu, rk_w0, rk_w2, rk_a0, rk_a2, rk_g2, rk_kk, rk_ka, rk_rk, rk_gn_g, rk_gn_b, w_c_out, w_o,
           g_ca, g_mem, w_cq, w_mk, w_mv, w_co, g_ffn, w_pq, peer_keys, peer_u, peer_v, g_final):
    b, t, d = x_prompt.shape
    db, dt, _ = x_sample.shape
    depth = w_in.shape[0]
    wb = cache_win_k.shape[2]
    mem_len = mem_prompt.shape[1]
    n_p, n_s = b * t, db * dt
    assert t % WINDOW == 0 and wb == WINDOW and dt <= SUBLANES

    xp = x_prompt.reshape(n_p, d)
    xs = x_sample.reshape(n_s, d)
    tabs_p = _rotary_tables(jnp.tile(jnp.arange(t, dtype=F32), b))
    tabs_s = _rotary_tables(jnp.tile(PAST_LEN + jnp.arange(dt, dtype=F32), db))
    tq = _tile(t)
    new_pad = SUBLANES - dt
    u_bf = peer_u.astype(BF16)
    vt_bf = jnp.swapaxes(peer_v.astype(BF16), 1, 2)
    mem = mem_prompt.reshape(b * mem_len, d)
    mem_k_all = cache_mem_k.reshape(depth * db, mem_len, d)
    mem_v_all = cache_mem_v.reshape(depth * db, mem_len, d)
    time_major = lambda a: a.reshape(db, dt, a.shape[-1]).transpose(1, 0, 2)
    seq_major = lambda a: a.transpose(1, 0, 2).reshape(n_s, a.shape[-1])
    k4 = lambda a, nb_, tt_: a.reshape(nb_, tt_, ATT_KV_HEADS, HEAD_DIM)

    outs = [[] for _ in range(12)]
    for l in range(depth):
        lp = {'rk_mu': rk_mu[l], 'rk_w0': rk_w0[l], 'rk_w2': rk_w2[l], 'rk_a0': rk_a0[l], 'rk_a2': rk_a2[l],
              'rk_g2': rk_g2[l], 'rk_kk': rk_kk[l], 'rk_ka': rk_ka[l], 'rk_rk': rk_rk[l].reshape(-1),
              'rk_gn_g': rk_gn_g[l], 'rk_gn_b': rk_gn_b[l], 'w_a_out': w_a_out[l], 'w_b_out': w_b_out[l],
              'b_b_out': b_b_out[l], 'w_c_out': w_c_out[l], 'w_o': w_o[l]}
        win = w_in[l].astype(BF16)
        c0, c1, c2 = A_COLS, A_COLS + B_COLS, A_COLS + B_COLS + C_COLS
        w_seg = [win[:, :c0], win[:, c0:c1], win[:, c1:c2], win[:, c2:]]
        pa_p, pb_p, pc_p, pg_p = (norm_matmul(xp, g_mix[l], w) for w in w_seg)
        pa_s, pb_s, pc_s, pg_s = (norm_matmul(xs, g_mix[l], w) for w in w_seg)

        q_p, k_p = rotary_qk(pa_p, tabs_p)
        q_s, k_s = rotary_qk(pa_s, tabs_s)
        oa_p = swa_prompt(q_p, k_p, pa_p, att_sinks[l], b, t)
        qs4 = q_s.reshape(db, dt, ATT_KV_HEADS, ATT_GROUP, HEAD_DIM).transpose(0, 2, 3, 1, 4).reshape(
            db, ATT_KV_HEADS, ATT_GROUP * dt, HEAD_DIM)
        k_new = k_s.reshape(db, dt, KV_W)
        v_new = pa_s[:, Q_W + KV_W:].reshape(db, dt, KV_W)
        padn = lambda a: jnp.pad(a, ((0, 0), (0, new_pad), (0, 0)))
        sink_rows = jnp.broadcast_to(
            jnp.repeat(att_sinks[l].reshape(ATT_KV_HEADS, ATT_GROUP), dt, axis=1)[:, :, None],
            (ATT_KV_HEADS, ATT_GROUP * dt, LANES))
        oa_s = swa_sample(qs4, cache_win_k, cache_win_v, l, padn(k_new), padn(v_new), sink_rows)
        oa_s = oa_s.reshape(db, ATT_KV_HEADS, ATT_GROUP, dt, HEAD_DIM).transpose(0, 3, 1, 2, 4).reshape(n_s, Q_W)

        yb_p, conv_st = conv_prompt(pb_p, conv_dw[l], conv_db[l], conv_ln_g[l], conv_ln_b[l], b, t)
        yb_s, conv_new = conv_sample(cache_conv[l].transpose(1, 0, 2), time_major(pb_s),
                                     conv_dw[l], conv_db[l], conv_ln_g[l], conv_ln_b[l])
        yb_s = seq_major(yb_s)

        *x5_p, v_p, g_p, bonus_p = rwkv_pre_prompt(pc_p, lp, b, t)
        pre_s = [seq_major(a) for a in rwkv_pre_sample(time_major(pc_s), state_shift[l], lp)]
        *x5_s, v_s, g_s, bonus_s = pre_s
        oc_p, wkv_p = rwkv_scan(x5_p, v_p, jnp.zeros((b, RWKV_HEADS, RWKV_HEAD, RWKV_HEAD), F32), b, t,
                                skewed=scan_skewed(b, t))
        oc_s, wkv_s = rwkv_scan(x5_s, v_s, state_wkv[l], db, dt)

        xp = merge_out(xp, oa_p, yb_p, oc_p, bonus_p, g_p, pg_p, lp)
        xs = merge_out(xs, oa_s, yb_s, oc_s, bonus_s, g_s, pg_s, lp)

        mk = norm_matmul(mem, g_mem[l], w_mk[l].astype(BF16))
        mv = norm_matmul(mem, g_mem[l], w_mv[l].astype(BF16))
        wcq, wco = w_cq[l].astype(BF16), w_co[l].astype(BF16)
        ca_p = cross_attn(norm_matmul(xp, g_ca[l], wcq).reshape(n_p // tq, tq, d), mk.reshape(b, mem_len, d),
                          mv.reshape(b, mem_len, d), t // tq)
        ca_s = cross_attn(norm_matmul(xs, g_ca[l], wcq).reshape(db, dt, d), mem_k_all, mem_v_all, 1, first_seq=l * db,
                          per_step=2 if db % 2 == 0 else 1)
        xp = matmul_residual(ca_p.reshape(n_p, d), wco, xp)
        xs = matmul_residual(ca_s.reshape(n_s, d), wco, xs)

        wpq = w_pq[l].astype(BF16)
        keys = peer_keys[l].reshape(2 * PEER_HEADS, N_KEYS, PEER_HALF).astype(BF16)
        xp = peer_dense(xp, g_ffn[l], peer_select(xp, g_ffn[l], wpq, keys), u_bf, vt_bf, l)
        xs = peer_dense(xs, g_ffn[l], peer_select(xs, g_ffn[l], wpq, keys), u_bf, vt_bf, l)

        outs[0].append(k4(k_p, b, t)[:, t - wb:])
        outs[1].append(k4(pa_p[:, Q_W + KV_W:], b, t)[:, t - wb:])
        outs[2].append(conv_st[:, CONV_HALO - (CONV_WIDTH - 1):])
        outs[3].append(pc_p.reshape(b, t, C_COLS)[:, -1])
        outs[4].append(wkv_p)
        outs[5].append(mk.reshape(b, mem_len, CA_HEADS, CA_HEAD_DIM))
        outs[6].append(mv.reshape(b, mem_len, CA_HEADS, CA_HEAD_DIM))
        outs[7].append(jnp.concatenate([cache_win_k[l][:, dt:], k4(k_new, db, dt)], axis=1))
        outs[8].append(jnp.concatenate([cache_win_v[l][:, dt:], k4(v_new, db, dt)], axis=1))
        outs[9].append(conv_new.transpose(1, 0, 2))
        outs[10].append(pc_s.reshape(db, dt, C_COLS)[:, -1])
        outs[11].append(wkv_s)

    yp = final_norm(xp, g_final)
    ys = final_norm(xs, g_final)
    return (yp.reshape(b, t, d), ys.reshape(db, dt, d)) + tuple(jnp.stack(o) for o in outs)
```

```python
import functools

import numpy as np
import jax
import jax.numpy as jnp
from jax import lax
from jax.experimental import pallas as pl
from jax.experimental.pallas import tpu as pltpu

F32 = jnp.float32
BF16 = jnp.bfloat16

D_MODEL = 1024
PAST_LEN = 8192
ATT_HEADS = 8
ATT_KV_HEADS = 2
ATT_GROUP = ATT_HEADS // ATT_KV_HEADS
HEAD_DIM = 64
ROT_DIM = HEAD_DIM // 4
ROPE_THETA = 500000.0
WINDOW = 128
CONV_CH = 512
CONV_WIDTH = 31
LN_EPS = 1e-5
RWKV_HEADS = 8
RWKV_HEAD = 64
RWKV_W = RWKV_HEADS * RWKV_HEAD
DECAY_LORA = 64
ICLR_LORA = 64
GATE_LORA = 128
GN_EPS = 64e-5
CA_HEADS = 4
CA_HEAD_DIM = D_MODEL // CA_HEADS
PEER_HEADS = 8
N_KEYS = 128
PEER_TOPK = 16
PEER_HALF = 128
RMS_EPS = 1e-6
NEG = -1e30

Q_W = ATT_HEADS * HEAD_DIM
KV_W = ATT_KV_HEADS * HEAD_DIM
A_COLS = Q_W + 2 * KV_W
B_COLS = 2 * CONV_CH
C_COLS = 3 * RWKV_W + DECAY_LORA + ICLR_LORA + GATE_LORA
G_COLS = 3 * D_MODEL

VMEM_LIMIT_V7X = 56 * 1024 * 1024
LANES = 128
SUBLANES = 8
SCAN_PAIRS = 16
SCAN_TSUB = 8
NT_DIMS = (((1,), (1,)), ((), ()))


def _tile(n, prefs=(512, 256, 128, 64, 32, 16, 8)):
    for t in prefs:
        if n % t == 0:
            return t
    raise ValueError(f"no tile for {n}")


def _params(sem):
    return pltpu.CompilerParams(dimension_semantics=sem, vmem_limit_bytes=VMEM_LIMIT_V7X)


def _sigmoid(x):
    return 1.0 / (1.0 + jnp.exp(-x))


def _full(shape):
    nd = len(shape)
    return pl.BlockSpec(shape, lambda *_: (0,) * nd)


def _norm_matmul_kernel(x_ref, g_ref, w_ref, o_ref):
    x = x_ref[...]
    h = x * lax.rsqrt(jnp.mean(x * x, axis=-1, keepdims=True) + RMS_EPS) * g_ref[...]
    o_ref[...] = jnp.dot(h.astype(BF16), w_ref[...], preferred_element_type=F32)


def norm_matmul(x, g, w):
    n, d = x.shape
    c = w.shape[1]
    tm = _tile(n)
    return pl.pallas_call(
        _norm_matmul_kernel,
        grid=(n // tm,),
        in_specs=[pl.BlockSpec((tm, d), lambda i: (i, 0)), _full((1, d)), _full((d, c))],
        out_specs=pl.BlockSpec((tm, c), lambda i: (i, 0)),
        out_shape=jax.ShapeDtypeStruct((n, c), F32),
        compiler_params=_params(("parallel",)),
    )(x, g.reshape(1, d), w)


def _matmul_res_kernel(a_ref, w_ref, x_ref, o_ref):
    o_ref[...] = x_ref[...] + jnp.dot(a_ref[...].astype(BF16), w_ref[...], preferred_element_type=F32)


def matmul_residual(a, w, x):
    n, k = a.shape
    c = w.shape[1]
    tm = _tile(n)
    return pl.pallas_call(
        _matmul_res_kernel,
        grid=(n // tm,),
        in_specs=[pl.BlockSpec((tm, k), lambda i: (i, 0)), _full((k, c)), pl.BlockSpec((tm, c), lambda i: (i, 0))],
        out_specs=pl.BlockSpec((tm, c), lambda i: (i, 0)),
        out_shape=jax.ShapeDtypeStruct((n, c), F32),
        compiler_params=_params(("parallel",)),
    )(a, w, x)


def _final_norm_kernel(x_ref, g_ref, o_ref):
    x = x_ref[...]
    o_ref[...] = x * lax.rsqrt(jnp.mean(x * x, axis=-1, keepdims=True) + RMS_EPS) * g_ref[...]


def final_norm(x, g):
    n, d = x.shape
    tm = _tile(n)
    return pl.pallas_call(
        _final_norm_kernel,
        grid=(n // tm,),
        in_specs=[pl.BlockSpec((tm, d), lambda i: (i, 0)), _full((1, d))],
        out_specs=pl.BlockSpec((tm, d), lambda i: (i, 0)),
        out_shape=jax.ShapeDtypeStruct((n, d), F32),
        compiler_params=_params(("parallel",)),
    )(x, g.reshape(1, d))


def _rotary_tables(pos):
    half = ROT_DIM // 2
    inv = ROPE_THETA ** (-2.0 * jnp.arange(half, dtype=F32) / ROT_DIM)
    ang = pos[:, None] * inv[None, :]
    cos, sin = jnp.cos(ang), jnp.sin(ang)
    n = pos.shape[0]
    one = jnp.ones((n, HEAD_DIM - ROT_DIM), F32)
    zero = jnp.zeros((n, HEAD_DIM - ROT_DIM), F32)
    zh = jnp.zeros((n, half), F32)
    c = jnp.concatenate([cos, cos, one], axis=1)
    s_up = jnp.concatenate([-sin, zh, zero], axis=1)
    s_dn = jnp.concatenate([zh, sin, zero], axis=1)
    return tuple(jnp.concatenate([t, t], axis=1) for t in (c, s_up, s_dn))


def _rotary_kernel(pa_ref, c_ref, su_ref, sd_ref, q_ref, k_ref):
    c, su, sd = c_ref[...], su_ref[...], sd_ref[...]
    half = ROT_DIM // 2
    for j in range((Q_W + KV_W) // LANES):
        x = pa_ref[:, j * LANES:(j + 1) * LANES]
        y = x * c + pltpu.roll(x, LANES - half, 1) * su + pltpu.roll(x, half, 1) * sd
        if j < Q_W // LANES:
            q_ref[:, j * LANES:(j + 1) * LANES] = y
        else:
            k_ref[...] = y


def rotary_qk(pa, tabs):
    n = pa.shape[0]
    tm = _tile(n)
    row = lambda w: pl.BlockSpec((tm, w), lambda i: (i, 0))
    return pl.pallas_call(
        _rotary_kernel,
        grid=(n // tm,),
        in_specs=[row(A_COLS), row(LANES), row(LANES), row(LANES)],
        out_specs=[row(Q_W), row(KV_W)],
        out_shape=[jax.ShapeDtypeStruct((n, Q_W), F32), jax.ShapeDtypeStruct((n, KV_W), F32)],
        compiler_params=_params(("parallel",)),
    )(pa, *tabs)


def _sink_softmax_pv(parts, sink):
    m = sink
    for s, _ in parts:
        m = jnp.maximum(m, jnp.max(s, axis=-1, keepdims=True))
    ps = [jnp.exp(s - m) for s, _ in parts]
    den = jnp.exp(sink - m)
    for p in ps:
        den = den + jnp.sum(p, axis=-1, keepdims=True)
    inv = 1.0 / den
    out = None
    for p, (_, v) in zip(ps, parts):
        o = jnp.dot((p * inv).astype(BF16), v, preferred_element_type=F32)
        out = o if out is None else out + o
    return out


def _swa_prompt_kernel(sink_ref, q_ref, kc_ref, kp_ref, vc_ref, vp_ref, o_ref):
    n = pl.program_id(1)
    qi = lax.broadcasted_iota(jnp.int32, (WINDOW, WINDOW), 0)
    kj = lax.broadcasted_iota(jnp.int32, (WINDOW, WINDOW), 1)
    mask_c = kj <= qi
    mask_p = kj > qi + jnp.where(n > 0, 0, WINDOW)
    scale = HEAD_DIM ** -0.5
    lane = lax.broadcasted_iota(jnp.int32, (WINDOW, KV_W), 1)
    low = lane < HEAD_DIM

    def halves(x, g):
        other = pltpu.roll(x, HEAD_DIM, 1)
        in_low, in_high = (x, other) if g == 0 else (other, x)
        return jnp.where(low, in_low, 0.0).astype(BF16), jnp.where(low, 0.0, in_high).astype(BF16)

    mask = jnp.concatenate([mask_p, mask_c], axis=1)
    for g in range(ATT_KV_HEADS):
        kk = [jnp.concatenate([p, c], axis=0) for p, c in zip(halves(kp_ref[...], g), halves(kc_ref[...], g))]
        vv = [jnp.concatenate([p, c], axis=0) for p, c in zip(halves(vp_ref[...], g), halves(vc_ref[...], g))]
        for pair in range(ATT_GROUP // 2):
            col = (g * (ATT_GROUP // 2) + pair) * KV_W
            q2 = q_ref[:, col:col + KV_W].astype(BF16)
            out = None
            for odd in range(2):
                s = lax.dot_general(q2, kk[odd], NT_DIMS, preferred_element_type=F32) * scale
                s = jnp.where(mask, s, NEG)
                o = _sink_softmax_pv([(s, vv[odd])], sink_ref[col // HEAD_DIM + odd])
                out = o if out is None else out + o
            o_ref[:, col:col + KV_W] = out


def swa_prompt(q, k, pa, sinks, b, t):
    nb = t // WINDOW
    cur = lambda w, c: pl.BlockSpec((WINDOW, w), lambda bi, ni: (bi * nb + ni, c))
    prev = lambda w, c: pl.BlockSpec((WINDOW, w), lambda bi, ni: (bi * nb + jnp.maximum(ni - 1, 0), c))
    vcol = (Q_W + KV_W) // KV_W
    return pl.pallas_call(
        _swa_prompt_kernel,
        grid=(b, nb),
        in_specs=[pl.BlockSpec(memory_space=pltpu.SMEM), cur(Q_W, 0), cur(KV_W, 0), prev(KV_W, 0),
                  cur(KV_W, vcol), prev(KV_W, vcol)],
        out_specs=cur(Q_W, 0),
        out_shape=jax.ShapeDtypeStruct((b * t, Q_W), F32),
        compiler_params=_params(("parallel", "parallel")),
    )(sinks, q, k, k, pa, pa)


def _swa_sample_kernel(sink_ref, q_ref, kb_ref, vb_ref, kn_ref, vn_ref, o_ref, *, bt, dt, wb):
    rows = ATT_GROUP * dt
    tb = lax.rem(lax.broadcasted_iota(jnp.int32, (rows, wb), 0), dt)
    jb = lax.broadcasted_iota(jnp.int32, (rows, wb), 1)
    mask_b = jb > tb + (wb - WINDOW)
    npad = kn_ref.shape[1]
    tn = lax.rem(lax.broadcasted_iota(jnp.int32, (rows, npad), 0), dt)
    jn = lax.broadcasted_iota(jnp.int32, (rows, npad), 1)
    mask_n = jn <= tn
    scale = HEAD_DIM ** -0.5
    for b in range(bt):
        for g in range(ATT_KV_HEADS):
            gs = slice(g * HEAD_DIM, (g + 1) * HEAD_DIM)
            q = q_ref[b, g].astype(BF16)
            sb = lax.dot_general(q, kb_ref[b, :, g, :].astype(BF16), NT_DIMS, preferred_element_type=F32) * scale
            sn = lax.dot_general(q, kn_ref[b, :, gs].astype(BF16), NT_DIMS, preferred_element_type=F32) * scale
            sb = jnp.where(mask_b, sb, NEG)
            sn = jnp.where(mask_n, sn, NEG)
            o_ref[b, g] = _sink_softmax_pv(
                [(sb, vb_ref[b, :, g, :].astype(BF16)), (sn, vn_ref[b, :, gs].astype(BF16))], sink_ref[g][:, :1])


def swa_sample(qs, cache_k, cache_v, layer, knew, vnew, sink_rows):
    db, _, rows, _ = qs.shape
    dt = rows // ATT_GROUP
    wb = cache_k.shape[2]
    bt = _tile(db, (8, 4, 2, 1))
    blk = lambda a: pl.BlockSpec((bt,) + a.shape[1:], lambda i: (i,) + (0,) * (a.ndim - 1))
    cache = pl.BlockSpec((None, bt) + cache_k.shape[2:], lambda i: (layer, i, 0, 0, 0))
    return pl.pallas_call(
        functools.partial(_swa_sample_kernel, bt=bt, dt=dt, wb=wb),
        grid=(db // bt,),
        in_specs=[_full(sink_rows.shape), blk(qs), cache, cache, blk(knew), blk(vnew)],
        out_specs=blk(qs),
        out_shape=jax.ShapeDtypeStruct(qs.shape, F32),
        compiler_params=_params(("parallel",)),
    )(sink_rows, qs, cache_k, cache_v, knew, vnew)


CONV_HALO = 32


def _ln_swish(y, lg, lb):
    mu = jnp.mean(y, axis=-1, keepdims=True)
    var = jnp.mean(jnp.square(y - mu), axis=-1, keepdims=True)
    yn = (y - mu) * lax.rsqrt(var + LN_EPS) * lg + lb
    return yn * _sigmoid(yn)


def _conv_prompt_kernel(pb_ref, dw_ref, db_ref, lg_ref, lb_ref, y_ref, st_ref, ext_ref, sh_ref, *, tt):
    i = pl.program_id(1)

    @pl.when(i == 0)
    def _():
        ext_ref[0:CONV_HALO, :] = jnp.zeros((CONV_HALO, CONV_CH), F32)

    @pl.when(i > 0)
    def _():
        ext_ref[0:CONV_HALO, :] = ext_ref[tt:tt + CONV_HALO, :]

    pb = pb_ref[...]
    ext_ref[CONV_HALO:CONV_HALO + tt, :] = pb[:, :CONV_CH] * _sigmoid(pb[:, CONV_CH:])
    acc = jnp.zeros((tt, CONV_CH), F32) + db_ref[...]
    first = CONV_HALO - (CONV_WIDTH - 1)
    span = tt + CONV_HALO - SUBLANES
    for r in range(SUBLANES):
        taps = [j for j in range(CONV_WIDTH) if (first + j) % SUBLANES == r]
        if r:
            sh_ref[0:span, :] = ext_ref[pl.ds(r, span), :]
        src = sh_ref if r else ext_ref
        for j in taps:
            q = first + j - r
            acc = acc + src[q:q + tt, :] * dw_ref[j:j + 1, :]
    y_ref[...] = _ln_swish(acc, lg_ref[...], lb_ref[...])

    @pl.when(i == pl.num_programs(1) - 1)
    def _():
        st_ref[0] = ext_ref[tt:tt + CONV_HALO, :]


def conv_prompt(pb, dw, db, lg, lb, b, t):
    tt = _tile(t)
    nt = t // tt
    vec = lambda a: a.reshape(1, CONV_CH)
    return pl.pallas_call(
        functools.partial(_conv_prompt_kernel, tt=tt),
        grid=(b, nt),
        in_specs=[pl.BlockSpec((tt, B_COLS), lambda bi, i: (bi * nt + i, 0)), _full((CONV_WIDTH, CONV_CH)),
                  _full((1, CONV_CH)), _full((1, CONV_CH)), _full((1, CONV_CH))],
        out_specs=[pl.BlockSpec((tt, CONV_CH), lambda bi, i: (bi * nt + i, 0)),
                   pl.BlockSpec((1, CONV_HALO, CONV_CH), lambda bi, i: (bi, 0, 0))],
        out_shape=[jax.ShapeDtypeStruct((b * t, CONV_CH), F32), jax.ShapeDtypeStruct((b, CONV_HALO, CONV_CH), F32)],
        scratch_shapes=[pltpu.VMEM((tt + CONV_HALO, CONV_CH), F32), pltpu.VMEM((tt + CONV_HALO, CONV_CH), F32)],
        compiler_params=_params(("arbitrary", "arbitrary")),
    )(pb, dw, vec(db), vec(lg), vec(lb))


def _conv_sample_kernel(c_ref, pb_ref, dw_ref, db_ref, lg_ref, lb_ref, y_ref, nc_ref, *, dt):
    nprev = CONV_WIDTH - 1
    us = []
    for t in range(dt):
        pb = pb_ref[t]
        us.append(pb[:, :CONV_CH] * _sigmoid(pb[:, CONV_CH:]))

    def ext(j):
        return c_ref[j] if j < nprev else us[j - nprev]

    for t in range(dt):
        acc = db_ref[...] + ext(t) * dw_ref[0:1, :]
        for j in range(1, CONV_WIDTH):
            acc = acc + ext(t + j) * dw_ref[j:j + 1, :]
        y_ref[t] = _ln_swish(acc, lg_ref[...], lb_ref[...])
    for j in range(nprev):
        nc_ref[j] = ext(j + dt)


def conv_sample(cache_t, pb_t, dw, db, lg, lb):
    nprev, dbt, _ = cache_t.shape
    dt = pb_t.shape[0]
    bt = _tile(dbt, (32, 16, 8))
    vec = lambda a: a.reshape(1, CONV_CH)
    blk = lambda lead, w: pl.BlockSpec((lead, bt, w), lambda i: (0, i, 0))
    return pl.pallas_call(
        functools.partial(_conv_sample_kernel, dt=dt),
        grid=(dbt // bt,),
        in_specs=[blk(nprev, CONV_CH), blk(dt, B_COLS), _full((CONV_WIDTH, CONV_CH)), _full((1, CONV_CH)),
                  _full((1, CONV_CH)), _full((1, CONV_CH))],
        out_specs=[blk(dt, CONV_CH), blk(nprev, CONV_CH)],
        out_shape=[jax.ShapeDtypeStruct((dt, dbt, CONV_CH), F32), jax.ShapeDtypeStruct((nprev, dbt, CONV_CH), F32)],
        compiler_params=_params(("parallel",)),
    )(cache_t, pb_t, dw, vec(db), vec(lg), vec(lb))


def _head_sum_matrix():
    h = np.arange(RWKV_W) // RWKV_HEAD
    return jnp.asarray((h[:, None] == h[None, :]).astype(np.float32), dtype=BF16)


def _head_sum(x, hs_ref):
    hi = x.astype(BF16)
    lo = (x - hi.astype(F32)).astype(BF16)
    hs = hs_ref[...]
    return jnp.dot(hi, hs, preferred_element_type=F32) + jnp.dot(lo, hs, preferred_element_type=F32)


def _rwkv_pre_math(pc, prev, c):
    mu_ref, w0_ref, w2_ref, a0_ref, a2_ref, g2_ref, kkp_ref, ka_ref, rk_ref, hs_ref = c
    xs = pc + (prev - pc) * mu_ref[...]
    o1, o2, o3 = RWKV_W, 2 * RWKV_W, 3 * RWKV_W
    o4 = o3 + DECAY_LORA
    o5 = o4 + ICLR_LORA
    r, k, v = xs[:, :o1], xs[:, o1:o2], xs[:, o2:o3]
    wd, ad, gd = xs[:, o3:o4], xs[:, o4:o5], xs[:, o5:]
    y = -(w0_ref[...] + jnp.dot(jnp.tanh(wd).astype(BF16), w2_ref[...], preferred_element_type=F32))
    softplus = jnp.maximum(y, 0.0) + jnp.log(1.0 + jnp.exp(-jnp.abs(y)))
    decay = jnp.exp(-jnp.exp(-softplus - 0.5))
    a = _sigmoid(a0_ref[...] + jnp.dot(ad.astype(BF16), a2_ref[...], preferred_element_type=F32))
    g = jnp.dot(_sigmoid(gd).astype(BF16), g2_ref[...], preferred_element_type=F32)
    kk = k * kkp_ref[...]
    norm = jnp.sqrt(_head_sum(kk * kk, hs_ref))
    kk = kk / jnp.maximum(norm, 1e-12)
    k_mod = k * (1.0 + (a - 1.0) * ka_ref[...])
    bonus = _head_sum(r * k_mod * rk_ref[...], hs_ref) * v
    return r, decay, k_mod, -kk, kk * a, v, g, bonus


RWKV_PRE_OUTS = 8
SCAN_KEY_OPERANDS = 5
SCAN_CHUNK = 256
SCAN_GROUP_UNROLL = 1
SHIFT_HALO = SUBLANES


def _rwkv_pre_prompt_kernel(pc_ref, *refs, tt, skew):
    consts, outs, ext_ref = refs[:10], refs[10:10 + RWKV_PRE_OUTS], refs[-1]
    i = pl.program_id(1)

    @pl.when(i == 0)
    def _():
        ext_ref[0:SHIFT_HALO, :] = jnp.zeros((SHIFT_HALO, C_COLS), F32)

    @pl.when(i > 0)
    def _():
        ext_ref[0:SHIFT_HALO, :] = ext_ref[tt:tt + SHIFT_HALO, :]

    pc = pc_ref[...]
    ext_ref[SHIFT_HALO:SHIFT_HALO + tt, :] = pc
    res = _rwkv_pre_math(pc, ext_ref[pl.ds(SHIFT_HALO - 1, tt), :], consts)
    for j, (o_ref, val) in enumerate(zip(outs, res)):
        if j < SCAN_KEY_OPERANDS and skew:
            vt = val.T
            for h in range(RWKV_HEADS):
                g = pl.program_id(0) * RWKV_HEADS + h
                rows = slice(h * RWKV_HEAD, (h + 1) * RWKV_HEAD)
                o_ref[rows, :] = pltpu.roll(vt[rows, :], g * SCAN_TSUB, 1)
        else:
            o_ref[...] = val


def _rwkv_pre_sample_kernel(pc_ref, shift_ref, *refs, dt):
    consts, outs = refs[:10], refs[10:10 + RWKV_PRE_OUTS]
    for t in range(dt):
        res = _rwkv_pre_math(pc_ref[t], shift_ref[...] if t == 0 else pc_ref[t - 1], consts)
        for o_ref, val in zip(outs, res):
            o_ref[t] = val


def _rwkv_pre_consts(lp):
    vec = lambda a: a.reshape(1, -1)
    return [vec(lp['rk_mu']), vec(lp['rk_w0']), lp['rk_w2'].astype(BF16), vec(lp['rk_a0']), lp['rk_a2'].astype(BF16),
            lp['rk_g2'].astype(BF16), vec(lp['rk_kk']), vec(lp['rk_ka']), vec(lp['rk_rk']), _head_sum_matrix()]


def scan_skewed(b, t):
    return b * RWKV_HEADS == SCAN_PAIRS and t % SCAN_CHUNK == 0


def rwkv_pre_prompt(pc, lp, b, t):
    skew = scan_skewed(b, t)
    tt = SCAN_CHUNK if skew else _tile(t, (256, 128))
    nt = t // tt
    consts = _rwkv_pre_consts(lp)
    row = lambda w: pl.BlockSpec((tt, w), lambda bi, i: (bi * nt + i, 0))
    nkey = SCAN_KEY_OPERANDS if skew else 0
    col = pl.BlockSpec((RWKV_W, tt), lambda bi, i: (bi, i))
    return pl.pallas_call(
        functools.partial(_rwkv_pre_prompt_kernel, tt=tt, skew=skew),
        grid=(b, nt),
        in_specs=[row(C_COLS)] + [_full(c.shape) for c in consts],
        out_specs=[col] * nkey + [row(RWKV_W)] * (RWKV_PRE_OUTS - nkey),
        out_shape=[jax.ShapeDtypeStruct((b * RWKV_W, t), F32)] * nkey
        + [jax.ShapeDtypeStruct((b * t, RWKV_W), F32)] * (RWKV_PRE_OUTS - nkey),
        scratch_shapes=[pltpu.VMEM((tt + SHIFT_HALO, C_COLS), F32)],
        compiler_params=_params(("arbitrary", "arbitrary")),
    )(pc, *consts)


def rwkv_pre_sample(pc_t, shift, lp):
    dt, dbt, _ = pc_t.shape
    bt = _tile(dbt, (64, 32, 16, 8))
    consts = _rwkv_pre_consts(lp)
    blk = lambda w: pl.BlockSpec((dt, bt, w), lambda i: (0, i, 0))
    return pl.pallas_call(
        functools.partial(_rwkv_pre_sample_kernel, dt=dt),
        grid=(dbt // bt,),
        in_specs=[blk(C_COLS), pl.BlockSpec((bt, C_COLS), lambda i: (i, 0))] + [_full(c.shape) for c in consts],
        out_specs=[blk(RWKV_W)] * RWKV_PRE_OUTS,
        out_shape=[jax.ShapeDtypeStruct((dt, dbt, RWKV_W), F32)] * RWKV_PRE_OUTS,
        compiler_params=_params(("parallel",)),
    )(pc_t, shift, *consts)


def _scan_select_matrices(pair_major):
    e = np.zeros((SCAN_TSUB, LANES, LANES), np.float32)
    vl = LANES // SCAN_PAIRS
    for tl in range(SCAN_TSUB):
        for g in range(SCAN_PAIRS):
            src = g * SCAN_TSUB + tl if pair_major else tl * SCAN_PAIRS + g
            e[tl, src, g * vl:(g + 1) * vl] = 1.0
    return jnp.asarray(np.concatenate([e, e, e], axis=1), dtype=BF16)


def _sublane_allsum(x):
    x = x + pltpu.roll(x, 4, 0)
    x = x + pltpu.roll(x, 2, 0)
    return x + pltpu.roll(x, 1, 0)


def _scan_kernel(xr_ref, xw_ref, xk_ref, xa_ref, xb_ref, v_ref, s0_ref, e_ref, o_ref, sf_ref, z_ref, tile_ref, xs_ref,
                 *, groups, steps, skewed):
    x_refs = (xr_ref, xw_ref, xk_ref, xa_ref, xb_ref)
    c = pl.program_id(1)
    nvh = z_ref.shape[0]
    nkb = RWKV_HEAD // SUBLANES
    sub = lax.broadcasted_iota(jnp.int32, (SUBLANES, LANES), 0)
    R_ROW, W_ROW, K_ROW, A_ROW, B_ROW = (j * RWKV_HEAD for j in range(5))

    @pl.when(c == 0)
    def _():
        z_ref[...] = s0_ref[0]

    lane_group = lax.broadcasted_iota(jnp.int32, (SUBLANES, LANES), 1) // SCAN_TSUB

    def gather_group(x_ref, th, kb):
        acc = jnp.zeros((SUBLANES, LANES), F32)
        for g in range(SCAN_PAIRS):
            pos = th + g
            tile = pl.multiple_of(((pos // SCAN_PAIRS) % 2) * LANES, LANES)
            rows = slice(g * RWKV_HEAD + kb * SUBLANES, g * RWKV_HEAD + (kb + 1) * SUBLANES)
            acc = jnp.where(lane_group == pos % SCAN_PAIRS, x_ref[rows, pl.ds(tile, LANES)], acc)
        return pltpu.roll(acc, ((SCAN_PAIRS - th % SCAN_PAIRS) % SCAN_PAIRS) * SCAN_TSUB, 1)

    def split_group(th):
        th = jnp.asarray(th, jnp.int32)
        for j, x_ref in enumerate(x_refs):
            rows = slice(j * RWKV_HEAD, (j + 1) * RWKV_HEAD)
            if skewed:
                xg = jnp.concatenate([gather_group(x_ref, th, kb) for kb in range(nkb)], axis=0)
            else:
                xg = x_ref[0, th]
            hi = xg.astype(BF16)
            r1 = xg - hi.astype(F32)
            mid = r1.astype(BF16)
            lo = (r1 - mid.astype(F32)).astype(BF16)
            xs_ref[rows, 0:LANES] = hi
            xs_ref[rows, LANES:2 * LANES] = mid
            xs_ref[rows, 2 * LANES:3 * LANES] = lo

    def spread(tl, buf):
        tile_ref[buf] = jnp.dot(xs_ref[...], e_ref[tl], preferred_element_type=F32)

    split_group(0)
    spread(0, 0)

    def group_steps(th):
        for tl in range(steps):
            cur = tl % 2
            t = th * steps + tl
            vrow = v_ref[0, t]
            blk = lambda row0, kb: tile_ref[cur, row0 + kb * SUBLANES:row0 + (kb + 1) * SUBLANES, :]
            sa = []
            for vh in range(nvh):
                acc = None
                for kb in range(nkb):
                    p = z_ref[vh, kb * SUBLANES:(kb + 1) * SUBLANES, :] * blk(A_ROW, kb)
                    acc = p if kb == 0 else acc + p
                sa.append(_sublane_allsum(acc))
            if tl + 1 < steps:
                spread(tl + 1, 1 - cur)
            else:
                split_group(jnp.minimum(th + 1, groups - 1))
                spread(0, 1 - cur)
            orow = jnp.zeros((SUBLANES, LANES), F32)
            for vh in range(nvh):
                vb = jnp.broadcast_to(vrow[vh:vh + 1], (SUBLANES, LANES))
                acc = None
                for kb in range(nkb):
                    ks = slice(kb * SUBLANES, (kb + 1) * SUBLANES)
                    zn = z_ref[vh, ks, :] * blk(W_ROW, kb) + blk(B_ROW, kb) * sa[vh] + blk(K_ROW, kb) * vb
                    z_ref[vh, ks, :] = zn
                    p = zn * blk(R_ROW, kb)
                    acc = p if kb == 0 else acc + p
                orow = jnp.where(sub == vh, _sublane_allsum(acc), orow)
            o_ref[0, t] = orow

    unroll = SCAN_GROUP_UNROLL if groups % SCAN_GROUP_UNROLL == 0 else 1

    def body(i, carry):
        for u in range(unroll):
            group_steps(i * unroll + u)
        return carry

    lax.fori_loop(0, groups // unroll, body, 0)

    @pl.when(c == pl.num_programs(1) - 1)
    def _():
        sf_ref[0] = z_ref[...]


def rwkv_scan(x5, v, s0, nseq, t, skewed=False):
    spb = SCAN_PAIRS // RWKV_HEADS
    assert nseq % spb == 0
    nblk = nseq // spb
    vl = LANES // SCAN_PAIRS
    vh = RWKV_HEAD // vl
    steps = min(SCAN_TSUB, t)
    assert t % steps == 0 and steps % 2 == 0
    t8 = t // steps
    def key_layout(a):
        a = a.reshape(nblk, spb, t8, steps, RWKV_HEADS, RWKV_HEAD)
        if steps < SCAN_TSUB:
            a = jnp.pad(a, ((0, 0),) * 3 + ((0, SCAN_TSUB - steps),) + ((0, 0),) * 2)
        return a.transpose(0, 2, 5, 3, 1, 4).reshape(nblk, t8, RWKV_HEAD, LANES)

    vk = v.reshape(nblk, spb, t, RWKV_HEADS, vh, vl).transpose(0, 2, 4, 1, 3, 5).reshape(nblk, t, vh, LANES)
    sk = s0.reshape(nblk, spb, RWKV_HEADS, vh, vl, RWKV_HEAD).transpose(0, 3, 5, 1, 2, 4).reshape(
        nblk, vh, RWKV_HEAD, LANES)
    if skewed:
        assert steps == SCAN_TSUB and t % SCAN_CHUNK == 0
        xk = x5
        groups = SCAN_CHUNK // steps
        key_spec = pl.BlockSpec((SCAN_PAIRS * RWKV_HEAD, SCAN_CHUNK), lambda i, c: (i, c))
    else:
        xk = [key_layout(a) for a in x5]
        groups = _tile(t8, (32, 16, 8, 4, 2, 1))
        key_spec = pl.BlockSpec((1, groups, RWKV_HEAD, LANES), lambda i, c: (i, c, 0, 0))
    nchunk = t8 // groups
    ch = groups * steps
    o, sf = pl.pallas_call(
        functools.partial(_scan_kernel, groups=groups, steps=steps, skewed=skewed),
        grid=(nblk, nchunk),
        in_specs=[key_spec] * 5 + [
                  pl.BlockSpec((1, ch, vh, LANES), lambda i, c: (i, c, 0, 0)),
                  pl.BlockSpec((1, vh, RWKV_HEAD, LANES), lambda i, c: (i, 0, 0, 0)),
                  _full((SCAN_TSUB, 3 * LANES, LANES))],
        out_specs=[pl.BlockSpec((1, ch, vh, LANES), lambda i, c: (i, c, 0, 0)),
                   pl.BlockSpec((1, vh, RWKV_HEAD, LANES), lambda i, c: (i, 0, 0, 0))],
        out_shape=[jax.ShapeDtypeStruct((nblk, t, vh, LANES), F32),
                   jax.ShapeDtypeStruct((nblk, vh, RWKV_HEAD, LANES), F32)],
        scratch_shapes=[pltpu.VMEM((vh, RWKV_HEAD, LANES), F32), pltpu.VMEM((2, 5 * RWKV_HEAD, LANES), F32),
                        pltpu.VMEM((5 * RWKV_HEAD, 3 * LANES), BF16)],
        compiler_params=_params(("arbitrary", "arbitrary")),
    )(*xk, vk, sk, _scan_select_matrices(pair_major=skewed))
    o = o.reshape(nblk, t, vh, spb, RWKV_HEADS, vl).transpose(0, 3, 1, 4, 2, 5).reshape(nseq * t, RWKV_W)
    sf = sf.reshape(nblk, vh, RWKV_HEAD, spb, RWKV_HEADS, vl).transpose(0, 3, 4, 1, 5, 2).reshape(
        nseq, RWKV_HEADS, RWKV_HEAD, RWKV_HEAD)
    return o, sf


def _merge_kernel(x_ref, oa_ref, yb_ref, oc_ref, bonus_ref, g_ref, pg_ref, wa_ref, wb_ref, bb_ref, wc_ref, wo_ref,
                  gng_ref, gnb_ref, hs_ref, o_ref):
    oc = oc_ref[...]
    mu = _head_sum(oc, hs_ref) * (1.0 / RWKV_HEAD)
    dev = oc - mu
    var = _head_sum(dev * dev, hs_ref) * (1.0 / RWKV_HEAD)
    on = dev * lax.rsqrt(var + GN_EPS) * gng_ref[...] + gnb_ref[...]
    yc = (on + bonus_ref[...]) * g_ref[...]
    d = D_MODEL
    dot = lambda a, w: jnp.dot(a.astype(BF16), w[...], preferred_element_type=F32)
    merged = (_sigmoid(pg_ref[:, 0:d]) * dot(oa_ref[...], wa_ref)
              + _sigmoid(pg_ref[:, d:2 * d]) * (dot(yb_ref[...], wb_ref) + bb_ref[...])
              + _sigmoid(pg_ref[:, 2 * d:3 * d]) * dot(yc, wc_ref))
    o_ref[...] = x_ref[...] + dot(merged, wo_ref)


def merge_out(x, oa, yb, oc, bonus, g, pg, lp):
    n = x.shape[0]
    tm = _tile(n, (256, 128, 64, 32, 16, 8))
    row = lambda w: pl.BlockSpec((tm, w), lambda i: (i, 0))
    vec = lambda a: a.reshape(1, -1)
    consts = [lp['w_a_out'].astype(BF16), lp['w_b_out'].astype(BF16), vec(lp['b_b_out']), lp['w_c_out'].astype(BF16),
              lp['w_o'].astype(BF16), vec(lp['rk_gn_g']), vec(lp['rk_gn_b']), _head_sum_matrix()]
    return pl.pallas_call(
        _merge_kernel,
        grid=(n // tm,),
        in_specs=[row(D_MODEL), row(Q_W), row(CONV_CH), row(RWKV_W), row(RWKV_W), row(RWKV_W), row(G_COLS)]
        + [_full(c.shape) for c in consts],
        out_specs=row(D_MODEL),
        out_shape=jax.ShapeDtypeStruct((n, D_MODEL), F32),
        compiler_params=_params(("parallel",)),
    )(x, oa, yb, oc, bonus, g, pg, *consts)


def _cross_attn_kernel(q_ref, k_ref, v_ref, o_ref):
    scale = CA_HEAD_DIM ** -0.5
    for b in range(q_ref.shape[0]):
        outs = []
        for h in range(CA_HEADS):
            hs = slice(h * CA_HEAD_DIM, (h + 1) * CA_HEAD_DIM)
            q = q_ref[b, :, hs].astype(BF16)
            s = lax.dot_general(q, k_ref[b, :, hs].astype(BF16), NT_DIMS, preferred_element_type=F32) * scale
            m = jnp.max(s, axis=-1, keepdims=True)
            p = jnp.exp(s - m)
            p = p / jnp.sum(p, axis=-1, keepdims=True)
            outs.append(jnp.dot(p.astype(BF16), v_ref[b, :, hs].astype(BF16), preferred_element_type=F32))
        o_ref[b] = jnp.concatenate(outs, axis=1)


def cross_attn(q, mk, mv, tiles_per_seq, first_seq=0, per_step=1):
    nt, tq, d = q.shape
    m = mk.shape[1]
    assert per_step == 1 or (tiles_per_seq == 1 and nt % per_step == 0 and first_seq % per_step == 0)
    nt //= per_step
    kv = pl.BlockSpec((per_step, m, d), lambda i: (first_seq // per_step + i // tiles_per_seq, 0, 0))
    qs = pl.BlockSpec((per_step, tq, d), lambda i: (i, 0, 0))
    return pl.pallas_call(
        _cross_attn_kernel,
        grid=(nt,),
        in_specs=[qs, kv, kv],
        out_specs=qs,
        out_shape=jax.ShapeDtypeStruct(q.shape, F32),
        compiler_params=_params(("parallel",)),
    )(q, mk, mv)


def _staircase():
    return [(i, PEER_TOPK // (i + 1)) for i in range(PEER_TOPK)]


PEER_CAND = sum(nj for _, nj in _staircase())
PEER_CAND_PAD = -(-PEER_CAND // SUBLANES) * SUBLANES


def _extract_topk(cur, out_ref, base, with_rank=False):
    rank = jnp.full(cur.shape, float(PEER_TOPK), F32) if with_rank else None
    for k in range(PEER_TOPK):
        m = jnp.max(cur, axis=0, keepdims=True)
        out_ref[base + k:base + k + 1, :] = m
        hit = cur == m
        if with_rank:
            rank = jnp.where(hit, float(k), rank)
        if k + 1 < PEER_TOPK:
            cur = jnp.where(hit, -jnp.inf, cur)
    return rank


def _peer_select_kernel(x_ref, g_ref, wq_ref, keys_ref, r1_ref, rho_ref, m2_ref, e2_ref, qp_ref, s_ref, tv_ref,
                        cand_ref, best_ref):
    nsub = 2 * PEER_HEADS
    x = x_ref[...]
    xn = x * lax.rsqrt(jnp.mean(x * x, axis=-1, keepdims=True) + RMS_EPS) * g_ref[...]
    qp_ref[...] = jnp.dot(xn.astype(BF16), wq_ref[...], preferred_element_type=F32)
    for hc in range(nsub):
        q = qp_ref[:, hc * PEER_HALF:(hc + 1) * PEER_HALF].astype(BF16)
        st = lax.dot_general(keys_ref[hc], q, NT_DIMS, preferred_element_type=F32)
        s_ref[hc] = st
        rank = _extract_topk(st, tv_ref, hc * PEER_TOPK, with_rank=hc % 2 == 0)
        if hc % 2 == 0:
            r1_ref[hc // 2] = rank
    tm = qp_ref.shape[0]
    for h in range(PEER_HEADS):
        b1 = 2 * h * PEER_TOPK
        b2 = b1 + PEER_TOPK
        cand_ref[PEER_CAND_PAD - SUBLANES:PEER_CAND_PAD, :] = jnp.full((SUBLANES, tm), -jnp.inf, F32)
        off = 0
        for i, nj in _staircase():
            cand_ref[off:off + nj, :] = tv_ref[b1 + i:b1 + i + 1, :] + tv_ref[b2:b2 + nj, :]
            off += nj
        _extract_topk(cand_ref[...], best_ref, 0)
        best = best_ref[...]
        z = jnp.sum(jnp.exp(best - best[0:1, :]), axis=0, keepdims=True)
        thr = best[PEER_TOPK - 1:PEER_TOPK, :]
        s2 = s_ref[2 * h + 1]
        m2 = jnp.zeros(s2.shape, F32)
        for i in range(PEER_TOPK):
            m2 = jnp.where((tv_ref[b1 + i:b1 + i + 1, :] + s2) >= thr, float(i + 1), m2)
        m2_ref[h] = m2
        e2_ref[h] = jnp.exp(s2 - tv_ref[b2:b2 + 1, :])
        rho_ref[h] = jnp.exp(s_ref[2 * h] - tv_ref[b1:b1 + 1, :]) * (1.0 / z)


def peer_select(x, g, wq, keys):
    n, d = x.shape
    tm = _tile(n, (256, 128))
    nsub = 2 * PEER_HEADS
    blk = pl.BlockSpec((PEER_HEADS, N_KEYS, tm), lambda i: (0, 0, i))
    shp = lambda dt: jax.ShapeDtypeStruct((PEER_HEADS, N_KEYS, n), dt)
    return pl.pallas_call(
        _peer_select_kernel,
        grid=(n // tm,),
        in_specs=[pl.BlockSpec((tm, d), lambda i: (i, 0)), _full((1, d)), _full(wq.shape), _full(keys.shape)],
        out_specs=[blk, blk, blk, blk],
        out_shape=[shp(F32)] * 4,
        scratch_shapes=[pltpu.VMEM((tm, nsub * PEER_HALF), F32), pltpu.VMEM((nsub, N_KEYS, tm), F32),
                        pltpu.VMEM((nsub * PEER_TOPK, tm), F32), pltpu.VMEM((PEER_CAND_PAD, tm), F32),
                        pltpu.VMEM((PEER_TOPK, tm), F32)],
        compiler_params=_params(("parallel",)),
    )(x, g.reshape(1, d), wq, keys)


PEER_PACK_ROWS = 16
PEER_EXPERT_TILE = 8 * N_KEYS


def _gelu_tanh(x):
    return 0.5 * x * (1.0 + jnp.tanh(0.7978845608028654 * (x + 0.044715 * (x * x * x))))


def _peer_dense_kernel(x_ref, g_ref, r1_ref, rho_ref, m2_ref, e2_ref, u_ref, vt_ref, o_ref, xnt_ref, acc_ref, ht_ref,
                       wh_ref, m2b_ref, e2b_ref, rows_ref, *, te):
    e = pl.program_id(1)
    tm = x_ref.shape[0]
    pk = PEER_PACK_ROWS

    @pl.when(e == 0)
    def _():
        x = x_ref[...]
        xn = x * lax.rsqrt(jnp.mean(x * x, axis=-1, keepdims=True) + RMS_EPS) * g_ref[...]
        xnt_ref[...] = xn.T.astype(BF16)
        acc_ref[...] = jnp.zeros_like(acc_ref)
        m2b_ref[...] = m2_ref[...].astype(BF16)
        e2b_ref[...] = e2_ref[...].astype(BF16)

    n_a = te // N_KEYS
    for al in range(n_a):
        a = e * n_a + al
        for h in range(PEER_HEADS):
            row = 2 * (al * PEER_HEADS + h)
            rows_ref[row:row + 1, :] = r1_ref[h, pl.ds(a, 1), :]
            rows_ref[row + 1:row + 2, :] = rho_ref[h, pl.ds(a, 1), :]

    ht_ref[...] = jnp.dot(u_ref[...], xnt_ref[...], preferred_element_type=F32)
    zero = jnp.zeros((), BF16)
    for al in range(n_a):
        for cl in range(tm // LANES):
            cs = slice(cl * LANES, (cl + 1) * LANES)
            packed = lambda row: jnp.broadcast_to(rows_ref[row:row + 1, cs], (pk, LANES)).astype(BF16)
            r1_rows = [packed(2 * (al * PEER_HEADS + h)) for h in range(PEER_HEADS)]
            rho_rows = [packed(2 * (al * PEER_HEADS + h) + 1) for h in range(PEER_HEADS)]
            for rb in range(N_KEYS // pk):
                rs = slice(rb * pk, (rb + 1) * pk)
                w = None
                for h in range(PEER_HEADS):
                    sel = r1_rows[h] < m2b_ref[h, rs, cs]
                    contrib = jnp.where(sel, e2b_ref[h, rs, cs] * rho_rows[h], zero)
                    w = contrib if w is None else w + contrib
                hr = slice(al * N_KEYS + rb * pk, al * N_KEYS + (rb + 1) * pk)
                wh_ref[hr, cs] = w * _gelu_tanh(ht_ref[hr, cs].astype(BF16))
    acc_ref[...] += jnp.dot(vt_ref[...], wh_ref[...], preferred_element_type=F32)

    @pl.when(e == pl.num_programs(1) - 1)
    def _():
        o_ref[...] = x_ref[...] + acc_ref[...].T


def peer_dense(x, g, sel, u, v_t, layer):
    n, d = x.shape
    ne = u.shape[1]
    tm = _tile(n, (512, 256, 128))
    te = PEER_EXPERT_TILE
    blk = pl.BlockSpec((PEER_HEADS, N_KEYS, tm), lambda i, e: (0, 0, i))
    return pl.pallas_call(
        functools.partial(_peer_dense_kernel, te=te),
        grid=(n // tm, ne // te),
        in_specs=[pl.BlockSpec((tm, d), lambda i, e: (i, 0)), _full((1, d)), blk, blk, blk, blk,
                  pl.BlockSpec((None, te, d), lambda i, e: (layer, e, 0)),
                  pl.BlockSpec((None, d, te), lambda i, e: (layer, 0, e))],
        out_specs=pl.BlockSpec((tm, d), lambda i, e: (i, 0)),
        out_shape=jax.ShapeDtypeStruct((n, d), F32),
        scratch_shapes=[pltpu.VMEM((d, tm), BF16), pltpu.VMEM((d, tm), F32), pltpu.VMEM((te, tm), F32),
                        pltpu.VMEM((te, tm), BF16), pltpu.VMEM((PEER_HEADS, N_KEYS, tm), BF16),
                        pltpu.VMEM((PEER_HEADS, N_KEYS, tm), BF16),
                        pltpu.VMEM((2 * PEER_HEADS * (te // N_KEYS), tm), F32)],
        compiler_params=_params(("parallel", "arbitrary")),
    )(x, g.reshape(1, d), *sel, u, v_t)


def kernel(x_prompt, x_sample, cache_win_k, cache_win_v, cache_conv, state_shift, state_wkv, cache_mem_k, cache_mem_v,
           mem_prompt, g_mix, w_in, att_sinks, w_a_out, conv_dw, conv_db, conv_ln_g, conv_ln_b, w_b_out, b_b_out,
           rk_mu, rk_w0, rk_w2, rk_a0, rk_a2, rk_g2, rk_kk, rk_ka, rk_rk, rk_gn_g, rk_gn_b, w_c_out, w_o,
           g_ca, g_mem, w_cq, w_mk, w_mv, w_co, g_ffn, w_pq, peer_keys, peer_u, peer_v, g_final):
    b, t, d = x_prompt.shape
    db, dt, _ = x_sample.shape
    depth = w_in.shape[0]
    wb = cache_win_k.shape[2]
    mem_len = mem_prompt.shape[1]
    n_p, n_s = b * t, db * dt
    assert t % WINDOW == 0 and wb == WINDOW and dt <= SUBLANES

    xp = x_prompt.reshape(n_p, d)
    xs = x_sample.reshape(n_s, d)
    tabs_p = _rotary_tables(jnp.tile(jnp.arange(t, dtype=F32), b))
    tabs_s = _rotary_tables(jnp.tile(PAST_LEN + jnp.arange(dt, dtype=F32), db))
    tq = _tile(t)
    new_pad = SUBLANES - dt
    u_bf = peer_u.astype(BF16)
    vt_bf = jnp.swapaxes(peer_v.astype(BF16), 1, 2)
    mem = mem_prompt.reshape(b * mem_len, d)
    mem_k_all = cache_mem_k.reshape(depth * db, mem_len, d)
    mem_v_all = cache_mem_v.reshape(depth * db, mem_len, d)
    time_major = lambda a: a.reshape(db, dt, a.shape[-1]).transpose(1, 0, 2)
    seq_major = lambda a: a.transpose(1, 0, 2).reshape(n_s, a.shape[-1])
    k4 = lambda a, nb_, tt_: a.reshape(nb_, tt_, ATT_KV_HEADS, HEAD_DIM)

    outs = [[] for _ in range(12)]
    for l in range(depth):
        lp = {'rk_mu': rk_mu[l], 'rk_w0': rk_w0[l], 'rk_w2': rk_w2[l], 'rk_a0': rk_a0[l], 'rk_a2': rk_a2[l],
              'rk_g2': rk_g2[l], 'rk_kk': rk_kk[l], 'rk_ka': rk_ka[l], 'rk_rk': rk_rk[l].reshape(-1),
              'rk_gn_g': rk_gn_g[l], 'rk_gn_b': rk_gn_b[l], 'w_a_out': w_a_out[l], 'w_b_out': w_b_out[l],
              'b_b_out': b_b_out[l], 'w_c_out': w_c_out[l], 'w_o': w_o[l]}
        win = w_in[l].astype(BF16)
        c0, c1, c2 = A_COLS, A_COLS + B_COLS, A_COLS + B_COLS + C_COLS
        w_seg = [win[:, :c0], win[:, c0:c1], win[:, c1:c2], win[:, c2:]]
        pa_p, pb_p, pc_p, pg_p = (norm_matmul(xp, g_mix[l], w) for w in w_seg)
        pa_s, pb_s, pc_s, pg_s = (norm_matmul(xs, g_mix[l], w) for w in w_seg)

        q_p, k_p = rotary_qk(pa_p, tabs_p)
        q_s, k_s = rotary_qk(pa_s, tabs_s)
        oa_p = swa_prompt(q_p, k_p, pa_p, att_sinks[l], b, t)
        qs4 = q_s.reshape(db, dt, ATT_KV_HEADS, ATT_GROUP, HEAD_DIM).transpose(0, 2, 3, 1, 4).reshape(
            db, ATT_KV_HEADS, ATT_GROUP * dt, HEAD_DIM)
        k_new = k_s.reshape(db, dt, KV_W)
        v_new = pa_s[:, Q_W + KV_W:].reshape(db, dt, KV_W)
        padn = lambda a: jnp.pad(a, ((0, 0), (0, new_pad), (0, 0)))
        sink_rows = jnp.broadcast_to(
            jnp.repeat(att_sinks[l].reshape(ATT_KV_HEADS, ATT_GROUP), dt, axis=1)[:, :, None],
            (ATT_KV_HEADS, ATT_GROUP * dt, LANES))
        oa_s = swa_sample(qs4, cache_win_k, cache_win_v, l, padn(k_new), padn(v_new), sink_rows)
        oa_s = oa_s.reshape(db, ATT_KV_HEADS, ATT_GROUP, dt, HEAD_DIM).transpose(0, 3, 1, 2, 4).reshape(n_s, Q_W)

        yb_p, conv_st = conv_prompt(pb_p, conv_dw[l], conv_db[l], conv_ln_g[l], conv_ln_b[l], b, t)
        yb_s, conv_new = conv_sample(cache_conv[l].transpose(1, 0, 2), time_major(pb_s),
                                     conv_dw[l], conv_db[l], conv_ln_g[l], conv_ln_b[l])
        yb_s = seq_major(yb_s)

        *x5_p, v_p, g_p, bonus_p = rwkv_pre_prompt(pc_p, lp, b, t)
        pre_s = [seq_major(a) for a in rwkv_pre_sample(time_major(pc_s), state_shift[l], lp)]
        *x5_s, v_s, g_s, bonus_s = pre_s
        oc_p, wkv_p = rwkv_scan(x5_p, v_p, jnp.zeros((b, RWKV_HEADS, RWKV_HEAD, RWKV_HEAD), F32), b, t,
                                skewed=scan_skewed(b, t))
        oc_s, wkv_s = rwkv_scan(x5_s, v_s, state_wkv[l], db, dt)

        xp = merge_out(xp, oa_p, yb_p, oc_p, bonus_p, g_p, pg_p, lp)
        xs = merge_out(xs, oa_s, yb_s, oc_s, bonus_s, g_s, pg_s, lp)

        mk = norm_matmul(mem, g_mem[l], w_mk[l].astype(BF16))
        mv = norm_matmul(mem, g_mem[l], w_mv[l].astype(BF16))
        wcq, wco = w_cq[l].astype(BF16), w_co[l].astype(BF16)
        ca_p = cross_attn(norm_matmul(xp, g_ca[l], wcq).reshape(n_p // tq, tq, d), mk.reshape(b, mem_len, d),
                          mv.reshape(b, mem_len, d), t // tq)
        ca_s = cross_attn(norm_matmul(xs, g_ca[l], wcq).reshape(db, dt, d), mem_k_all, mem_v_all, 1, first_seq=l * db,
                          per_step=2 if db % 2 == 0 else 1)
        xp = matmul_residual(ca_p.reshape(n_p, d), wco, xp)
        xs = matmul_residual(ca_s.reshape(n_s, d), wco, xs)

        wpq = w_pq[l].astype(BF16)
        keys = peer_keys[l].reshape(2 * PEER_HEADS, N_KEYS, PEER_HALF).astype(BF16)
        xp = peer_dense(xp, g_ffn[l], peer_select(xp, g_ffn[l], wpq, keys), u_bf, vt_bf, l)
        xs = peer_dense(xs, g_ffn[l], peer_select(xs, g_ffn[l], wpq, keys), u_bf, vt_bf, l)

        outs[0].append(k4(k_p, b, t)[:, t - wb:])
        outs[1].append(k4(pa_p[:, Q_W + KV_W:], b, t)[:, t - wb:])
        outs[2].append(conv_st[:, CONV_HALO - (CONV_WIDTH - 1):])
        outs[3].append(pc_p.reshape(b, t, C_COLS)[:, -1])
        outs[4].append(wkv_p)
        outs[5].append(mk.reshape(b, mem_len, CA_HEADS, CA_HEAD_DIM))
        outs[6].append(mv.reshape(b, mem_len, CA_HEADS, CA_HEAD_DIM))
        outs[7].append(jnp.concatenate([cache_win_k[l][:, dt:], k4(k_new, db, dt)], axis=1))
        outs[8].append(jnp.concatenate([cache_win_v[l][:, dt:], k4(v_new, db, dt)], axis=1))
        outs[9].append(conv_new.transpose(1, 0, 2))
        outs[10].append(pc_s.reshape(db, dt, C_COLS)[:, -1])
        outs[11].append(wkv_s)

    yp = final_norm(xp, g_final)
    ys = final_norm(xs, g_final)
    return (yp.reshape(b, t, d), ys.reshape(db, dt, d)) + tuple(jnp.stack(o) for o in outs)
```

```python
import functools

import numpy as np
import jax
import jax.numpy as jnp
from jax import lax
from jax.experimental import pallas as pl
from jax.experimental.pallas import tpu as pltpu

F32 = jnp.float32
BF16 = jnp.bfloat16

D_MODEL = 1024
PAST_LEN = 8192
ATT_HEADS = 8
ATT_KV_HEADS = 2
ATT_GROUP = ATT_HEADS // ATT_KV_HEADS
HEAD_DIM = 64
ROT_DIM = HEAD_DIM // 4
ROPE_THETA = 500000.0
WINDOW = 128
CONV_CH = 512
CONV_WIDTH = 31
LN_EPS = 1e-5
RWKV_HEADS = 8
RWKV_HEAD = 64
RWKV_W = RWKV_HEADS * RWKV_HEAD
DECAY_LORA = 64
ICLR_LORA = 64
GATE_LORA = 128
GN_EPS = 64e-5
CA_HEADS = 4
CA_HEAD_DIM = D_MODEL // CA_HEADS
PEER_HEADS = 8
N_KEYS = 128
PEER_TOPK = 16
PEER_HALF = 128
RMS_EPS = 1e-6
NEG = -1e30

Q_W = ATT_HEADS * HEAD_DIM
KV_W = ATT_KV_HEADS * HEAD_DIM
A_COLS = Q_W + 2 * KV_W
B_COLS = 2 * CONV_CH
C_COLS = 3 * RWKV_W + DECAY_LORA + ICLR_LORA + GATE_LORA
G_COLS = 3 * D_MODEL

VMEM_LIMIT_V7X = 56 * 1024 * 1024
LANES = 128
SUBLANES = 8
SCAN_PAIRS = 16
SCAN_TSUB = 8
NT_DIMS = (((1,), (1,)), ((), ()))


def _tile(n, prefs=(512, 256, 128, 64, 32, 16, 8)):
    for t in prefs:
        if n % t == 0:
            return t
    raise ValueError(f"no tile for {n}")


def _params(sem):
    return pltpu.CompilerParams(dimension_semantics=sem, vmem_limit_bytes=VMEM_LIMIT_V7X)


def _sigmoid(x):
    return 1.0 / (1.0 + jnp.exp(-x))


def _full(shape):
    nd = len(shape)
    return pl.BlockSpec(shape, lambda *_: (0,) * nd)


def _norm_matmul_kernel(x_ref, g_ref, w_ref, o_ref):
    x = x_ref[...]
    h = x * lax.rsqrt(jnp.mean(x * x, axis=-1, keepdims=True) + RMS_EPS) * g_ref[...]
    o_ref[...] = jnp.dot(h.astype(BF16), w_ref[...], preferred_element_type=F32)


def norm_matmul(x, g, w):
    n, d = x.shape
    c = w.shape[1]
    tm = _tile(n)
    return pl.pallas_call(
        _norm_matmul_kernel,
        grid=(n // tm,),
        in_specs=[pl.BlockSpec((tm, d), lambda i: (i, 0)), _full((1, d)), _full((d, c))],
        out_specs=pl.BlockSpec((tm, c), lambda i: (i, 0)),
        out_shape=jax.ShapeDtypeStruct((n, c), F32),
        compiler_params=_params(("parallel",)),
    )(x, g.reshape(1, d), w)


def _matmul_res_kernel(a_ref, w_ref, x_ref, o_ref):
    o_ref[...] = x_ref[...] + jnp.dot(a_ref[...].astype(BF16), w_ref[...], preferred_element_type=F32)


def matmul_residual(a, w, x):
    n, k = a.shape
    c = w.shape[1]
    tm = _tile(n)
    return pl.pallas_call(
        _matmul_res_kernel,
        grid=(n // tm,),
        in_specs=[pl.BlockSpec((tm, k), lambda i: (i, 0)), _full((k, c)), pl.BlockSpec((tm, c), lambda i: (i, 0))],
        out_specs=pl.BlockSpec((tm, c), lambda i: (i, 0)),
        out_shape=jax.ShapeDtypeStruct((n, c), F32),
        compiler_params=_params(("parallel",)),
    )(a, w, x)


def _final_norm_kernel(x_ref, g_ref, o_ref):
    x = x_ref[...]
    o_ref[...] = x * lax.rsqrt(jnp.mean(x * x, axis=-1, keepdims=True) + RMS_EPS) * g_ref[...]


def final_norm(x, g):
    n, d = x.shape
    tm = _tile(n)
    return pl.pallas_call(
        _final_norm_kernel,
        grid=(n // tm,),
        in_specs=[pl.BlockSpec((tm, d), lambda i: (i, 0)), _full((1, d))],
        out_specs=pl.BlockSpec((tm, d), lambda i: (i, 0)),
        out_shape=jax.ShapeDtypeStruct((n, d), F32),
        compiler_params=_params(("parallel",)),
    )(x, g.reshape(1, d))


def _rotary_tables(pos):
    half = ROT_DIM // 2
    inv = ROPE_THETA ** (-2.0 * jnp.arange(half, dtype=F32) / ROT_DIM)
    ang = pos[:, None] * inv[None, :]
    cos, sin = jnp.cos(ang), jnp.sin(ang)
    n = pos.shape[0]
    one = jnp.ones((n, HEAD_DIM - ROT_DIM), F32)
    zero = jnp.zeros((n, HEAD_DIM - ROT_DIM), F32)
    zh = jnp.zeros((n, half), F32)
    c = jnp.concatenate([cos, cos, one], axis=1)
    s_up = jnp.concatenate([-sin, zh, zero], axis=1)
    s_dn = jnp.concatenate([zh, sin, zero], axis=1)
    return tuple(jnp.concatenate([t, t], axis=1) for t in (c, s_up, s_dn))


def _rotary_kernel(pa_ref, c_ref, su_ref, sd_ref, q_ref, k_ref):
    c, su, sd = c_ref[...], su_ref[...], sd_ref[...]
    half = ROT_DIM // 2
    for j in range((Q_W + KV_W) // LANES):
        x = pa_ref[:, j * LANES:(j + 1) * LANES]
        y = x * c + pltpu.roll(x, LANES - half, 1) * su + pltpu.roll(x, half, 1) * sd
        if j < Q_W // LANES:
            q_ref[:, j * LANES:(j + 1) * LANES] = y
        else:
            k_ref[...] = y


def rotary_qk(pa, tabs):
    n = pa.shape[0]
    tm = _tile(n)
    row = lambda w: pl.BlockSpec((tm, w), lambda i: (i, 0))
    return pl.pallas_call(
        _rotary_kernel,
        grid=(n // tm,),
        in_specs=[row(A_COLS), row(LANES), row(LANES), row(LANES)],
        out_specs=[row(Q_W), row(KV_W)],
        out_shape=[jax.ShapeDtypeStruct((n, Q_W), F32), jax.ShapeDtypeStruct((n, KV_W), F32)],
        compiler_params=_params(("parallel",)),
    )(pa, *tabs)


def _sink_softmax_pv(parts, sink):
    m = sink
    for s, _ in parts:
        m = jnp.maximum(m, jnp.max(s, axis=-1, keepdims=True))
    ps = [jnp.exp(s - m) for s, _ in parts]
    den = jnp.exp(sink - m)
    for p in ps:
        den = den + jnp.sum(p, axis=-1, keepdims=True)
    inv = 1.0 / den
    out = None
    for p, (_, v) in zip(ps, parts):
        o = jnp.dot((p * inv).astype(BF16), v, preferred_element_type=F32)
        out = o if out is None else out + o
    return out


def _swa_prompt_kernel(sink_ref, q_ref, kc_ref, kp_ref, vc_ref, vp_ref, o_ref):
    n = pl.program_id(1)
    qi = lax.broadcasted_iota(jnp.int32, (WINDOW, WINDOW), 0)
    kj = lax.broadcasted_iota(jnp.int32, (WINDOW, WINDOW), 1)
    mask_c = kj <= qi
    mask_p = kj > qi + jnp.where(n > 0, 0, WINDOW)
    scale = HEAD_DIM ** -0.5
    lane = lax.broadcasted_iota(jnp.int32, (WINDOW, KV_W), 1)
    low = lane < HEAD_DIM

    def halves(x, g):
        other = pltpu.roll(x, HEAD_DIM, 1)
        in_low, in_high = (x, other) if g == 0 else (other, x)
        return jnp.where(low, in_low, 0.0).astype(BF16), jnp.where(low, 0.0, in_high).astype(BF16)

    mask = jnp.concatenate([mask_p, mask_c], axis=1)
    for g in range(ATT_KV_HEADS):
        kk = [jnp.concatenate([p, c], axis=0) for p, c in zip(halves(kp_ref[...], g), halves(kc_ref[...], g))]
        vv = [jnp.concatenate([p, c], axis=0) for p, c in zip(halves(vp_ref[...], g), halves(vc_ref[...], g))]
        for pair in range(ATT_GROUP // 2):
            col = (g * (ATT_GROUP // 2) + pair) * KV_W
            q2 = q_ref[:, col:col + KV_W].astype(BF16)
            out = None
            for odd in range(2):
                s = lax.dot_general(q2, kk[odd], NT_DIMS, preferred_element_type=F32) * scale
                s = jnp.where(mask, s, NEG)
                o = _sink_softmax_pv([(s, vv[odd])], sink_ref[col // HEAD_DIM + odd])
                out = o if out is None else out + o
            o_ref[:, col:col + KV_W] = out


def swa_prompt(q, k, pa, sinks, b, t):
    nb = t // WINDOW
    cur = lambda w, c: pl.BlockSpec((WINDOW, w), lambda bi, ni: (bi * nb + ni, c))
    prev = lambda w, c: pl.BlockSpec((WINDOW, w), lambda bi, ni: (bi * nb + jnp.maximum(ni - 1, 0), c))
    vcol = (Q_W + KV_W) // KV_W
    return pl.pallas_call(
        _swa_prompt_kernel,
        grid=(b, nb),
        in_specs=[pl.BlockSpec(memory_space=pltpu.SMEM), cur(Q_W, 0), cur(KV_W, 0), prev(KV_W, 0),
                  cur(KV_W, vcol), prev(KV_W, vcol)],
        out_specs=cur(Q_W, 0),
        out_shape=jax.ShapeDtypeStruct((b * t, Q_W), F32),
        compiler_params=_params(("parallel", "parallel")),
    )(sinks, q, k, k, pa, pa)


def _swa_sample_kernel(sink_ref, q_ref, kb_ref, vb_ref, kn_ref, vn_ref, o_ref, *, bt, dt, wb):
    rows = ATT_GROUP * dt
    tb = lax.rem(lax.broadcasted_iota(jnp.int32, (rows, wb), 0), dt)
    jb = lax.broadcasted_iota(jnp.int32, (rows, wb), 1)
    mask_b = jb > tb + (wb - WINDOW)
    npad = kn_ref.shape[1]
    tn = lax.rem(lax.broadcasted_iota(jnp.int32, (rows, npad), 0), dt)
    jn = lax.broadcasted_iota(jnp.int32, (rows, npad), 1)
    mask_n = jn <= tn
    scale = HEAD_DIM ** -0.5
    for b in range(bt):
        for g in range(ATT_KV_HEADS):
            gs = slice(g * HEAD_DIM, (g + 1) * HEAD_DIM)
            q = q_ref[b, g].astype(BF16)
            sb = lax.dot_general(q, kb_ref[b, :, g, :].astype(BF16), NT_DIMS, preferred_element_type=F32) * scale
            sn = lax.dot_general(q, kn_ref[b, :, gs].astype(BF16), NT_DIMS, preferred_element_type=F32) * scale
            sb = jnp.where(mask_b, sb, NEG)
            sn = jnp.where(mask_n, sn, NEG)
            o_ref[b, g] = _sink_softmax_pv(
                [(sb, vb_ref[b, :, g, :].astype(BF16)), (sn, vn_ref[b, :, gs].astype(BF16))], sink_ref[g][:, :1])


def swa_sample(qs, cache_k, cache_v, layer, knew, vnew, sink_rows):
    db, _, rows, _ = qs.shape
    dt = rows // ATT_GROUP
    wb = cache_k.shape[2]
    bt = _tile(db, (8, 4, 2, 1))
    blk = lambda a: pl.BlockSpec((bt,) + a.shape[1:], lambda i: (i,) + (0,) * (a.ndim - 1))
    cache = pl.BlockSpec((None, bt) + cache_k.shape[2:], lambda i: (layer, i, 0, 0, 0))
    return pl.pallas_call(
        functools.partial(_swa_sample_kernel, bt=bt, dt=dt, wb=wb),
        grid=(db // bt,),
        in_specs=[_full(sink_rows.shape), blk(qs), cache, cache, blk(knew), blk(vnew)],
        out_specs=blk(qs),
        out_shape=jax.ShapeDtypeStruct(qs.shape, F32),
        compiler_params=_params(("parallel",)),
    )(sink_rows, qs, cache_k, cache_v, knew, vnew)


CONV_HALO = 32


def _ln_swish(y, lg, lb):
    mu = jnp.mean(y, axis=-1, keepdims=True)
    var = jnp.mean(jnp.square(y - mu), axis=-1, keepdims=True)
    yn = (y - mu) * lax.rsqrt(var + LN_EPS) * lg + lb
    return yn * _sigmoid(yn)


def _conv_prompt_kernel(pb_ref, dw_ref, db_ref, lg_ref, lb_ref, y_ref, st_ref, ext_ref, sh_ref, *, tt):
    i = pl.program_id(1)

    @pl.when(i == 0)
    def _():
        ext_ref[0:CONV_HALO, :] = jnp.zeros((CONV_HALO, CONV_CH), F32)

    @pl.when(i > 0)
    def _():
        ext_ref[0:CONV_HALO, :] = ext_ref[tt:tt + CONV_HALO, :]

    pb = pb_ref[...]
    ext_ref[CONV_HALO:CONV_HALO + tt, :] = pb[:, :CONV_CH] * _sigmoid(pb[:, CONV_CH:])
    acc = jnp.zeros((tt, CONV_CH), F32) + db_ref[...]
    first = CONV_HALO - (CONV_WIDTH - 1)
    span = tt + CONV_HALO - SUBLANES
    for r in range(SUBLANES):
        taps = [j for j in range(CONV_WIDTH) if (first + j) % SUBLANES == r]
        if r:
            sh_ref[0:span, :] = ext_ref[pl.ds(r, span), :]
        src = sh_ref if r else ext_ref
        for j in taps:
            q = first + j - r
            acc = acc + src[q:q + tt, :] * dw_ref[j:j + 1, :]
    y_ref[...] = _ln_swish(acc, lg_ref[...], lb_ref[...])

    @pl.when(i == pl.num_programs(1) - 1)
    def _():
        st_ref[0] = ext_ref[tt:tt + CONV_HALO, :]


def conv_prompt(pb, dw, db, lg, lb, b, t):
    tt = _tile(t)
    nt = t // tt
    vec = lambda a: a.reshape(1, CONV_CH)
    return pl.pallas_call(
        functools.partial(_conv_prompt_kernel, tt=tt),
        grid=(b, nt),
        in_specs=[pl.BlockSpec((tt, B_COLS), lambda bi, i: (bi * nt + i, 0)), _full((CONV_WIDTH, CONV_CH)),
                  _full((1, CONV_CH)), _full((1, CONV_CH)), _full((1, CONV_CH))],
        out_specs=[pl.BlockSpec((tt, CONV_CH), lambda bi, i: (bi * nt + i, 0)),
                   pl.BlockSpec((1, CONV_HALO, CONV_CH), lambda bi, i: (bi, 0, 0))],
        out_shape=[jax.ShapeDtypeStruct((b * t, CONV_CH), F32), jax.ShapeDtypeStruct((b, CONV_HALO, CONV_CH), F32)],
        scratch_shapes=[pltpu.VMEM((tt + CONV_HALO, CONV_CH), F32), pltpu.VMEM((tt + CONV_HALO, CONV_CH), F32)],
        compiler_params=_params(("arbitrary", "arbitrary")),
    )(pb, dw, vec(db), vec(lg), vec(lb))


def _conv_sample_kernel(c_ref, pb_ref, dw_ref, db_ref, lg_ref, lb_ref, y_ref, nc_ref, *, dt):
    nprev = CONV_WIDTH - 1
    us = []
    for t in range(dt):
        pb = pb_ref[t]
        us.append(pb[:, :CONV_CH] * _sigmoid(pb[:, CONV_CH:]))

    def ext(j):
        return c_ref[j] if j < nprev else us[j - nprev]

    for t in range(dt):
        acc = db_ref[...] + ext(t) * dw_ref[0:1, :]
        for j in range(1, CONV_WIDTH):
            acc = acc + ext(t + j) * dw_ref[j:j + 1, :]
        y_ref[t] = _ln_swish(acc, lg_ref[...], lb_ref[...])
    for j in range(nprev):
        nc_ref[j] = ext(j + dt)


def conv_sample(cache_t, pb_t, dw, db, lg, lb):
    nprev, dbt, _ = cache_t.shape
    dt = pb_t.shape[0]
    bt = _tile(dbt, (32, 16, 8))
    vec = lambda a: a.reshape(1, CONV_CH)
    blk = lambda lead, w: pl.BlockSpec((lead, bt, w), lambda i: (0, i, 0))
    return pl.pallas_call(
        functools.partial(_conv_sample_kernel, dt=dt),
        grid=(dbt // bt,),
        in_specs=[blk(nprev, CONV_CH), blk(dt, B_COLS), _full((CONV_WIDTH, CONV_CH)), _full((1, CONV_CH)),
                  _full((1, CONV_CH)), _full((1, CONV_CH))],
        out_specs=[blk(dt, CONV_CH), blk(nprev, CONV_CH)],
        out_shape=[jax.ShapeDtypeStruct((dt, dbt, CONV_CH), F32), jax.ShapeDtypeStruct((nprev, dbt, CONV_CH), F32)],
        compiler_params=_params(("parallel",)),
    )(cache_t, pb_t, dw, vec(db), vec(lg), vec(lb))


def _head_sum_matrix():
    h = np.arange(RWKV_W) // RWKV_HEAD
    return jnp.asarray((h[:, None] == h[None, :]).astype(np.float32), dtype=BF16)


def _head_sum(x, hs_ref):
    hi = x.astype(BF16)
    lo = (x - hi.astype(F32)).astype(BF16)
    hs = hs_ref[...]
    return jnp.dot(hi, hs, preferred_element_type=F32) + jnp.dot(lo, hs, preferred_element_type=F32)


def _rwkv_pre_math(pc, prev, c):
    mu_ref, w0_ref, w2_ref, a0_ref, a2_ref, g2_ref, kkp_ref, ka_ref, rk_ref, hs_ref = c
    xs = pc + (prev - pc) * mu_ref[...]
    o1, o2, o3 = RWKV_W, 2 * RWKV_W, 3 * RWKV_W
    o4 = o3 + DECAY_LORA
    o5 = o4 + ICLR_LORA
    r, k, v = xs[:, :o1], xs[:, o1:o2], xs[:, o2:o3]
    wd, ad, gd = xs[:, o3:o4], xs[:, o4:o5], xs[:, o5:]
    y = -(w0_ref[...] + jnp.dot(jnp.tanh(wd).astype(BF16), w2_ref[...], preferred_element_type=F32))
    softplus = jnp.maximum(y, 0.0) + jnp.log(1.0 + jnp.exp(-jnp.abs(y)))
    decay = jnp.exp(-jnp.exp(-softplus - 0.5))
    a = _sigmoid(a0_ref[...] + jnp.dot(ad.astype(BF16), a2_ref[...], preferred_element_type=F32))
    g = jnp.dot(_sigmoid(gd).astype(BF16), g2_ref[...], preferred_element_type=F32)
    kk = k * kkp_ref[...]
    norm = jnp.sqrt(_head_sum(kk * kk, hs_ref))
    kk = kk / jnp.maximum(norm, 1e-12)
    k_mod = k * (1.0 + (a - 1.0) * ka_ref[...])
    bonus = _head_sum(r * k_mod * rk_ref[...], hs_ref) * v
    return r, decay, k_mod, -kk, kk * a, v, g, bonus


RWKV_PRE_OUTS = 8
SCAN_KEY_OPERANDS = 5
SCAN_CHUNK = 256
SCAN_GROUP_UNROLL = 1
SHIFT_HALO = SUBLANES


def _rwkv_pre_prompt_kernel(pc_ref, *refs, tt, skew):
    consts, outs, ext_ref = refs[:10], refs[10:10 + RWKV_PRE_OUTS], refs[-1]
    i = pl.program_id(1)

    @pl.when(i == 0)
    def _():
        ext_ref[0:SHIFT_HALO, :] = jnp.zeros((SHIFT_HALO, C_COLS), F32)

    @pl.when(i > 0)
    def _():
        ext_ref[0:SHIFT_HALO, :] = ext_ref[tt:tt + SHIFT_HALO, :]

    pc = pc_ref[...]
    ext_ref[SHIFT_HALO:SHIFT_HALO + tt, :] = pc
    res = _rwkv_pre_math(pc, ext_ref[pl.ds(SHIFT_HALO - 1, tt), :], consts)
    for j, (o_ref, val) in enumerate(zip(outs, res)):
        if j < SCAN_KEY_OPERANDS and skew:
            vt = val.T
            for h in range(RWKV_HEADS):
                g = pl.program_id(0) * RWKV_HEADS + h
                rows = slice(h * RWKV_HEAD, (h + 1) * RWKV_HEAD)
                o_ref[rows, :] = pltpu.roll(vt[rows, :], g * SCAN_TSUB, 1)
        else:
            o_ref[...] = val


def _rwkv_pre_sample_kernel(pc_ref, shift_ref, *refs, dt):
    consts, outs = refs[:10], refs[10:10 + RWKV_PRE_OUTS]
    for t in range(dt):
        res = _rwkv_pre_math(pc_ref[t], shift_ref[...] if t == 0 else pc_ref[t - 1], consts)
        for o_ref, val in zip(outs, res):
            o_ref[t] = val


def _rwkv_pre_consts(lp):
    vec = lambda a: a.reshape(1, -1)
    return [vec(lp['rk_mu']), vec(lp['rk_w0']), lp['rk_w2'].astype(BF16), vec(lp['rk_a0']), lp['rk_a2'].astype(BF16),
            lp['rk_g2'].astype(BF16), vec(lp['rk_kk']), vec(lp['rk_ka']), vec(lp['rk_rk']), _head_sum_matrix()]


def scan_skewed(b, t):
    return b * RWKV_HEADS == SCAN_PAIRS and t % SCAN_CHUNK == 0


def rwkv_pre_prompt(pc, lp, b, t):
    skew = scan_skewed(b, t)
    tt = SCAN_CHUNK if skew else _tile(t, (256, 128))
    nt = t // tt
    consts = _rwkv_pre_consts(lp)
    row = lambda w: pl.BlockSpec((tt, w), lambda bi, i: (bi * nt + i, 0))
    nkey = SCAN_KEY_OPERANDS if skew else 0
    col = pl.BlockSpec((RWKV_W, tt), lambda bi, i: (bi, i))
    return pl.pallas_call(
        functools.partial(_rwkv_pre_prompt_kernel, tt=tt, skew=skew),
        grid=(b, nt),
        in_specs=[row(C_COLS)] + [_full(c.shape) for c in consts],
        out_specs=[col] * nkey + [row(RWKV_W)] * (RWKV_PRE_OUTS - nkey),
        out_shape=[jax.ShapeDtypeStruct((b * RWKV_W, t), F32)] * nkey
        + [jax.ShapeDtypeStruct((b * t, RWKV_W), F32)] * (RWKV_PRE_OUTS - nkey),
        scratch_shapes=[pltpu.VMEM((tt + SHIFT_HALO, C_COLS), F32)],
        compiler_params=_params(("arbitrary", "arbitrary")),
    )(pc, *consts)


def rwkv_pre_sample(pc_t, shift, lp):
    dt, dbt, _ = pc_t.shape
    bt = _tile(dbt, (64, 32, 16, 8))
    consts = _rwkv_pre_consts(lp)
    blk = lambda w: pl.BlockSpec((dt, bt, w), lambda i: (0, i, 0))
    return pl.pallas_call(
        functools.partial(_rwkv_pre_sample_kernel, dt=dt),
        grid=(dbt // bt,),
        in_specs=[blk(C_COLS), pl.BlockSpec((bt, C_COLS), lambda i: (i, 0))] + [_full(c.shape) for c in consts],
        out_specs=[blk(RWKV_W)] * RWKV_PRE_OUTS,
        out_shape=[jax.ShapeDtypeStruct((dt, dbt, RWKV_W), F32)] * RWKV_PRE_OUTS,
        compiler_params=_params(("parallel",)),
    )(pc_t, shift, *consts)


def _scan_select_matrices(pair_major):
    e = np.zeros((SCAN_TSUB, LANES, LANES), np.float32)
    vl = LANES // SCAN_PAIRS
    for tl in range(SCAN_TSUB):
        for g in range(SCAN_PAIRS):
            src = g * SCAN_TSUB + tl if pair_major else tl * SCAN_PAIRS + g
            e[tl, src, g * vl:(g + 1) * vl] = 1.0
    return jnp.asarray(np.concatenate([e, e, e], axis=1), dtype=BF16)


def _sublane_allsum(x):
    x = x + pltpu.roll(x, 4, 0)
    x = x + pltpu.roll(x, 2, 0)
    return x + pltpu.roll(x, 1, 0)


def _scan_kernel(xr_ref, xw_ref, xk_ref, xa_ref, xb_ref, v_ref, s0_ref, e_ref, o_ref, sf_ref, z_ref, tile_ref, xs_ref,
                 *, groups, steps, skewed):
    x_refs = (xr_ref, xw_ref, xk_ref, xa_ref, xb_ref)
    c = pl.program_id(1)
    nvh = z_ref.shape[0]
    nkb = RWKV_HEAD // SUBLANES
    sub = lax.broadcasted_iota(jnp.int32, (SUBLANES, LANES), 0)
    R_ROW, W_ROW, K_ROW, A_ROW, B_ROW = (j * RWKV_HEAD for j in range(5))

    @pl.when(c == 0)
    def _():
        z_ref[...] = s0_ref[0]

    lane_group = lax.broadcasted_iota(jnp.int32, (SUBLANES, LANES), 1) // SCAN_TSUB

    def gather_group(x_ref, th, kb):
        acc = jnp.zeros((SUBLANES, LANES), F32)
        for g in range(SCAN_PAIRS):
            pos = th + g
            tile = pl.multiple_of(((pos // SCAN_PAIRS) % 2) * LANES, LANES)
            rows = slice(g * RWKV_HEAD + kb * SUBLANES, g * RWKV_HEAD + (kb + 1) * SUBLANES)
            acc = jnp.where(lane_group == pos % SCAN_PAIRS, x_ref[rows, pl.ds(tile, LANES)], acc)
        return pltpu.roll(acc, ((SCAN_PAIRS - th % SCAN_PAIRS) % SCAN_PAIRS) * SCAN_TSUB, 1)

    def split_group(th):
        th = jnp.asarray(th, jnp.int32)
        for j, x_ref in enumerate(x_refs):
            rows = slice(j * RWKV_HEAD, (j + 1) * RWKV_HEAD)
            if skewed:
                xg = jnp.concatenate([gather_group(x_ref, th, kb) for kb in range(nkb)], axis=0)
            else:
                xg = x_ref[0, th]
            hi = xg.astype(BF16)
            r1 = xg - hi.astype(F32)
            mid = r1.astype(BF16)
            lo = (r1 - mid.astype(F32)).astype(BF16)
            xs_ref[rows, 0:LANES] = hi
            xs_ref[rows, LANES:2 * LANES] = mid
            xs_ref[rows, 2 * LANES:3 * LANES] = lo

    def spread(tl, buf):
        tile_ref[buf] = jnp.dot(xs_ref[...], e_ref[tl], preferred_element_type=F32)

    split_group(0)
    spread(0, 0)

    def group_steps(th):
        for tl in range(steps):
            cur = tl % 2
            t = th * steps + tl
            vrow = v_ref[0, t]
            blk = lambda row0, kb: tile_ref[cur, row0 + kb * SUBLANES:row0 + (kb + 1) * SUBLANES, :]
            sa = []
            for vh in range(nvh):
                acc = None
                for kb in range(nkb):
                    p = z_ref[vh, kb * SUBLANES:(kb + 1) * SUBLANES, :] * blk(A_ROW, kb)
                    acc = p if kb == 0 else acc + p
                sa.append(_sublane_allsum(acc))
            if tl + 1 < steps:
                spread(tl + 1, 1 - cur)
            else:
                split_group(jnp.minimum(th + 1, groups - 1))
                spread(0, 1 - cur)
            orow = jnp.zeros((SUBLANES, LANES), F32)
            for vh in range(nvh):
                vb = jnp.broadcast_to(vrow[vh:vh + 1], (SUBLANES, LANES))
                acc = None
                for kb in range(nkb):
                    ks = slice(kb * SUBLANES, (kb + 1) * SUBLANES)
                    zn = z_ref[vh, ks, :] * blk(W_ROW, kb) + blk(B_ROW, kb) * sa[vh] + blk(K_ROW, kb) * vb
                    z_ref[vh, ks, :] = zn
                    p = zn * blk(R_ROW, kb)
                    acc = p if kb == 0 else acc + p
                orow = jnp.where(sub == vh, _sublane_allsum(acc), orow)
            o_ref[0, t] = orow

    unroll = SCAN_GROUP_UNROLL if groups % SCAN_GROUP_UNROLL == 0 else 1

    def body(i, carry):
        for u in range(unroll):
            group_steps(i * unroll + u)
        return carry

    lax.fori_loop(0, groups // unroll, body, 0)

    @pl.when(c == pl.num_programs(1) - 1)
    def _():
        sf_ref[0] = z_ref[...]


def rwkv_scan(x5, v, s0, nseq, t, skewed=False):
    spb = SCAN_PAIRS // RWKV_HEADS
    assert nseq % spb == 0
    nblk = nseq // spb
    vl = LANES // SCAN_PAIRS
    vh = RWKV_HEAD // vl
    steps = min(SCAN_TSUB, t)
    assert t % steps == 0 and steps % 2 == 0
    t8 = t // steps
    def key_layout(a):
        a = a.reshape(nblk, spb, t8, steps, RWKV_HEADS, RWKV_HEAD)
        if steps < SCAN_TSUB:
            a = jnp.pad(a, ((0, 0),) * 3 + ((0, SCAN_TSUB - steps),) + ((0, 0),) * 2)
        return a.transpose(0, 2, 5, 3, 1, 4).reshape(nblk, t8, RWKV_HEAD, LANES)

    vk = v.reshape(nblk, spb, t, RWKV_HEADS, vh, vl).transpose(0, 2, 4, 1, 3, 5).reshape(nblk, t, vh, LANES)
    sk = s0.reshape(nblk, spb, RWKV_HEADS, vh, vl, RWKV_HEAD).transpose(0, 3, 5, 1, 2, 4).reshape(
        nblk, vh, RWKV_HEAD, LANES)
    if skewed:
        assert steps == SCAN_TSUB and t % SCAN_CHUNK == 0
        xk = x5
        groups = SCAN_CHUNK // steps
        key_spec = pl.BlockSpec((SCAN_PAIRS * RWKV_HEAD, SCAN_CHUNK), lambda i, c: (i, c))
    else:
        xk = [key_layout(a) for a in x5]
        groups = _tile(t8, (32, 16, 8, 4, 2, 1))
        key_spec = pl.BlockSpec((1, groups, RWKV_HEAD, LANES), lambda i, c: (i, c, 0, 0))
    nchunk = t8 // groups
    ch = groups * steps
    o, sf = pl.pallas_call(
        functools.partial(_scan_kernel, groups=groups, steps=steps, skewed=skewed),
        grid=(nblk, nchunk),
        in_specs=[key_spec] * 5 + [
                  pl.BlockSpec((1, ch, vh, LANES), lambda i, c: (i, c, 0, 0)),
                  pl.BlockSpec((1, vh, RWKV_HEAD, LANES), lambda i, c: (i, 0, 0, 0)),
                  _full((SCAN_TSUB, 3 * LANES, LANES))],
        out_specs=[pl.BlockSpec((1, ch, vh, LANES), lambda i, c: (i, c, 0, 0)),
                   pl.BlockSpec((1, vh, RWKV_HEAD, LANES), lambda i, c: (i, 0, 0, 0))],
        out_shape=[jax.ShapeDtypeStruct((nblk, t, vh, LANES), F32),
                   jax.ShapeDtypeStruct((nblk, vh, RWKV_HEAD, LANES), F32)],
        scratch_shapes=[pltpu.VMEM((vh, RWKV_HEAD, LANES), F32), pltpu.VMEM((2, 5 * RWKV_HEAD, LANES), F32),
                        pltpu.VMEM((5 * RWKV_HEAD, 3 * LANES), BF16)],
        compiler_params=_params(("arbitrary", "arbitrary")),
    )(*xk, vk, sk, _scan_select_matrices(pair_major=skewed))
    o = o.reshape(nblk, t, vh, spb, RWKV_HEADS, vl).transpose(0, 3, 1, 4, 2, 5).reshape(nseq * t, RWKV_W)
    sf = sf.reshape(nblk, vh, RWKV_HEAD, spb, RWKV_HEADS, vl).transpose(0, 3, 4, 1, 5, 2).reshape(
        nseq, RWKV_HEADS, RWKV_HEAD, RWKV_HEAD)
    return o, sf


def _merge_kernel(x_ref, oa_ref, yb_ref, oc_ref, bonus_ref, g_ref, pg_ref, wa_ref, wb_ref, bb_ref, wc_ref, wo_ref,
                  gng_ref, gnb_ref, hs_ref, o_ref):
    oc = oc_ref[...]
    mu = _head_sum(oc, hs_ref) * (1.0 / RWKV_HEAD)
    dev = oc - mu
    var = _head_sum(dev * dev, hs_ref) * (1.0 / RWKV_HEAD)
    on = dev * lax.rsqrt(var + GN_EPS) * gng_ref[...] + gnb_ref[...]
    yc = (on + bonus_ref[...]) * g_ref[...]
    d = D_MODEL
    dot = lambda a, w: jnp.dot(a.astype(BF16), w[...], preferred_element_type=F32)
    merged = (_sigmoid(pg_ref[:, 0:d]) * dot(oa_ref[...], wa_ref)
              + _sigmoid(pg_ref[:, d:2 * d]) * (dot(yb_ref[...], wb_ref) + bb_ref[...])
              + _sigmoid(pg_ref[:, 2 * d:3 * d]) * dot(yc, wc_ref))
    o_ref[...] = x_ref[...] + dot(merged, wo_ref)


def merge_out(x, oa, yb, oc, bonus, g, pg, lp):
    n = x.shape[0]
    tm = _tile(n)
    row = lambda w: pl.BlockSpec((tm, w), lambda i: (i, 0))
    vec = lambda a: a.reshape(1, -1)
    consts = [lp['w_a_out'].astype(BF16), lp['w_b_out'].astype(BF16), vec(lp['b_b_out']), lp['w_c_out'].astype(BF16),
              lp['w_o'].astype(BF16), vec(lp['rk_gn_g']), vec(lp['rk_gn_b']), _head_sum_matrix()]
    return pl.pallas_call(
        _merge_kernel,
        grid=(n // tm,),
        in_specs=[row(D_MODEL), row(Q_W), row(CONV_CH), row(RWKV_W), row(RWKV_W), row(RWKV_W), row(G_COLS)]
        + [_full(c.shape) for c in consts],
        out_specs=row(D_MODEL),
        out_shape=jax.ShapeDtypeStruct((n, D_MODEL), F32),
        compiler_params=_params(("parallel",)),
    )(x, oa, yb, oc, bonus, g, pg, *consts)


def _cross_attn_kernel(q_ref, k_ref, v_ref, o_ref):
    scale = CA_HEAD_DIM ** -0.5
    for b in range(q_ref.shape[0]):
        outs = []
        for h in range(CA_HEADS):
            hs = slice(h * CA_HEAD_DIM, (h + 1) * CA_HEAD_DIM)
            q = q_ref[b, :, hs].astype(BF16)
            s = lax.dot_general(q, k_ref[b, :, hs].astype(BF16), NT_DIMS, preferred_element_type=F32) * scale
            m = jnp.max(s, axis=-1, keepdims=True)
            p = jnp.exp(s - m)
            p = p / jnp.sum(p, axis=-1, keepdims=True)
            outs.append(jnp.dot(p.astype(BF16), v_ref[b, :, hs].astype(BF16), preferred_element_type=F32))
        o_ref[b] = jnp.concatenate(outs, axis=1)


def cross_attn(q, mk, mv, tiles_per_seq, first_seq=0, per_step=1):
    nt, tq, d = q.shape
    m = mk.shape[1]
    assert per_step == 1 or (tiles_per_seq == 1 and nt % per_step == 0 and first_seq % per_step == 0)
    nt //= per_step
    kv = pl.BlockSpec((per_step, m, d), lambda i: (first_seq // per_step + i // tiles_per_seq, 0, 0))
    qs = pl.BlockSpec((per_step, tq, d), lambda i: (i, 0, 0))
    return pl.pallas_call(
        _cross_attn_kernel,
        grid=(nt,),
        in_specs=[qs, kv, kv],
        out_specs=qs,
        out_shape=jax.ShapeDtypeStruct(q.shape, F32),
        compiler_params=_params(("parallel",)),
    )(q, mk, mv)


def _staircase():
    return [(i, PEER_TOPK // (i + 1)) for i in range(PEER_TOPK)]


PEER_CAND = sum(nj for _, nj in _staircase())
PEER_CAND_PAD = -(-PEER_CAND // SUBLANES) * SUBLANES


def _extract_topk(cur, out_ref, base, with_rank=False):
    rank = jnp.full(cur.shape, float(PEER_TOPK), F32) if with_rank else None
    for k in range(PEER_TOPK):
        m = jnp.max(cur, axis=0, keepdims=True)
        out_ref[base + k:base + k + 1, :] = m
        hit = cur == m
        if with_rank:
            rank = jnp.where(hit, float(k), rank)
        if k + 1 < PEER_TOPK:
            cur = jnp.where(hit, -jnp.inf, cur)
    return rank


def _peer_select_kernel(x_ref, g_ref, wq_ref, keys_ref, r1_ref, rho_ref, m2_ref, e2_ref, qp_ref, s_ref, tv_ref,
                        cand_ref, best_ref):
    nsub = 2 * PEER_HEADS
    x = x_ref[...]
    xn = x * lax.rsqrt(jnp.mean(x * x, axis=-1, keepdims=True) + RMS_EPS) * g_ref[...]
    qp_ref[...] = jnp.dot(xn.astype(BF16), wq_ref[...], preferred_element_type=F32)
    for hc in range(nsub):
        q = qp_ref[:, hc * PEER_HALF:(hc + 1) * PEER_HALF].astype(BF16)
        st = lax.dot_general(keys_ref[hc], q, NT_DIMS, preferred_element_type=F32)
        s_ref[hc] = st
        rank = _extract_topk(st, tv_ref, hc * PEER_TOPK, with_rank=hc % 2 == 0)
        if hc % 2 == 0:
            r1_ref[hc // 2] = rank
    tm = qp_ref.shape[0]
    for h in range(PEER_HEADS):
        b1 = 2 * h * PEER_TOPK
        b2 = b1 + PEER_TOPK
        cand_ref[PEER_CAND_PAD - SUBLANES:PEER_CAND_PAD, :] = jnp.full((SUBLANES, tm), -jnp.inf, F32)
        off = 0
        for i, nj in _staircase():
            cand_ref[off:off + nj, :] = tv_ref[b1 + i:b1 + i + 1, :] + tv_ref[b2:b2 + nj, :]
            off += nj
        _extract_topk(cand_ref[...], best_ref, 0)
        best = best_ref[...]
        z = jnp.sum(jnp.exp(best - best[0:1, :]), axis=0, keepdims=True)
        thr = best[PEER_TOPK - 1:PEER_TOPK, :]
        s2 = s_ref[2 * h + 1]
        m2 = jnp.zeros(s2.shape, F32)
        for i in range(PEER_TOPK):
            m2 = jnp.where((tv_ref[b1 + i:b1 + i + 1, :] + s2) >= thr, float(i + 1), m2)
        m2_ref[h] = m2
        e2_ref[h] = jnp.exp(s2 - tv_ref[b2:b2 + 1, :])
        rho_ref[h] = jnp.exp(s_ref[2 * h] - tv_ref[b1:b1 + 1, :]) * (1.0 / z)


def peer_select(x, g, wq, keys):
    n, d = x.shape
    tm = _tile(n, (256, 128))
    nsub = 2 * PEER_HEADS
    blk = pl.BlockSpec((PEER_HEADS, N_KEYS, tm), lambda i: (0, 0, i))
    shp = lambda dt: jax.ShapeDtypeStruct((PEER_HEADS, N_KEYS, n), dt)
    return pl.pallas_call(
        _peer_select_kernel,
        grid=(n // tm,),
        in_specs=[pl.BlockSpec((tm, d), lambda i: (i, 0)), _full((1, d)), _full(wq.shape), _full(keys.shape)],
        out_specs=[blk, blk, blk, blk],
        out_shape=[shp(F32)] * 4,
        scratch_shapes=[pltpu.VMEM((tm, nsub * PEER_HALF), F32), pltpu.VMEM((nsub, N_KEYS, tm), F32),
                        pltpu.VMEM((nsub * PEER_TOPK, tm), F32), pltpu.VMEM((PEER_CAND_PAD, tm), F32),
                        pltpu.VMEM((PEER_TOPK, tm), F32)],
        compiler_params=_params(("parallel",)),
    )(x, g.reshape(1, d), wq, keys)


PEER_PACK_ROWS = 16
PEER_EXPERT_TILE = 8 * N_KEYS


def _gelu_tanh(x):
    return 0.5 * x * (1.0 + jnp.tanh(0.7978845608028654 * (x + 0.044715 * (x * x * x))))


def _peer_dense_kernel(x_ref, g_ref, r1_ref, rho_ref, m2_ref, e2_ref, u_ref, vt_ref, o_ref, xnt_ref, acc_ref, ht_ref,
                       wh_ref, m2b_ref, e2b_ref, rows_ref, *, te):
    e = pl.program_id(1)
    tm = x_ref.shape[0]
    pk = PEER_PACK_ROWS

    @pl.when(e == 0)
    def _():
        x = x_ref[...]
        xn = x * lax.rsqrt(jnp.mean(x * x, axis=-1, keepdims=True) + RMS_EPS) * g_ref[...]
        xnt_ref[...] = xn.T.astype(BF16)
        acc_ref[...] = jnp.zeros_like(acc_ref)
        m2b_ref[...] = m2_ref[...].astype(BF16)
        e2b_ref[...] = e2_ref[...].astype(BF16)

    n_a = te // N_KEYS
    for al in range(n_a):
        a = e * n_a + al
        for h in range(PEER_HEADS):
            row = 2 * (al * PEER_HEADS + h)
            rows_ref[row:row + 1, :] = r1_ref[h, pl.ds(a, 1), :]
            rows_ref[row + 1:row + 2, :] = rho_ref[h, pl.ds(a, 1), :]

    ht_ref[...] = jnp.dot(u_ref[...], xnt_ref[...], preferred_element_type=F32)
    zero = jnp.zeros((), BF16)
    for al in range(n_a):
        for cl in range(tm // LANES):
            cs = slice(cl * LANES, (cl + 1) * LANES)
            packed = lambda row: jnp.broadcast_to(rows_ref[row:row + 1, cs], (pk, LANES)).astype(BF16)
            r1_rows = [packed(2 * (al * PEER_HEADS + h)) for h in range(PEER_HEADS)]
            rho_rows = [packed(2 * (al * PEER_HEADS + h) + 1) for h in range(PEER_HEADS)]
            for rb in range(N_KEYS // pk):
                rs = slice(rb * pk, (rb + 1) * pk)
                w = None
                for h in range(PEER_HEADS):
                    sel = r1_rows[h] < m2b_ref[h, rs, cs]
                    contrib = jnp.where(sel, e2b_ref[h, rs, cs] * rho_rows[h], zero)
                    w = contrib if w is None else w + contrib
                hr = slice(al * N_KEYS + rb * pk, al * N_KEYS + (rb + 1) * pk)
                wh_ref[hr, cs] = w * _gelu_tanh(ht_ref[hr, cs].astype(BF16))
    acc_ref[...] += jnp.dot(vt_ref[...], wh_ref[...], preferred_element_type=F32)

    @pl.when(e == pl.num_programs(1) - 1)
    def _():
        o_ref[...] = x_ref[...] + acc_ref[...].T


def peer_dense(x, g, sel, u, v_t, layer):
    n, d = x.shape
    ne = u.shape[1]
    tm = _tile(n, (512, 256, 128))
    te = PEER_EXPERT_TILE
    blk = pl.BlockSpec((PEER_HEADS, N_KEYS, tm), lambda i, e: (0, 0, i))
    return pl.pallas_call(
        functools.partial(_peer_dense_kernel, te=te),
        grid=(n // tm, ne // te),
        in_specs=[pl.BlockSpec((tm, d), lambda i, e: (i, 0)), _full((1, d)), blk, blk, blk, blk,
                  pl.BlockSpec((None, te, d), lambda i, e: (layer, e, 0)),
                  pl.BlockSpec((None, d, te), lambda i, e: (layer, 0, e))],
        out_specs=pl.BlockSpec((tm, d), lambda i, e: (i, 0)),
        out_shape=jax.ShapeDtypeStruct((n, d), F32),
        scratch_shapes=[pltpu.VMEM((d, tm), BF16), pltpu.VMEM((d, tm), F32), pltpu.VMEM((te, tm), F32),
                        pltpu.VMEM((te, tm), BF16), pltpu.VMEM((PEER_HEADS, N_KEYS, tm), BF16),
                        pltpu.VMEM((PEER_HEADS, N_KEYS, tm), BF16),
                        pltpu.VMEM((2 * PEER_HEADS * (te // N_KEYS), tm), F32)],
        compiler_params=_params(("parallel", "arbitrary")),
    )(x, g.reshape(1, d), *sel, u, v_t)


def kernel(x_prompt, x_sample, cache_win_k, cache_win_v, cache_conv, state_shift, state_wkv, cache_mem_k, cache_mem_v,
           mem_prompt, g_mix, w_in, att_sinks, w_a_out, conv_dw, conv_db, conv_ln_g, conv_ln_b, w_b_out, b_b_out,
           rk_mu, rk_w0, rk_w2, rk_a0, rk_a2, rk_g2, rk_kk, rk_ka, rk_rk, rk_gn_g, rk_gn_b, w_c_out, w_o,
           g_ca, g_mem, w_cq, w_mk, w_mv, w_co, g_ffn, w_pq, peer_keys, peer_u, peer_v, g_final):
    b, t, d = x_prompt.shape
    db, dt, _ = x_sample.shape
    depth = w_in.shape[0]
    wb = cache_win_k.shape[2]
    mem_len = mem_prompt.shape[1]
    n_p, n_s = b * t, db * dt
    assert t % WINDOW == 0 and wb == WINDOW and dt <= SUBLANES

    xp = x_prompt.reshape(n_p, d)
    xs = x_sample.reshape(n_s, d)
    tabs_p = _rotary_tables(jnp.tile(jnp.arange(t, dtype=F32), b))
    tabs_s = _rotary_tables(jnp.tile(PAST_LEN + jnp.arange(dt, dtype=F32), db))
    tq = _tile(t)
    new_pad = SUBLANES - dt
    u_bf = peer_u.astype(BF16)
    vt_bf = jnp.swapaxes(peer_v.astype(BF16), 1, 2)
    mem = mem_prompt.reshape(b * mem_len, d)
    mem_k_all = cache_mem_k.reshape(depth * db, mem_len, d)
    mem_v_all = cache_mem_v.reshape(depth * db, mem_len, d)
    time_major = lambda a: a.reshape(db, dt, a.shape[-1]).transpose(1, 0, 2)
    seq_major = lambda a: a.transpose(1, 0, 2).reshape(n_s, a.shape[-1])
    k4 = lambda a, nb_, tt_: a.reshape(nb_, tt_, ATT_KV_HEADS, HEAD_DIM)

    outs = [[] for _ in range(12)]
    for l in range(depth):
        lp = {'rk_mu': rk_mu[l], 'rk_w0': rk_w0[l], 'rk_w2': rk_w2[l], 'rk_a0': rk_a0[l], 'rk_a2': rk_a2[l],
              'rk_g2': rk_g2[l], 'rk_kk': rk_kk[l], 'rk_ka': rk_ka[l], 'rk_rk': rk_rk[l].reshape(-1),
              'rk_gn_g': rk_gn_g[l], 'rk_gn_b': rk_gn_b[l], 'w_a_out': w_a_out[l], 'w_b_out': w_b_out[l],
              'b_b_out': b_b_out[l], 'w_c_out': w_c_out[l], 'w_o': w_o[l]}
        win = w_in[l].astype(BF16)
        c0, c1, c2 = A_COLS, A_COLS + B_COLS, A_COLS + B_COLS + C_COLS
        w_seg = [win[:, :c0], win[:, c0:c1], win[:, c1:c2], win[:, c2:]]
        pa_p, pb_p, pc_p, pg_p = (norm_matmul(xp, g_mix[l], w) for w in w_seg)
        pa_s, pb_s, pc_s, pg_s = (norm_matmul(xs, g_mix[l], w) for w in w_seg)

        q_p, k_p = rotary_qk(pa_p, tabs_p)
        q_s, k_s = rotary_qk(pa_s, tabs_s)
        oa_p = swa_prompt(q_p, k_p, pa_p, att_sinks[l], b, t)
        qs4 = q_s.reshape(db, dt, ATT_KV_HEADS, ATT_GROUP, HEAD_DIM).transpose(0, 2, 3, 1, 4).reshape(
            db, ATT_KV_HEADS, ATT_GROUP * dt, HEAD_DIM)
        k_new = k_s.reshape(db, dt, KV_W)
        v_new = pa_s[:, Q_W + KV_W:].reshape(db, dt, KV_W)
        padn = lambda a: jnp.pad(a, ((0, 0), (0, new_pad), (0, 0)))
        sink_rows = jnp.broadcast_to(
            jnp.repeat(att_sinks[l].reshape(ATT_KV_HEADS, ATT_GROUP), dt, axis=1)[:, :, None],
            (ATT_KV_HEADS, ATT_GROUP * dt, LANES))
        oa_s = swa_sample(qs4, cache_win_k, cache_win_v, l, padn(k_new), padn(v_new), sink_rows)
        oa_s = oa_s.reshape(db, ATT_KV_HEADS, ATT_GROUP, dt, HEAD_DIM).transpose(0, 3, 1, 2, 4).reshape(n_s, Q_W)

        yb_p, conv_st = conv_prompt(pb_p, conv_dw[l], conv_db[l], conv_ln_g[l], conv_ln_b[l], b, t)
        yb_s, conv_new = conv_sample(cache_conv[l].transpose(1, 0, 2), time_major(pb_s),
                                     conv_dw[l], conv_db[l], conv_ln_g[l], conv_ln_b[l])
        yb_s = seq_major(yb_s)

        *x5_p, v_p, g_p, bonus_p = rwkv_pre_prompt(pc_p, lp, b, t)
        pre_s = [seq_major(a) for a in rwkv_pre_sample(time_major(pc_s), state_shift[l], lp)]
        *x5_s, v_s, g_s, bonus_s = pre_s
        oc_p, wkv_p = rwkv_scan(x5_p, v_p, jnp.zeros((b, RWKV_HEADS, RWKV_HEAD, RWKV_HEAD), F32), b, t,
                                skewed=scan_skewed(b, t))
        oc_s, wkv_s = rwkv_scan(x5_s, v_s, state_wkv[l], db, dt)

        xp = merge_out(xp, oa_p, yb_p, oc_p, bonus_p, g_p, pg_p, lp)
        xs = merge_out(xs, oa_s, yb_s, oc_s, bonus_s, g_s, pg_s, lp)

        mk = norm_matmul(mem, g_mem[l], w_mk[l].astype(BF16))
        mv = norm_matmul(mem, g_mem[l], w_mv[l].astype(BF16))
        wcq, wco = w_cq[l].astype(BF16), w_co[l].astype(BF16)
        ca_p = cross_attn(norm_matmul(xp, g_ca[l], wcq).reshape(n_p // tq, tq, d), mk.reshape(b, mem_len, d),
                          mv.reshape(b, mem_len, d), t // tq)
        ca_s = cross_attn(norm_matmul(xs, g_ca[l], wcq).reshape(db, dt, d), mem_k_all, mem_v_all, 1, first_seq=l * db,
                          per_step=2 if db % 2 == 0 else 1)
        xp = matmul_residual(ca_p.reshape(n_p, d), wco, xp)
        xs = matmul_residual(ca_s.reshape(n_s, d), wco, xs)

        wpq = w_pq[l].astype(BF16)
        keys = peer_keys[l].reshape(2 * PEER_HEADS, N_KEYS, PEER_HALF).astype(BF16)
        xp = peer_dense(xp, g_ffn[l], peer_select(xp, g_ffn[l], wpq, keys), u_bf, vt_bf, l)
        xs = peer_dense(xs, g_ffn[l], peer_select(xs, g_ffn[l], wpq, keys), u_bf, vt_bf, l)

        outs[0].append(k4(k_p, b, t)[:, t - wb:])
        outs[1].append(k4(pa_p[:, Q_W + KV_W:], b, t)[:, t - wb:])
        outs[2].append(conv_st[:, CONV_HALO - (CONV_WIDTH - 1):])
        outs[3].append(pc_p.reshape(b, t, C_COLS)[:, -1])
        outs[4].append(wkv_p)
        outs[5].append(mk.reshape(b, mem_len, CA_HEADS, CA_HEAD_DIM))
        outs[6].append(mv.reshape(b, mem_len, CA_HEADS, CA_HEAD_DIM))
        outs[7].append(jnp.concatenate([cache_win_k[l][:, dt:], k4(k_new, db, dt)], axis=1))
        outs[8].append(jnp.concatenate([cache_win_v[l][:, dt:], k4(v_new, db, dt)], axis=1))
        outs[9].append(conv_new.transpose(1, 0, 2))
        outs[10].append(pc_s.reshape(db, dt, C_COLS)[:, -1])
        outs[11].append(wkv_s)

    yp = final_norm(xp, g_final)
    ys = final_norm(xs, g_final)
    return (yp.reshape(b, t, d), ys.reshape(db, dt, d)) + tuple(jnp.stack(o) for o in outs)
```
